```python
import jax, jax.numpy as jnp
from jax import lax
import numpy as np

D_MODEL = 2048
BATCH = 4
SEQ = 2048
DEPTH = 1
DEC_BATCH = 128
DEC_SEQ = 4
PAST_LEN = 16384
PAGE_SIZE = 128

N_META = 16
D_MIX = 2 * D_MODEL
D_SSM = D_MIX // 2
D_CONF = D_MIX - D_SSM
SSM_HEAD_DIM = 64
SSM_HEADS = D_SSM // SSM_HEAD_DIM
SSM_GROUPS = 2
D_STATE = 128
SSM_CONV = 4
SSM_CHUNK = 128
CONV_DIM = D_SSM + 2 * SSM_GROUPS * D_STATE
CONF_KERNEL = 31
CONF_GROUPS = 16
D_FF = 5632
D_IN_PROJ = D_SSM + CONV_DIM + SSM_HEADS + 2 * D_CONF
EPS = 1e-5

kernel_name = 'hymba_ssd_conformer_macaron_step'


def rms_norm(x, g):
    x32 = x.astype(jnp.float32)
    y = x32 * lax.rsqrt(jnp.mean(x32 * x32, axis=-1, keepdims=True) + EPS)
    return (y * g.astype(jnp.float32)).astype(x.dtype)


def group_rms_norm(x, g, n_groups):
    shp = x.shape
    x32 = x.astype(jnp.float32).reshape(shp[:-1] + (n_groups, shp[-1] // n_groups))
    y = x32 * lax.rsqrt(jnp.mean(x32 * x32, axis=-1, keepdims=True) + EPS)
    return (y.reshape(shp) * g.astype(jnp.float32)).astype(x.dtype)


def group_layer_norm(x, g, b, n_groups):
    shp = x.shape
    x32 = x.astype(jnp.float32).reshape(shp[:-1] + (n_groups, shp[-1] // n_groups))
    mu = jnp.mean(x32, axis=-1, keepdims=True)
    var = jnp.mean(jnp.square(x32 - mu), axis=-1, keepdims=True)
    y = ((x32 - mu) * lax.rsqrt(var + EPS)).reshape(shp)
    return (y * g.astype(jnp.float32) + b.astype(jnp.float32)).astype(x.dtype)


def swiglu(x, w_gate, w_up, w_down):
    return (jax.nn.silu(x @ w_gate) * (x @ w_up)) @ w_down


def causal_dwconv(x, buf, w, b):
    k = w.shape[0]
    xp = jnp.concatenate([buf.astype(x.dtype), x], axis=1)
    y = lax.conv_general_dilated(xp, w[:, None, :].astype(x.dtype), window_strides=(1,), padding='VALID',
                                 dimension_numbers=('NWC', 'WIO', 'NWC'), feature_group_count=x.shape[-1])
    return y + b.astype(x.dtype), xp[:, -(k - 1):]


def ssd_chunked(x, dt, A, B, C, h0, chunk):
    bsz, T, H, P = x.shape
    G, N = B.shape[-2:]
    R = H // G
    nc = T // chunk
    a = (dt.astype(jnp.float32) * A).reshape(bsz, nc, chunk, G, R)
    xdt = (x * dt[..., None].astype(x.dtype)).reshape(bsz, nc, chunk, G, R, P)
    Bc = B.reshape(bsz, nc, chunk, G, N)
    Cc = C.reshape(bsz, nc, chunk, G, N)
    a_cs = jnp.cumsum(a, axis=2)
    seg = a_cs[:, :, :, None] - a_cs[:, :, None, :]
    mask = jnp.tril(jnp.ones((chunk, chunk), dtype=bool))[:, :, None, None]
    Lmat = jnp.exp(jnp.where(mask, seg, -jnp.inf))
    CB = jnp.einsum('bclgn,bcsgn->bclsg', Cc, Bc)
    y_diag = jnp.einsum('bclsg,bclsgr,bcsgrp->bclgrp', CB, Lmat, xdt)
    decay_end = jnp.exp(a_cs[:, :, -1:] - a_cs)
    states = jnp.einsum('bclgn,bclgr,bclgrp->bcgrpn', Bc, decay_end, xdt).astype(jnp.float32)
    chunk_decay = jnp.exp(a_cs[:, :, -1])

    def step(h, inp):
        dec, st = inp
        return h * dec[..., None, None] + st, h

    h0r = h0.astype(jnp.float32).reshape(bsz, G, R, P, N)
    h_last, h_prev = lax.scan(step, h0r, (jnp.moveaxis(chunk_decay, 1, 0), jnp.moveaxis(states, 1, 0)))
    h_prev = jnp.moveaxis(h_prev, 0, 1)
    y_off = jnp.einsum('bclgn,bcgrpn,bclgr->bclgrp', Cc, h_prev, jnp.exp(a_cs))
    y = (y_diag + y_off).reshape(bsz, T, H, P)
    return y, h_last.reshape(bsz, H, P, N)


def hybrid_mixer(u, h0, ssm_buf, conf_buf, w_in, ssm_conv_w, ssm_conv_b, ssm_dt_bias, ssm_A_log, ssm_D,
                 ssm_norm, conf_conv_w, conf_conv_b, conf_ln_g, conf_ln_b, w_out, pad_front, chunk):
    bsz, T, _ = u.shape
    s1 = D_SSM
    s2 = s1 + CONV_DIM
    s3 = s2 + SSM_HEADS
    s4 = s3 + D_CONF
    z, xBC, dt_raw, ca, cg = jnp.split(u @ w_in, [s1, s2, s3, s4], axis=-1)
    xBC_c, ssm_buf_new = causal_dwconv(xBC, ssm_buf, ssm_conv_w, ssm_conv_b)
    xBC_c = jax.nn.silu(xBC_c)
    xs, Bm, Cm = jnp.split(xBC_c, [D_SSM, D_SSM + SSM_GROUPS * D_STATE], axis=-1)
    xs = xs.reshape(bsz, T, SSM_HEADS, SSM_HEAD_DIM)
    Bm = Bm.reshape(bsz, T, SSM_GROUPS, D_STATE)
    Cm = Cm.reshape(bsz, T, SSM_GROUPS, D_STATE)
    dt = jax.nn.softplus(dt_raw.astype(jnp.float32) + ssm_dt_bias.astype(jnp.float32))
    A = -jnp.exp(ssm_A_log.astype(jnp.float32))
    pad_end = (-(pad_front + T)) % chunk
    pw = ((0, 0), (pad_front, pad_end))
    y, h_new = ssd_chunked(jnp.pad(xs, pw + ((0, 0), (0, 0))), jnp.pad(dt, pw + ((0, 0),)), A,
                           jnp.pad(Bm, pw + ((0, 0), (0, 0))), jnp.pad(Cm, pw + ((0, 0), (0, 0))), h0, chunk)
    y = y[:, pad_front:pad_front + T] + xs * ssm_D[:, None].astype(xs.dtype)
    y = y.reshape(bsz, T, D_SSM).astype(u.dtype) * jax.nn.silu(z)
    y_ssm = group_rms_norm(y, ssm_norm, SSM_GROUPS)
    glu = ca * jax.nn.sigmoid(cg)
    cc, conf_buf_new = causal_dwconv(glu, conf_buf, conf_conv_w, conf_conv_b)
    y_conf = jax.nn.silu(group_layer_norm(cc, conf_ln_g, conf_ln_b, CONF_GROUPS))
    out = jnp.concatenate([y_ssm, y_conf], axis=-1) @ w_out
    return out, h_new, ssm_buf_new, conf_buf_new


def setup_inputs(seed: int = 0) -> dict:
    key = jax.random.key(seed)
    ks = jax.random.split(key, 32)
    f32 = jnp.float32

    def nrm(k, shape, scale):
        return jax.random.normal(k, shape, f32) * scale

    dt0 = jnp.exp(jax.random.uniform(ks[10], (DEPTH, SSM_HEADS), f32, np.log(1e-3), np.log(1e-1)))
    return {
        'x_prompt': nrm(ks[0], (BATCH, SEQ, D_MODEL), 1.0),
        'x_sample': nrm(ks[1], (DEC_BATCH, DEC_SEQ, D_MODEL), 1.0),
        'state_ssm': nrm(ks[2], (DEPTH, DEC_BATCH, SSM_HEADS, SSM_HEAD_DIM, D_STATE), 0.5),
        'state_ssm_conv': nrm(ks[3], (DEPTH, DEC_BATCH, SSM_CONV - 1, CONV_DIM), 1.0),
        'state_conf_conv': nrm(ks[4], (DEPTH, DEC_BATCH, CONF_KERNEL - 1, D_CONF), 0.5),
        'meta_tokens': nrm(ks[5], (N_META, D_MODEL), 1.0),
        'ffn1_norm': 1.0 + nrm(ks[6], (DEPTH, D_MODEL), 0.01),
        'ffn1_w_gate': nrm(ks[7], (DEPTH, D_MODEL, D_FF), D_MODEL ** -0.5),
        'ffn1_w_up': nrm(ks[8], (DEPTH, D_MODEL, D_FF), D_MODEL ** -0.5),
        'ffn1_w_down': nrm(ks[9], (DEPTH, D_FF, D_MODEL), D_FF ** -0.5),
        'mix_norm': 1.0 + nrm(ks[11], (DEPTH, D_MODEL), 0.01),
        'w_in': nrm(ks[12], (DEPTH, D_MODEL, D_IN_PROJ), D_MODEL ** -0.5),
        'ssm_conv_w': nrm(ks[13], (DEPTH, SSM_CONV, CONV_DIM), SSM_CONV ** -0.5),
        'ssm_conv_b': nrm(ks[14], (DEPTH, CONV_DIM), 0.01),
        'ssm_dt_bias': dt0 + jnp.log(-jnp.expm1(-dt0)),
        'ssm_A_log': jnp.log(jax.random.uniform(ks[15], (DEPTH, SSM_HEADS), f32, 1.0, 16.0)),
        'ssm_D': 1.0 + nrm(ks[16], (DEPTH, SSM_HEADS), 0.1),
        'ssm_norm': 1.0 + nrm(ks[17], (DEPTH, D_SSM), 0.01),
        'conf_conv_w': nrm(ks[18], (DEPTH, CONF_KERNEL, D_CONF), CONF_KERNEL ** -0.5),
        'conf_conv_b': nrm(ks[19], (DEPTH, D_CONF), 0.01),
        'conf_ln_g': 1.0 + nrm(ks[20], (DEPTH, D_CONF), 0.01),
        'conf_ln_b': nrm(ks[21], (DEPTH, D_CONF), 0.01),
        'w_out': nrm(ks[22], (DEPTH, D_MIX, D_MODEL), D_MIX ** -0.5),
        'ffn2_norm': 1.0 + nrm(ks[23], (DEPTH, D_MODEL), 0.01),
        'ffn2_w_gate': nrm(ks[24], (DEPTH, D_MODEL, D_FF), D_MODEL ** -0.5),
        'ffn2_w_up': nrm(ks[25], (DEPTH, D_MODEL, D_FF), D_MODEL ** -0.5),
        'ffn2_w_down': nrm(ks[26], (DEPTH, D_FF, D_MODEL), D_FF ** -0.5),
        'final_norm': 1.0 + nrm(ks[27], (D_MODEL,), 0.01),
    }


def reference(x_prompt, x_sample, state_ssm, state_ssm_conv, state_conf_conv, meta_tokens,
              ffn1_norm, ffn1_w_gate, ffn1_w_up, ffn1_w_down, mix_norm, w_in, ssm_conv_w, ssm_conv_b,
              ssm_dt_bias, ssm_A_log, ssm_D, ssm_norm, conf_conv_w, conf_conv_b, conf_ln_g, conf_ln_b,
              w_out, ffn2_norm, ffn2_w_gate, ffn2_w_up, ffn2_w_down, final_norm):

    def trunk(x, st_ssm, st_sconv, st_cconv, pad_front, chunk):
        hs, scs, ccs = [], [], []
        for l in range(DEPTH):
            h = x + 0.5 * swiglu(rms_norm(x, ffn1_norm[l]), ffn1_w_gate[l], ffn1_w_up[l], ffn1_w_down[l])
            m, h_new, sc_new, cc_new = hybrid_mixer(
                rms_norm(h, mix_norm[l]), st_ssm[l], st_sconv[l], st_cconv[l], w_in[l], ssm_conv_w[l],
                ssm_conv_b[l], ssm_dt_bias[l], ssm_A_log[l], ssm_D[l], ssm_norm[l], conf_conv_w[l],
                conf_conv_b[l], conf_ln_g[l], conf_ln_b[l], w_out[l], pad_front, chunk)
            h = h + m
            x = h + 0.5 * swiglu(rms_norm(h, ffn2_norm[l]), ffn2_w_gate[l], ffn2_w_up[l], ffn2_w_down[l])
            hs.append(h_new)
            scs.append(sc_new)
            ccs.append(cc_new)
        return rms_norm(x, final_norm), jnp.stack(hs, 0), jnp.stack(scs, 0), jnp.stack(ccs, 0)

    bp = x_prompt.shape[0]
    dtp = x_prompt.dtype
    xp = jnp.concatenate([jnp.broadcast_to(meta_tokens[None].astype(dtp), (bp, N_META, D_MODEL)), x_prompt], axis=1)
    z_ssm = jnp.zeros((DEPTH, bp, SSM_HEADS, SSM_HEAD_DIM, D_STATE), jnp.float32)
    z_sconv = jnp.zeros((DEPTH, bp, SSM_CONV - 1, CONV_DIM), dtp)
    z_cconv = jnp.zeros((DEPTH, bp, CONF_KERNEL - 1, D_CONF), dtp)
    yp, p_ssm, p_sconv, p_cconv = trunk(xp, z_ssm, z_sconv, z_cconv, (-N_META) % SSM_CHUNK, SSM_CHUNK)
    y_prompt = yp[:, N_META:]

    y_sample, s_ssm, s_sconv, s_cconv = trunk(x_sample, state_ssm, state_ssm_conv, state_conf_conv, 0, x_sample.shape[1])

    return (y_prompt, y_sample, p_ssm, p_sconv, p_cconv, s_ssm, s_sconv, s_cconv)
```

```python
import functools

import jax
import jax.numpy as jnp
from jax import lax
from jax.experimental import pallas as pl
from jax.experimental.pallas import tpu as pltpu

f32 = jnp.float32
bf16 = jnp.bfloat16

D_MODEL = 2048
N_META = 16
D_SSM = 2048
D_CONF = 2048
HEAD_DIM = 64
N_HEADS = 32
N_GROUPS = 2
D_STATE = 128
SSM_CONV = 4
CONV_DIM = D_SSM + 2 * N_GROUPS * D_STATE
CONF_KERNEL = 31
CONF_GROUP = 128
D_FF = 5632
EPS = 1e-5

LANES = 128
SUBLANES = 8
CHUNK = 128
VMEM_LIMIT = 56 * 1024 * 1024

N_PROMPT_ROWS = 4 * 2048
N_SAMPLE_ROWS = 128 * 4
ROW_SAMPLE = N_PROMPT_ROWS
ROW_META = N_PROMPT_ROWS + N_SAMPLE_ROWS
N_ROWS = ROW_META + CHUNK
TM = 736
TF = 256


def _cparams(sem):
    return pltpu.CompilerParams(dimension_semantics=sem, vmem_limit_bytes=VMEM_LIMIT)


def _sigmoid(x):
    return 1.0 / (1.0 + jnp.exp(-x))


def _silu(x):
    return x * _sigmoid(x)


def _rms(x, g):
    return x * lax.rsqrt(jnp.mean(x * x, axis=-1, keepdims=True) + EPS) * g


def _softplus(x):
    return jnp.maximum(x, 0.0) + jnp.log1p(jnp.exp(-jnp.abs(x)))


def _split3(v):
    hi = v.astype(bf16).astype(f32)
    r1 = v - hi
    mid = r1.astype(bf16).astype(f32)
    return hi, mid, r1 - mid


def _ffn_kernel(x_ref, nw_ref, wg_ref, wu_ref, wd_ref, pw_ref, *refs, emit_res):
    if emit_res:
        y_ref, yn_ref, xn_ref = refs
    else:
        y_ref, xn_ref = refs
        yn_ref = y_ref
    f = pl.program_id(1)

    @pl.when(f == 0)
    def _():
        x = x_ref[...]
        xn_ref[...] = _rms(x, nw_ref[...]).astype(bf16)
        y_ref[...] = x

    xn = xn_ref[...]
    g = jnp.dot(xn, wg_ref[...], preferred_element_type=f32)
    u = jnp.dot(xn, wu_ref[...], preferred_element_type=f32)
    a = (_silu(g) * u).astype(bf16)
    y_ref[...] += 0.5 * jnp.dot(a, wd_ref[...], preferred_element_type=f32)

    @pl.when(f == pl.num_programs(1) - 1)
    def _():
        yn_ref[...] = _rms(y_ref[...], pw_ref[...]).astype(yn_ref.dtype)


def _ffn(x, norm_w, wg, wu, wd, post_w, *, emit_res, post_dtype):
    n_rows = x.shape[0]
    grid = (n_rows // TM, D_FF // TF)
    row_spec = pl.BlockSpec((TM, D_MODEL), lambda i, f: (i, 0))
    vec_spec = pl.BlockSpec((1, D_MODEL), lambda i, f: (0, 0))
    if emit_res:
        out_shape = (jax.ShapeDtypeStruct((n_rows, D_MODEL), f32), jax.ShapeDtypeStruct((n_rows, D_MODEL), post_dtype))
        out_specs = (row_spec, row_spec)
    else:
        out_shape = jax.ShapeDtypeStruct((n_rows, D_MODEL), post_dtype)
        out_specs = row_spec
    return pl.pallas_call(
        functools.partial(_ffn_kernel, emit_res=emit_res),
        grid=grid,
        in_specs=[
            row_spec, vec_spec,
            pl.BlockSpec((D_MODEL, TF), lambda i, f: (0, f)),
            pl.BlockSpec((D_MODEL, TF), lambda i, f: (0, f)),
            pl.BlockSpec((TF, D_MODEL), lambda i, f: (f, 0)),
            vec_spec,
        ],
        out_specs=out_specs,
        out_shape=out_shape,
        scratch_shapes=[pltpu.VMEM((TM, D_MODEL), bf16)],
        compiler_params=_cparams(("arbitrary", "arbitrary")),
        name="ffn",
    )(x, norm_w, wg, wu, wd, post_w)


def _matmul_kernel(a_ref, w_ref, o_ref):
    o_ref[...] = jnp.dot(a_ref[...], w_ref[...], preferred_element_type=f32)


def _matmul(a, w, tn):
    n_rows, k = a.shape
    n = w.shape[1]
    return pl.pallas_call(
        _matmul_kernel,
        grid=(n_rows // TM, n // tn),
        in_specs=[pl.BlockSpec((TM, k), lambda i, j: (i, 0)), pl.BlockSpec((k, tn), lambda i, j: (0, j))],
        out_specs=pl.BlockSpec((TM, tn), lambda i, j: (i, j)),
        out_shape=jax.ShapeDtypeStruct((n_rows, n), f32),
        compiler_params=_cparams(("arbitrary", "arbitrary")),
        name="in_proj",
    )(a, w)


def _outproj_kernel(h_ref, ys_ref, yc_ref, w0_ref, w1_ref, o_ref):
    acc = jnp.dot(ys_ref[...], w0_ref[...], preferred_element_type=f32)
    acc = acc + jnp.dot(yc_ref[...], w1_ref[...], preferred_element_type=f32)
    o_ref[...] = h_ref[...] + acc


def _outproj(h, ys, yc, w_out, tn=1024):
    n_rows = h.shape[0]
    return pl.pallas_call(
        _outproj_kernel,
        grid=(n_rows // TM, D_MODEL // tn),
        in_specs=[
            pl.BlockSpec((TM, tn), lambda i, j: (i, j)),
            pl.BlockSpec((TM, D_SSM), lambda i, j: (i, 0)),
            pl.BlockSpec((TM, D_CONF), lambda i, j: (i, 0)),
            pl.BlockSpec((D_SSM, tn), lambda i, j: (0, j)),
            pl.BlockSpec((D_CONF, tn), lambda i, j: (1, j)),
        ],
        out_specs=pl.BlockSpec((TM, tn), lambda i, j: (i, j)),
        out_shape=jax.ShapeDtypeStruct((n_rows, D_MODEL), f32),
        compiler_params=_cparams(("arbitrary", "arbitrary")),
        name="out_proj",
    )(h, ys, yc, w_out, w_out)


def _gate_norm(y, z, nw):
    y = y * _silu(z)
    half = D_SSM // N_GROUPS
    parts = []
    for g in range(N_GROUPS):
        yg = y[:, g * half:(g + 1) * half]
        ms = jnp.mean(yg * yg, axis=-1, keepdims=True)
        parts.append(yg * lax.rsqrt(ms + EPS) * nw[:, g * half:(g + 1) * half])
    return jnp.concatenate(parts, axis=1)


def _ssd_chunk_kernel(z_ref, xbc_ref, dt_ref, cinit_ref, hinit_ref, cw_ref, cb_ref, dtb_ref, alog_ref,
                      dx_ref, nw_ref, tri_ref, e_ref, *refs, n_valid, transpose_out, aliased):
    if aliased:
        refs = refs[1:]
    y_ref, hout_ref, tail_ref, win_scr, ht_scr, y_scr = refs
    c = pl.program_id(1)

    @pl.when(c == 0)
    def _():
        ht_scr[...] = hinit_ref[...]
        win_scr[0:SUBLANES, :] = cinit_ref[...]

    xbc = xbc_ref[...]
    win_scr[SUBLANES:SUBLANES + CHUNK, :] = xbc
    acc = cb_ref[...] + cw_ref[SSM_CONV - 1:SSM_CONV, :] * xbc
    for j in range(1, SSM_CONV):
        acc = acc + cw_ref[SSM_CONV - 1 - j:SSM_CONV - j, :] * win_scr[SUBLANES - j:SUBLANES - j + CHUNK, :]
    tail_ref[...] = win_scr[n_valid:n_valid + SUBLANES, :]
    win_scr[0:SUBLANES, :] = win_scr[CHUNK:CHUNK + SUBLANES, :]
    xc = _silu(acc)
    b_mats = [xc[:, D_SSM + g * D_STATE:D_SSM + (g + 1) * D_STATE] for g in range(N_GROUPS)]
    c_mats = [xc[:, D_SSM + (N_GROUPS + g) * D_STATE:D_SSM + (N_GROUPS + g + 1) * D_STATE] for g in range(N_GROUPS)]

    rows = lax.broadcasted_iota(jnp.int32, (CHUNK, LANES), 0)
    lanes = lax.broadcasted_iota(jnp.int32, (CHUNK, LANES), 1)
    dt = _softplus(dt_ref[...] + dtb_ref[...])
    if n_valid < CHUNK:
        dt = jnp.where(rows < n_valid, dt, 0.0)
    a = dt * (-jnp.exp(alog_ref[...]))
    acs = jnp.dot(tri_ref[...], a, precision=lax.Precision.HIGHEST, preferred_element_type=f32)
    total = acs[CHUNK - 1:CHUNK, :]
    dend = jnp.exp(total - acs)
    acs_t = acs.T
    dt_t = dt.T
    w_t = (dt * dend).T
    hi, mid, lo = _split3(jnp.exp(total))
    rid = lax.broadcasted_iota(jnp.int32, (SUBLANES, LANES), 0)
    cd3 = jnp.where(rid == 0, hi, jnp.where(rid == 1, mid, jnp.where(rid == 2, lo, 0.0)))
    cdx = jnp.sum(jnp.dot(cd3.astype(bf16), e_ref[...], preferred_element_type=f32), axis=0, keepdims=True)

    cb_mats, bt_mats = [], []
    for g in range(N_GROUPS):
        cb_mats.append(lax.dot_general(c_mats[g].astype(bf16), b_mats[g].astype(bf16), (((1,), (1,)), ((), ())),
                                       preferred_element_type=f32))
        bt_mats.append(b_mats[g].T)

    lane_lo = lanes < HEAD_DIM
    tril = rows >= lanes
    heads_per_group = N_HEADS // N_GROUPS
    for j in range(N_HEADS // 2):
        sl = slice(j * LANES, (j + 1) * LANES)
        xs_pair = xc[:, sl]
        ht_pair = ht_scr[:, sl]
        y_pair = jnp.zeros((CHUNK, LANES), f32)
        upd = jnp.zeros((D_STATE, LANES), f32)
        for half in range(2):
            h = 2 * j + half
            g = h // heads_per_group
            keep = lane_lo if half == 0 else jnp.logical_not(lane_lo)
            xs_m = jnp.where(keep, xs_pair, 0.0).astype(bf16)
            ht_m = jnp.where(keep, ht_pair, 0.0).astype(bf16)
            acs_col = jnp.broadcast_to(acs[:, h:h + 1], (CHUNK, CHUNK))
            acs_row = jnp.broadcast_to(acs_t[h:h + 1, :], (CHUNK, CHUNK))
            lmat = jnp.exp(jnp.where(tril, acs_col - acs_row, -jnp.inf))
            m_h = cb_mats[g] * lmat * dt_t[h:h + 1, :]
            ce_h = c_mats[g] * jnp.exp(acs_col)
            lhs = jnp.concatenate([m_h, ce_h], axis=1).astype(bf16)
            rhs = jnp.concatenate([xs_m, ht_m], axis=0)
            y_pair = y_pair + jnp.dot(lhs, rhs, preferred_element_type=f32)
            bw = (bt_mats[g] * w_t[h:h + 1, :]).astype(bf16)
            upd = upd + jnp.dot(bw, xs_m, preferred_element_type=f32)
        y_scr[:, sl] = y_pair
        ht_scr[:, sl] = ht_pair * cdx[:, sl] + upd

    xs = xc[:, :D_SSM]
    y = y_scr[...] + xs * dx_ref[...]
    y_ref[...] = _gate_norm(y, z_ref[...], nw_ref[...]).astype(bf16)

    @pl.when(c == pl.num_programs(1) - 1)
    def _():
        if transpose_out:
            hout_ref[...] = ht_scr[...].T
        else:
            hout_ref[...] = ht_scr[...]


def _ssd_chunks(pz, pxbc, pdt, cinit, hinit, ssm_w, consts, y_prev, *, n_seq, n_chunks, row_block0, n_valid,
                transpose_out):
    cw, cb, dtb, alog, dx, nw = ssm_w
    tri, e_mat = consts
    aliased = y_prev is not None
    rows_map = lambda b, c: (row_block0 + b * n_chunks + c, 0)
    fixed = lambda b, c: (0, 0)
    in_specs = [
        pl.BlockSpec((CHUNK, D_SSM), rows_map),
        pl.BlockSpec((CHUNK, CONV_DIM), rows_map),
        pl.BlockSpec((CHUNK, LANES), rows_map),
        pl.BlockSpec((SUBLANES, CONV_DIM), fixed),
        pl.BlockSpec((D_STATE, D_SSM), fixed),
        pl.BlockSpec((SSM_CONV, CONV_DIM), fixed),
        pl.BlockSpec((1, CONV_DIM), fixed),
        pl.BlockSpec((1, LANES), fixed),
        pl.BlockSpec((1, LANES), fixed),
        pl.BlockSpec((1, D_SSM), fixed),
        pl.BlockSpec((1, D_SSM), fixed),
        pl.BlockSpec((CHUNK, CHUNK), fixed),
        pl.BlockSpec((LANES, D_SSM), fixed),
    ]
    args = [pz, pxbc, pdt, cinit, hinit, cw, cb, dtb, alog, dx, nw, tri, e_mat]
    aliases = {}
    if aliased:
        in_specs.append(pl.BlockSpec(memory_space=pl.ANY))
        aliases = {len(args): 0}
        args.append(y_prev)
    hshape = (n_seq, D_SSM, D_STATE) if transpose_out else (n_seq, D_STATE, D_SSM)
    return pl.pallas_call(
        functools.partial(_ssd_chunk_kernel, n_valid=n_valid, transpose_out=transpose_out, aliased=aliased),
        grid=(n_seq, n_chunks),
        in_specs=in_specs,
        out_specs=(
            pl.BlockSpec((CHUNK, D_SSM), rows_map),
            pl.BlockSpec((None,) + hshape[1:], lambda b, c: (b, 0, 0)),
            pl.BlockSpec((None, SUBLANES, CONV_DIM), lambda b, c: (b, 0, 0)),
        ),
        out_shape=(
            jax.ShapeDtypeStruct((N_ROWS, D_SSM), bf16),
            jax.ShapeDtypeStruct(hshape, f32),
            jax.ShapeDtypeStruct((n_seq, SUBLANES, CONV_DIM), f32),
        ),
        scratch_shapes=[
            pltpu.VMEM((SUBLANES + CHUNK, CONV_DIM), f32),
            pltpu.VMEM((D_STATE, D_SSM), f32),
            pltpu.VMEM((CHUNK, D_SSM), f32),
        ],
        input_output_aliases=aliases,
        compiler_params=_cparams(("arbitrary", "arbitrary")),
        name="ssd_chunks",
    )(*args)


SEQ_TILE = SUBLANES
TOK0 = SEQ_TILE - 4
SSD_BS = 8


def _ssd_sample_kernel(z_ref, xbc_ref, dt_ref, h0_ref, cw_ref, cb_ref, dtb_ref, alog_ref, dx_ref, nw_ref,
                       e_ref, sel_ref, yprev_ref, y_ref, hnew_ref, yoff_scr, lhs_scr, rhs_scr):
    del yprev_ref
    n_rows = SSD_BS * SEQ_TILE
    half = D_SSM // N_GROUPS
    x8 = xbc_ref[...]
    acc = cb_ref[...] + cw_ref[SSM_CONV - 1:SSM_CONV, :] * x8
    for j in range(1, SSM_CONV):
        acc = acc + cw_ref[SSM_CONV - 1 - j:SSM_CONV - j, :] * pltpu.roll(x8, j, axis=0)
    xc = _silu(acc)
    xs = xc[:, :D_SSM]
    bm = xc[:, D_SSM:D_SSM + N_GROUPS * D_STATE]
    cm = xc[:, D_SSM + N_GROUPS * D_STATE:]

    pos = lax.broadcasted_iota(jnp.int32, (n_rows, LANES), 0) % SEQ_TILE
    lanes = lax.broadcasted_iota(jnp.int32, (n_rows, LANES), 1)
    tok = pos >= TOK0
    dt = jnp.where(tok, _softplus(dt_ref[...] + dtb_ref[...]), 0.0)
    a = dt * (-jnp.exp(alog_ref[...]))
    acs = a
    for d in range(1, 4):
        acs = acs + pltpu.roll(a, d, axis=0)
    tot = jnp.sum(a.reshape(SSD_BS, SEQ_TILE, LANES), axis=1, keepdims=True)
    tot = jnp.broadcast_to(tot, (SSD_BS, SEQ_TILE, LANES)).reshape(n_rows, LANES)
    dend = jnp.where(tok, jnp.exp(tot - acs), 0.0)
    eacs = jnp.where(tok, jnp.exp(acs), 0.0)
    cdec = jnp.exp(tot)

    def expand(v):
        hi, mid, lo = _split3(v)
        stack = jnp.concatenate([hi, mid, lo], axis=0).astype(bf16)
        r = jnp.dot(stack, e_ref[...], preferred_element_type=f32)
        return r[0:n_rows] + r[n_rows:2 * n_rows] + r[2 * n_rows:3 * n_rows]

    heads_per_group = N_HEADS // N_GROUPS
    y = expand(eacs)
    for s in range(SSD_BS):
        for g in range(N_GROUPS):
            c8 = cm[s * SEQ_TILE:(s + 1) * SEQ_TILE, g * D_STATE:(g + 1) * D_STATE].astype(bf16)
            hg = h0_ref[s, g * half:(g + 1) * half, :].astype(bf16)
            yoff_scr[s * SEQ_TILE:(s + 1) * SEQ_TILE, g * half:(g + 1) * half] = lax.dot_general(
                c8, hg, (((1,), (1,)), ((), ())), preferred_element_type=f32)
    y = y * yoff_scr[...] + xs * dx_ref[...]
    for d in range(4):
        b_sh = pltpu.roll(bm, d, axis=0) if d else bm
        prod = cm * b_sh
        cbs = [jnp.sum(prod[:, g * D_STATE:(g + 1) * D_STATE], axis=-1, keepdims=True) for g in range(N_GROUPS)]
        cbh = jnp.where(lanes < heads_per_group, cbs[0], cbs[1])
        if d:
            coef = cbh * jnp.exp(acs - pltpu.roll(acs, d, axis=0)) * pltpu.roll(dt, d, axis=0)
        else:
            coef = cbh * dt
        coef = jnp.where(pos >= TOK0 + d, coef, 0.0)
        y = y + expand(coef) * (pltpu.roll(xs, d, axis=0) if d else xs)

    yn = _gate_norm(y, z_ref[...], nw_ref[...]).astype(bf16)
    y_ref[...] = jnp.dot(sel_ref[...], yn, preferred_element_type=f32).astype(bf16)

    xw = xs * expand(dt * dend)
    hi, mid, lo = _split3(expand(cdec))
    pos_w = lax.broadcasted_iota(jnp.int32, (n_rows, D_SSM), 0) % SEQ_TILE
    lhs_scr[...] = jnp.where(pos_w == 0, hi, jnp.where(pos_w == 1, mid, jnp.where(pos_w == 2, lo, xw)))
    ones = jnp.where(pos < 3, 1.0, 0.0)
    for g in range(N_GROUPS):
        rhs_scr[:, 2 * g * D_STATE:(2 * g + 1) * D_STATE] = jnp.where(tok, bm[:, g * D_STATE:(g + 1) * D_STATE], 0.0)
        rhs_scr[:, (2 * g + 1) * D_STATE:(2 * g + 2) * D_STATE] = ones
    for s in range(SSD_BS):
        for g in range(N_GROUPS):
            l8 = lhs_scr[s * SEQ_TILE:(s + 1) * SEQ_TILE, g * half:(g + 1) * half].astype(bf16)
            r8 = rhs_scr[s * SEQ_TILE:(s + 1) * SEQ_TILE, 2 * g * D_STATE:(2 * g + 2) * D_STATE].astype(bf16)
            res = lax.dot_general(l8, r8, (((0,), (0,)), ((), ())), preferred_element_type=f32)
            hnew_ref[s, g * half:(g + 1) * half, :] = (
                h0_ref[s, g * half:(g + 1) * half, :] * res[:, D_STATE:] + res[:, :D_STATE])


def _ssd_sample(pz8, pxbc8, pdt8, h0, ssm_w, e_mat, sel, y_prev):
    cw, cb, dtb, alog, dx, nw = ssm_w
    n_seq = h0.shape[0]
    n_rows = SSD_BS * SEQ_TILE
    n_out = SSD_BS * 4
    rows_map = lambda i: (i, 0)
    fixed = lambda i: (0, 0)
    return pl.pallas_call(
        _ssd_sample_kernel,
        grid=(n_seq // SSD_BS,),
        in_specs=[
            pl.BlockSpec((n_rows, D_SSM), rows_map),
            pl.BlockSpec((n_rows, CONV_DIM), rows_map),
            pl.BlockSpec((n_rows, LANES), rows_map),
            pl.BlockSpec((SSD_BS, D_SSM, D_STATE), lambda i: (i, 0, 0)),
            pl.BlockSpec((SSM_CONV, CONV_DIM), fixed),
            pl.BlockSpec((1, CONV_DIM), fixed),
            pl.BlockSpec((1, LANES), fixed),
            pl.BlockSpec((1, LANES), fixed),
            pl.BlockSpec((1, D_SSM), fixed),
            pl.BlockSpec((1, D_SSM), fixed),
            pl.BlockSpec((LANES, D_SSM), fixed),
            pl.BlockSpec((n_out, n_rows), fixed),
            pl.BlockSpec(memory_space=pl.ANY),
        ],
        out_specs=(
            pl.BlockSpec((n_out, D_SSM), lambda i: (ROW_SAMPLE // n_out + i, 0)),
            pl.BlockSpec((SSD_BS, D_SSM, D_STATE), lambda i: (i, 0, 0)),
        ),
        out_shape=(
            jax.ShapeDtypeStruct((N_ROWS, D_SSM), bf16),
            jax.ShapeDtypeStruct((n_seq, D_SSM, D_STATE), f32),
        ),
        scratch_shapes=[
            pltpu.VMEM((n_rows, D_SSM), f32),
            pltpu.VMEM((n_rows, D_SSM), f32),
            pltpu.VMEM((n_rows, 2 * N_GROUPS * D_STATE), f32),
        ],
        input_output_aliases={12: 0},
        compiler_params=_cparams(("arbitrary",)),
        name="ssd_sample",
    )(pz8, pxbc8, pdt8, h0, cw, cb, dtb, alog, dx, nw, e_mat, sel, y_prev)


N_LT = D_CONF // LANES
HIST = 32
CONF_RC = 64


def _conf_norm_act(acc, g, beta):
    mu = jnp.mean(acc, axis=-1, keepdims=True)
    xc = acc - mu
    var = jnp.mean(xc * xc, axis=-1, keepdims=True)
    return _silu(xc * lax.rsqrt(var + EPS) * g + beta)


def _conf_tile_kernel(ca_ref, cg_ref, hist_ref, w_ref, p_ref, *refs, tt, n_valid, aliased):
    if aliased:
        refs = refs[1:]
    y_ref, tail_ref, s_scr, y_scr = refs
    t = pl.program_id(1)

    @pl.when(t == 0)
    def _():
        for lt in range(N_LT):
            s_scr[lt, 0:HIST, :] = hist_ref[:, lt * LANES:(lt + 1) * LANES]

    @pl.when(t > 0)
    def _():
        s_scr[:, 0:HIST, :] = s_scr[:, tt:tt + HIST, :]

    glu = ca_ref[...] * _sigmoid(cg_ref[...])
    for lt in range(N_LT):
        s_scr[lt, HIST:HIST + tt, :] = glu[:, lt * LANES:(lt + 1) * LANES]

    off = HIST - (CONF_KERNEL - 1)

    def lane_tile(lt, carry):
        for r0 in range(0, tt, CONF_RC):
            acc = jnp.broadcast_to(p_ref[lt, 0:1, :], (CONF_RC, LANES))
            for k in range(CONF_KERNEL):
                acc = acc + w_ref[lt, k:k + 1, :] * s_scr[lt, r0 + off + k:r0 + off + k + CONF_RC, :]
            y_scr[lt, r0:r0 + CONF_RC, :] = _conf_norm_act(acc, p_ref[lt, 1:2, :], p_ref[lt, 2:3, :])
        return carry

    lax.fori_loop(0, N_LT, lane_tile, 0)
    for lt in range(N_LT):
        y_ref[:, lt * LANES:(lt + 1) * LANES] = y_scr[lt].astype(bf16)
        tail_ref[:, lt * LANES:(lt + 1) * LANES] = s_scr[lt, n_valid:n_valid + HIST, :]


def _conf_tiles(pc, hist, conf_w, y_prev, *, n_seq, n_tiles, tt, row_block0, n_valid):
    w3, p3 = conf_w
    aliased = y_prev is not None
    in_specs = [
        pl.BlockSpec((tt, D_CONF), lambda b, t: (row_block0 + b * n_tiles + t, 0)),
        pl.BlockSpec((tt, D_CONF), lambda b, t: (row_block0 + b * n_tiles + t, 1)),
        pl.BlockSpec((HIST, D_CONF), lambda b, t: (0, 0)),
        pl.BlockSpec((N_LT, HIST, LANES), lambda b, t: (0, 0, 0)),
        pl.BlockSpec((N_LT, SUBLANES, LANES), lambda b, t: (0, 0, 0)),
    ]
    args = [pc, pc, hist, w3, p3]
    aliases = {}
    if aliased:
        in_specs.append(pl.BlockSpec(memory_space=pl.ANY))
        aliases = {len(args): 0}
        args.append(y_prev)
    return pl.pallas_call(
        functools.partial(_conf_tile_kernel, tt=tt, n_valid=n_valid, aliased=aliased),
        grid=(n_seq, n_tiles),
        in_specs=in_specs,
        out_specs=(
            pl.BlockSpec((tt, D_CONF), lambda b, t: (row_block0 + b * n_tiles + t, 0)),
            pl.BlockSpec((None, HIST, D_CONF), lambda b, t: (b, 0, 0)),
        ),
        out_shape=(
            jax.ShapeDtypeStruct((N_ROWS, D_CONF), bf16),
            jax.ShapeDtypeStruct((n_seq, HIST, D_CONF), f32),
        ),
        scratch_shapes=[
            pltpu.VMEM((N_LT, HIST + tt, LANES), f32),
            pltpu.VMEM((N_LT, tt, LANES), f32),
        ],
        input_output_aliases=aliases,
        compiler_params=_cparams(("arbitrary", "arbitrary")),
        name="conf_tiles",
    )(*args)


CONF_BS = 8
WIN = 40


def _conf_sample_kernel(ca_ref, cg_ref, buf_ref, w_ref, p_ref, sel_ref, yprev_ref, y_ref, newbuf_ref, w_scr, y8_scr):
    del yprev_ref
    n_hist = CONF_KERNEL - 1
    glu = ca_ref[...] * _sigmoid(cg_ref[...])
    for s in range(CONF_BS):
        w_scr[s, 0:SUBLANES, :] = jnp.zeros((SUBLANES, D_CONF), f32)
        w_scr[s, WIN - SEQ_TILE:WIN, :] = glu[s * SEQ_TILE:(s + 1) * SEQ_TILE, :]
        w_scr[s, WIN - 4 - n_hist:WIN - 4, :] = buf_ref[s]
        newbuf_ref[s] = w_scr[s, WIN - n_hist:WIN, :]
    off = WIN - SEQ_TILE - n_hist

    def one_seq(s, carry):
        row0 = pl.multiple_of(s * SEQ_TILE, SEQ_TILE)
        for lt in range(N_LT):
            sl = slice(lt * LANES, (lt + 1) * LANES)
            acc = jnp.broadcast_to(p_ref[lt, 0:1, :], (SEQ_TILE, LANES))
            for k in range(CONF_KERNEL):
                acc = acc + w_ref[lt, k:k + 1, :] * w_scr[s, off + k:off + k + SEQ_TILE, sl]
            y8_scr[pl.ds(row0, SEQ_TILE), sl] = _conf_norm_act(acc, p_ref[lt, 1:2, :], p_ref[lt, 2:3, :])
        return carry

    lax.fori_loop(0, CONF_BS, one_seq, 0)
    y_ref[...] = jnp.dot(sel_ref[...], y8_scr[...].astype(bf16), preferred_element_type=f32).astype(bf16)


def _conf_sample(pc8, buf, conf_w, sel, y_prev):
    w3, p3 = conf_w
    n_seq = buf.shape[0]
    n_rows = CONF_BS * SEQ_TILE
    n_out = CONF_BS * 4
    n_hist = CONF_KERNEL - 1
    return pl.pallas_call(
        _conf_sample_kernel,
        grid=(n_seq // CONF_BS,),
        in_specs=[
            pl.BlockSpec((n_rows, D_CONF), lambda i: (i, 0)),
            pl.BlockSpec((n_rows, D_CONF), lambda i: (i, 1)),
            pl.BlockSpec((CONF_BS, n_hist, D_CONF), lambda i: (i, 0, 0)),
            pl.BlockSpec((N_LT, HIST, LANES), lambda i: (0, 0, 0)),
            pl.BlockSpec((N_LT, SUBLANES, LANES), lambda i: (0, 0, 0)),
            pl.BlockSpec((n_out, n_rows), lambda i: (0, 0)),
            pl.BlockSpec(memory_space=pl.ANY),
        ],
        out_specs=(
            pl.BlockSpec((n_out, D_CONF), lambda i: (ROW_SAMPLE // n_out + i, 0)),
            pl.BlockSpec((CONF_BS, n_hist, D_CONF), lambda i: (i, 0, 0)),
        ),
        out_shape=(
            jax.ShapeDtypeStruct((N_ROWS, D_CONF), bf16),
            jax.ShapeDtypeStruct((n_seq, n_hist, D_CONF), f32),
        ),
        scratch_shapes=[
            pltpu.VMEM((CONF_BS, WIN, D_CONF), f32),
            pltpu.VMEM((n_rows, D_CONF), f32),
        ],
        input_output_aliases={6: 0},
        compiler_params=_cparams(("arbitrary",)),
        name="conf_sample",
    )(pc8, pc8, buf, w3, p3, sel, y_prev)


def _seq_tiles(rows, n_seq, n_tok):
    w = rows.shape[-1]
    return jnp.pad(rows.reshape(n_seq, n_tok, w), ((0, 0), (SEQ_TILE - n_tok, 0), (0, 0))).reshape(n_seq * SEQ_TILE, w)


def kernel(x_prompt, x_sample, state_ssm, state_ssm_conv, state_conf_conv, meta_tokens, ffn1_norm, ffn1_w_gate, ffn1_w_up, ffn1_w_down, mix_norm, w_in, ssm_conv_w, ssm_conv_b, ssm_dt_bias, ssm_A_log, ssm_D, ssm_norm, conf_conv_w, conf_conv_b, conf_ln_g, conf_ln_b, w_out, ffn2_norm, ffn2_w_gate, ffn2_w_up, ffn2_w_down, final_norm):
    n_pb, n_pt = x_prompt.shape[0], x_prompt.shape[1]
    n_sb, n_st = x_sample.shape[0], x_sample.shape[1]
    assert (n_pb * n_pt, n_sb * n_st, n_st) == (N_PROMPT_ROWS, N_SAMPLE_ROWS, 4) and n_pt % CHUNK == 0

    x_all = jnp.concatenate([
        x_prompt.reshape(N_PROMPT_ROWS, D_MODEL), x_sample.reshape(N_SAMPLE_ROWS, D_MODEL), meta_tokens,
        jnp.zeros((CHUNK - N_META, D_MODEL), f32)], axis=0)

    row = lambda v: v.reshape(1, -1)
    h, u = _ffn(x_all, row(ffn1_norm[0]), ffn1_w_gate[0].astype(bf16), ffn1_w_up[0].astype(bf16),
                ffn1_w_down[0].astype(bf16), row(mix_norm[0]), emit_res=True, post_dtype=bf16)

    s1, s2, s3 = D_SSM, D_SSM + CONV_DIM, D_SSM + CONV_DIM + N_HEADS
    w_in0 = w_in[0]
    pz = _matmul(u, w_in0[:, :s1].astype(bf16), 1024)
    pxbc = _matmul(u, w_in0[:, s1:s2].astype(bf16), 1280)
    pdt = _matmul(u, jnp.pad(w_in0[:, s2:s3], ((0, 0), (0, LANES - N_HEADS))).astype(bf16), LANES)
    pc = _matmul(u, w_in0[:, s3:].astype(bf16), 1024)

    pad_heads = lambda v: jnp.pad(v.reshape(1, N_HEADS), ((0, 0), (0, LANES - N_HEADS)))
    ssm_w = (ssm_conv_w[0], row(ssm_conv_b[0]), pad_heads(ssm_dt_bias[0]), pad_heads(ssm_A_log[0]),
             row(jnp.repeat(ssm_D[0], HEAD_DIM)), row(ssm_norm[0]))
    tri = jnp.tril(jnp.ones((CHUNK, CHUNK), f32))
    e_mat = (jnp.arange(LANES)[:, None] == (jnp.arange(D_SSM)[None, :] // HEAD_DIM)).astype(bf16)
    lane_major = lambda v: v.reshape(v.shape[0], N_LT, LANES).transpose(1, 0, 2)
    conf_w = (lane_major(jnp.pad(conf_conv_w[0], ((0, HIST - CONF_KERNEL), (0, 0)))),
              lane_major(jnp.pad(jnp.stack([conf_conv_b[0], conf_ln_g[0], conf_ln_b[0]]), ((0, SUBLANES - 3), (0, 0)))))
    out_rows = jnp.arange(SSD_BS * 4)
    sel = ((out_rows // 4) * SEQ_TILE + TOK0 + out_rows % 4)[:, None] == jnp.arange(SSD_BS * SEQ_TILE)[None, :]
    sel = sel.astype(bf16)

    meta_block = ROW_META // CHUNK
    ys, h_meta, tail_meta = _ssd_chunks(
        pz, pxbc, pdt, jnp.zeros((SUBLANES, CONV_DIM), f32), jnp.zeros((D_STATE, D_SSM), f32), ssm_w, (tri, e_mat), None,
        n_seq=1, n_chunks=1, row_block0=meta_block, n_valid=N_META, transpose_out=False)
    yc, hist_meta = _conf_tiles(pc, jnp.zeros((HIST, D_CONF), f32), conf_w, None,
                                n_seq=1, n_tiles=1, tt=CHUNK, row_block0=meta_block, n_valid=N_META)

    ys, p_ssm, p_tail = _ssd_chunks(
        pz, pxbc, pdt, tail_meta[0], h_meta[0], ssm_w, (tri, e_mat), ys,
        n_seq=n_pb, n_chunks=n_pt // CHUNK, row_block0=0, n_valid=CHUNK, transpose_out=True)
    conf_tt = 256
    yc, p_hist = _conf_tiles(pc, hist_meta[0], conf_w, yc,
                             n_seq=n_pb, n_tiles=n_pt // conf_tt, tt=conf_tt, row_block0=0, n_valid=conf_tt)

    smp = slice(ROW_SAMPLE, ROW_SAMPLE + N_SAMPLE_ROWS)
    pxbc_s = pxbc[smp].reshape(n_sb, n_st, CONV_DIM)
    pxbc8 = jnp.concatenate([jnp.zeros((n_sb, 1, CONV_DIM), f32), state_ssm_conv[0], pxbc_s], axis=1)
    ys, s_ssm = _ssd_sample(_seq_tiles(pz[smp], n_sb, n_st), pxbc8.reshape(n_sb * SEQ_TILE, CONV_DIM),
                            _seq_tiles(pdt[smp], n_sb, n_st), state_ssm[0].reshape(n_sb, D_SSM, D_STATE),
                            ssm_w, e_mat, sel, ys)
    yc, s_cconv = _conf_sample(_seq_tiles(pc[smp], n_sb, n_st), state_conf_conv[0], conf_w, sel, yc)

    h2 = _outproj(h, ys, yc, w_out[0].astype(bf16))
    out = _ffn(h2, row(ffn2_norm[0]), ffn2_w_gate[0].astype(bf16), ffn2_w_up[0].astype(bf16),
               ffn2_w_down[0].astype(bf16), row(final_norm), emit_res=False, post_dtype=f32)

    y_prompt = out[:N_PROMPT_ROWS].reshape(n_pb, n_pt, D_MODEL)
    y_sample = out[smp].reshape(n_sb, n_st, D_MODEL)
    n_hist = CONF_KERNEL - 1
    return (
        y_prompt, y_sample,
        p_ssm.reshape(1, n_pb, N_HEADS, HEAD_DIM, D_STATE),
        p_tail[:, SUBLANES - (SSM_CONV - 1):, :][None],
        p_hist[:, HIST - n_hist:, :][None],
        s_ssm.reshape(1, n_sb, N_HEADS, HEAD_DIM, D_STATE),
        pxbc_s[:, n_st - (SSM_CONV - 1):, :][None],
        s_cconv[None],
    )
```

```python
import functools

import jax
import jax.numpy as jnp
from jax import lax
from jax.experimental import pallas as pl
from jax.experimental.pallas import tpu as pltpu

f32 = jnp.float32
bf16 = jnp.bfloat16

D_MODEL = 2048
N_META = 16
D_SSM = 2048
D_CONF = 2048
HEAD_DIM = 64
N_HEADS = 32
N_GROUPS = 2
D_STATE = 128
SSM_CONV = 4
CONV_DIM = D_SSM + 2 * N_GROUPS * D_STATE
CONF_KERNEL = 31
CONF_GROUP = 128
D_FF = 5632
EPS = 1e-5

LANES = 128
SUBLANES = 8
CHUNK = 128
VMEM_LIMIT = 56 * 1024 * 1024

N_PROMPT_ROWS = 4 * 2048
N_SAMPLE_ROWS = 128 * 4
ROW_SAMPLE = N_PROMPT_ROWS
ROW_META = N_PROMPT_ROWS + N_SAMPLE_ROWS
N_ROWS = ROW_META + CHUNK
TM = 736
TF = 256


def _cparams(sem):
    return pltpu.CompilerParams(dimension_semantics=sem, vmem_limit_bytes=VMEM_LIMIT)


def _sigmoid(x):
    return 1.0 / (1.0 + jnp.exp(-x))


def _silu(x):
    return x * _sigmoid(x)


def _rms(x, g):
    return x * lax.rsqrt(jnp.mean(x * x, axis=-1, keepdims=True) + EPS) * g


def _softplus(x):
    return jnp.maximum(x, 0.0) + jnp.log1p(jnp.exp(-jnp.abs(x)))


def _split3(v):
    hi = v.astype(bf16).astype(f32)
    r1 = v - hi
    mid = r1.astype(bf16).astype(f32)
    return hi, mid, r1 - mid


def _row_segments(src_rows, n_total, tm):
    bounds, start = [], 0
    for k, n in enumerate(src_rows):
        bounds.append((k, start, n))
        start += n
    if start < n_total:
        bounds.append((None, start, n_total - start))
    table = []
    for t in range(n_total // tm):
        lo, hi = t * tm, (t + 1) * tm
        segs = []
        for k, s, n in bounds:
            a, b = max(lo, s), min(hi, s + n)
            if a < b:
                segs.append((k, a - s, a - lo, b - a))
        table.append(segs)
    return table


def _fill_rows(i, seg_table, groups, sem):
    for t, segs in enumerate(seg_table):
        @pl.when(i == t)
        def _(segs=segs):
            copies = []
            for srcs, dst_ref in groups:
                for k, r0, d0, n in segs:
                    if k is None:
                        dst_ref[d0:d0 + n, :] = jnp.zeros((n, dst_ref.shape[1]), dst_ref.dtype)
                    else:
                        cp = pltpu.make_async_copy(srcs[k].at[pl.ds(r0, n)], dst_ref.at[pl.ds(d0, n)], sem)
                        cp.start()
                        copies.append(cp)
            for cp in copies:
                cp.wait()


def _ffn_kernel(*refs, seg_table, n_src, emit_res):
    srcs = refs[:n_src]
    nw_ref, wg_ref, wu_ref, wd_ref, pw_ref = refs[n_src:n_src + 5]
    if emit_res:
        y_ref, yn_ref, xn_ref, sem = refs[n_src + 5:]
    else:
        y_ref, xn_ref, sem = refs[n_src + 5:]
        yn_ref = y_ref
    f = pl.program_id(1)

    @pl.when(f == 0)
    def _():
        _fill_rows(pl.program_id(0), seg_table, [(srcs, y_ref)], sem)
        xn_ref[...] = _rms(y_ref[...], nw_ref[...]).astype(bf16)

    xn = xn_ref[...]
    g = jnp.dot(xn, wg_ref[...].astype(bf16), preferred_element_type=f32)
    u = jnp.dot(xn, wu_ref[...].astype(bf16), preferred_element_type=f32)
    a = (_silu(g) * u).astype(bf16)
    y_ref[...] += 0.5 * jnp.dot(a, wd_ref[...].astype(bf16), preferred_element_type=f32)

    @pl.when(f == pl.num_programs(1) - 1)
    def _():
        yn_ref[...] = _rms(y_ref[...], pw_ref[...]).astype(yn_ref.dtype)


def _ffn(srcs, norm_w, wg, wu, wd, post_w, *, emit_res, post_dtype):
    seg_table = _row_segments([s.shape[0] for s in srcs], N_ROWS, TM)
    grid = (N_ROWS // TM, D_FF // TF)
    row_spec = pl.BlockSpec((TM, D_MODEL), lambda i, f: (i, 0))
    vec_spec = pl.BlockSpec((1, D_MODEL), lambda i, f: (0, 0))
    if emit_res:
        out_shape = (jax.ShapeDtypeStruct((N_ROWS, D_MODEL), f32), jax.ShapeDtypeStruct((N_ROWS, D_MODEL), post_dtype))
        out_specs = (row_spec, row_spec)
    else:
        out_shape = jax.ShapeDtypeStruct((N_ROWS, D_MODEL), post_dtype)
        out_specs = row_spec
    return pl.pallas_call(
        functools.partial(_ffn_kernel, seg_table=seg_table, n_src=len(srcs), emit_res=emit_res),
        grid=grid,
        in_specs=[pl.BlockSpec(memory_space=pl.ANY)] * len(srcs) + [
            vec_spec,
            pl.BlockSpec((None, D_MODEL, TF), lambda i, f: (0, 0, f)),
            pl.BlockSpec((None, D_MODEL, TF), lambda i, f: (0, 0, f)),
            pl.BlockSpec((None, TF, D_MODEL), lambda i, f: (0, f, 0)),
            vec_spec,
        ],
        out_specs=out_specs,
        out_shape=out_shape,
        scratch_shapes=[pltpu.VMEM((TM, D_MODEL), bf16), pltpu.SemaphoreType.DMA(())],
        compiler_params=_cparams(("arbitrary", "arbitrary")),
        name="ffn",
    )(*srcs, norm_w, wg, wu, wd, post_w)


def _matmul_kernel(a_ref, w_ref, o_ref):
    o_ref[...] = jnp.dot(a_ref[...], w_ref[...].astype(bf16), preferred_element_type=f32)


def _matmul(a, w, *, n, tn, col0=0):
    n_rows, k = a.shape
    assert col0 % tn == 0 and n % tn == 0
    if w.ndim == 3:
        w_spec = pl.BlockSpec((None, k, tn), lambda i, j: (0, 0, col0 // tn + j))
    else:
        w_spec = pl.BlockSpec((k, tn), lambda i, j: (0, col0 // tn + j))
    return pl.pallas_call(
        _matmul_kernel,
        grid=(n_rows // TM, n // tn),
        in_specs=[pl.BlockSpec((TM, k), lambda i, j: (i, 0)), w_spec],
        out_specs=pl.BlockSpec((TM, tn), lambda i, j: (i, j)),
        out_shape=jax.ShapeDtypeStruct((n_rows, n), f32),
        compiler_params=_cparams(("arbitrary", "arbitrary")),
        name="in_proj",
    )(a, w)


def _outproj_kernel(*refs, seg_table, n_src):
    ys_srcs, yc_srcs = refs[:n_src], refs[n_src:2 * n_src]
    h_ref, w0_ref, w1_ref, o_ref, ys_scr, yc_scr, sem = refs[2 * n_src:]

    @pl.when(pl.program_id(1) == 0)
    def _():
        _fill_rows(pl.program_id(0), seg_table, [(ys_srcs, ys_scr), (yc_srcs, yc_scr)], sem)

    acc = jnp.dot(ys_scr[...], w0_ref[...].astype(bf16), preferred_element_type=f32)
    acc = acc + jnp.dot(yc_scr[...], w1_ref[...].astype(bf16), preferred_element_type=f32)
    o_ref[...] = h_ref[...] + acc


def _outproj(h, ys_srcs, yc_srcs, w_out, tn=512):
    n_rows = h.shape[0]
    seg_table = _row_segments([s.shape[0] for s in ys_srcs], n_rows, TM)
    n_src = len(ys_srcs)
    return pl.pallas_call(
        functools.partial(_outproj_kernel, seg_table=seg_table, n_src=n_src),
        grid=(n_rows // TM, D_MODEL // tn),
        in_specs=[pl.BlockSpec(memory_space=pl.ANY)] * (2 * n_src) + [
            pl.BlockSpec((TM, tn), lambda i, j: (i, j)),
            pl.BlockSpec((None, D_SSM, tn), lambda i, j: (0, 0, j)),
            pl.BlockSpec((None, D_CONF, tn), lambda i, j: (0, 1, j)),
        ],
        out_specs=pl.BlockSpec((TM, tn), lambda i, j: (i, j)),
        out_shape=jax.ShapeDtypeStruct((n_rows, D_MODEL), f32),
        scratch_shapes=[pltpu.VMEM((TM, D_SSM), bf16), pltpu.VMEM((TM, D_CONF), bf16), pltpu.SemaphoreType.DMA(())],
        compiler_params=_cparams(("arbitrary", "arbitrary")),
        name="out_proj",
    )(*ys_srcs, *yc_srcs, h, w_out, w_out)


def _gate_norm(y, z, nw):
    y = y * _silu(z)
    half = D_SSM // N_GROUPS
    parts = []
    for g in range(N_GROUPS):
        yg = y[:, g * half:(g + 1) * half]
        ms = jnp.mean(yg * yg, axis=-1, keepdims=True)
        parts.append(yg * lax.rsqrt(ms + EPS) * nw[:, g * half:(g + 1) * half])
    return jnp.concatenate(parts, axis=1)


def _ssd_chunk_kernel(z_ref, xs_ref, bc_ref, dt_ref, cinit_ref, hinit_ref, cw_ref, cb_ref, dtb_ref, alog_ref,
                      dx_ref, nw_ref, tri_ref, e_ref, y_ref, hout_ref, tail_ref, win_scr, ht_scr, y_scr,
                      *, n_valid, transpose_out):
    c = pl.program_id(1)

    @pl.when(c == 0)
    def _():
        ht_scr[...] = hinit_ref[...]
        win_scr[0:SUBLANES, :] = cinit_ref[...]

    xbc = jnp.concatenate([xs_ref[...], bc_ref[...]], axis=1)
    win_scr[SUBLANES:SUBLANES + CHUNK, :] = xbc
    acc = cb_ref[...] + cw_ref[SSM_CONV - 1:SSM_CONV, :] * xbc
    for j in range(1, SSM_CONV):
        acc = acc + cw_ref[SSM_CONV - 1 - j:SSM_CONV - j, :] * win_scr[SUBLANES - j:SUBLANES - j + CHUNK, :]
    tail_ref[...] = win_scr[n_valid:n_valid + SUBLANES, :]
    win_scr[0:SUBLANES, :] = win_scr[CHUNK:CHUNK + SUBLANES, :]
    xc = _silu(acc)
    b_mats = [xc[:, D_SSM + g * D_STATE:D_SSM + (g + 1) * D_STATE] for g in range(N_GROUPS)]
    c_mats = [xc[:, D_SSM + (N_GROUPS + g) * D_STATE:D_SSM + (N_GROUPS + g + 1) * D_STATE] for g in range(N_GROUPS)]

    rows = lax.broadcasted_iota(jnp.int32, (CHUNK, LANES), 0)
    lanes = lax.broadcasted_iota(jnp.int32, (CHUNK, LANES), 1)
    dt = _softplus(dt_ref[...] + dtb_ref[...])
    if n_valid < CHUNK:
        dt = jnp.where(rows < n_valid, dt, 0.0)
    a = dt * (-jnp.exp(alog_ref[...]))
    acs = jnp.dot(tri_ref[...], a, precision=lax.Precision.HIGHEST, preferred_element_type=f32)
    total = acs[CHUNK - 1:CHUNK, :]
    dend = jnp.exp(total - acs)
    acs_t = acs.T
    dt_t = dt.T
    w_t = (dt * dend).T
    hi, mid, lo = _split3(jnp.exp(total))
    rid = lax.broadcasted_iota(jnp.int32, (SUBLANES, LANES), 0)
    cd3 = jnp.where(rid == 0, hi, jnp.where(rid == 1, mid, jnp.where(rid == 2, lo, 0.0)))
    cdx = jnp.sum(jnp.dot(cd3.astype(bf16), e_ref[...], preferred_element_type=f32), axis=0, keepdims=True)

    cb_mats, bt_mats = [], []
    for g in range(N_GROUPS):
        cb_mats.append(lax.dot_general(c_mats[g].astype(bf16), b_mats[g].astype(bf16), (((1,), (1,)), ((), ())),
                                       preferred_element_type=f32))
        bt_mats.append(b_mats[g].T)

    lane_lo = lanes < HEAD_DIM
    tril = rows >= lanes
    heads_per_group = N_HEADS // N_GROUPS
    for j in range(N_HEADS // 2):
        sl = slice(j * LANES, (j + 1) * LANES)
        xs_pair = xc[:, sl]
        ht_pair = ht_scr[:, sl]
        y_pair = jnp.zeros((CHUNK, LANES), f32)
        upd = jnp.zeros((D_STATE, LANES), f32)
        for half in range(2):
            h = 2 * j + half
            g = h // heads_per_group
            keep = lane_lo if half == 0 else jnp.logical_not(lane_lo)
            xs_m = jnp.where(keep, xs_pair, 0.0).astype(bf16)
            ht_m = jnp.where(keep, ht_pair, 0.0).astype(bf16)
            acs_col = jnp.broadcast_to(acs[:, h:h + 1], (CHUNK, CHUNK))
            acs_row = jnp.broadcast_to(acs_t[h:h + 1, :], (CHUNK, CHUNK))
            lmat = jnp.exp(jnp.where(tril, acs_col - acs_row, -jnp.inf))
            m_h = cb_mats[g] * lmat * dt_t[h:h + 1, :]
            ce_h = c_mats[g] * jnp.exp(acs_col)
            lhs = jnp.concatenate([m_h, ce_h], axis=1).astype(bf16)
            rhs = jnp.concatenate([xs_m, ht_m], axis=0)
            y_pair = y_pair + jnp.dot(lhs, rhs, preferred_element_type=f32)
            bw = (bt_mats[g] * w_t[h:h + 1, :]).astype(bf16)
            upd = upd + jnp.dot(bw, xs_m, preferred_element_type=f32)
        y_scr[:, sl] = y_pair
        ht_scr[:, sl] = ht_pair * cdx[:, sl] + upd

    xs = xc[:, :D_SSM]
    y = y_scr[...] + xs * dx_ref[...]
    y_ref[...] = _gate_norm(y, z_ref[...], nw_ref[...]).astype(bf16)

    @pl.when(c == pl.num_programs(1) - 1)
    def _():
        if transpose_out:
            hout_ref[...] = ht_scr[...].T
        else:
            hout_ref[...] = ht_scr[...]


def _ssd_chunks(pzx, pdt, cinit, hinit, ssm_w, consts, *, n_seq, n_chunks, row_block0, n_valid, transpose_out):
    cw, cb, dtb, alog, dx, nw = ssm_w
    tri, e_mat = consts
    bc_dim = CONV_DIM - D_SSM
    rows_map = lambda b, c: (row_block0 + b * n_chunks + c, 0)
    fixed = lambda b, c: (0, 0)
    in_specs = [
        pl.BlockSpec((CHUNK, D_SSM), rows_map),
        pl.BlockSpec((CHUNK, D_SSM), lambda b, c: (row_block0 + b * n_chunks + c, 1)),
        pl.BlockSpec((CHUNK, bc_dim), lambda b, c: (row_block0 + b * n_chunks + c, 2 * D_SSM // bc_dim)),
        pl.BlockSpec((CHUNK, LANES), rows_map),
        pl.BlockSpec((SUBLANES, CONV_DIM), fixed),
        pl.BlockSpec((D_STATE, D_SSM), fixed),
        pl.BlockSpec((SSM_CONV, CONV_DIM), fixed),
        pl.BlockSpec((1, CONV_DIM), fixed),
        pl.BlockSpec((1, LANES), fixed),
        pl.BlockSpec((1, LANES), fixed),
        pl.BlockSpec((1, D_SSM), fixed),
        pl.BlockSpec((1, D_SSM), fixed),
        pl.BlockSpec((CHUNK, CHUNK), fixed),
        pl.BlockSpec((LANES, D_SSM), fixed),
    ]
    args = [pzx, pzx, pzx, pdt, cinit, hinit, cw, cb, dtb, alog, dx, nw, tri, e_mat]
    hshape = (n_seq, D_SSM, D_STATE) if transpose_out else (n_seq, D_STATE, D_SSM)
    return pl.pallas_call(
        functools.partial(_ssd_chunk_kernel, n_valid=n_valid, transpose_out=transpose_out),
        grid=(n_seq, n_chunks),
        in_specs=in_specs,
        out_specs=(
            pl.BlockSpec((CHUNK, D_SSM), lambda b, c: (b * n_chunks + c, 0)),
            pl.BlockSpec((None,) + hshape[1:], lambda b, c: (b, 0, 0)),
            pl.BlockSpec((None, SUBLANES, CONV_DIM), lambda b, c: (b, 0, 0)),
        ),
        out_shape=(
            jax.ShapeDtypeStruct((n_seq * n_chunks * CHUNK, D_SSM), bf16),
            jax.ShapeDtypeStruct(hshape, f32),
            jax.ShapeDtypeStruct((n_seq, SUBLANES, CONV_DIM), f32),
        ),
        scratch_shapes=[
            pltpu.VMEM((SUBLANES + CHUNK, CONV_DIM), f32),
            pltpu.VMEM((D_STATE, D_SSM), f32),
            pltpu.VMEM((CHUNK, D_SSM), f32),
        ],
        compiler_params=_cparams(("arbitrary", "arbitrary")),
        name="ssd_chunks",
    )(*args)


SEQ_TILE = SUBLANES
TOK0 = SEQ_TILE - 4
SSD_BS = 8


def _ssd_sample_kernel(z_ref, xbc_ref, dt_ref, h0_ref, cw_ref, cb_ref, dtb_ref, alog_ref, dx_ref, nw_ref,
                       e_ref, sel_ref, y_ref, hnew_ref, yoff_scr, lhs_scr, rhs_scr):
    n_rows = SSD_BS * SEQ_TILE
    half = D_SSM // N_GROUPS
    x8 = xbc_ref[...]
    acc = cb_ref[...] + cw_ref[SSM_CONV - 1:SSM_CONV, :] * x8
    for j in range(1, SSM_CONV):
        acc = acc + cw_ref[SSM_CONV - 1 - j:SSM_CONV - j, :] * pltpu.roll(x8, j, axis=0)
    xc = _silu(acc)
    xs = xc[:, :D_SSM]
    bm = xc[:, D_SSM:D_SSM + N_GROUPS * D_STATE]
    cm = xc[:, D_SSM + N_GROUPS * D_STATE:]

    pos = lax.broadcasted_iota(jnp.int32, (n_rows, LANES), 0) % SEQ_TILE
    lanes = lax.broadcasted_iota(jnp.int32, (n_rows, LANES), 1)
    tok = pos >= TOK0
    dt = jnp.where(tok, _softplus(dt_ref[...] + dtb_ref[...]), 0.0)
    a = dt * (-jnp.exp(alog_ref[...]))
    acs = a
    for d in range(1, 4):
        acs = acs + pltpu.roll(a, d, axis=0)
    tot = jnp.sum(a.reshape(SSD_BS, SEQ_TILE, LANES), axis=1, keepdims=True)
    tot = jnp.broadcast_to(tot, (SSD_BS, SEQ_TILE, LANES)).reshape(n_rows, LANES)
    dend = jnp.where(tok, jnp.exp(tot - acs), 0.0)
    eacs = jnp.where(tok, jnp.exp(acs), 0.0)
    cdec = jnp.exp(tot)

    def expand(v):
        hi, mid, lo = _split3(v)
        stack = jnp.concatenate([hi, mid, lo], axis=0).astype(bf16)
        r = jnp.dot(stack, e_ref[...], preferred_element_type=f32)
        return r[0:n_rows] + r[n_rows:2 * n_rows] + r[2 * n_rows:3 * n_rows]

    heads_per_group = N_HEADS // N_GROUPS
    y = expand(eacs)
    for s in range(SSD_BS):
        for g in range(N_GROUPS):
            c8 = cm[s * SEQ_TILE:(s + 1) * SEQ_TILE, g * D_STATE:(g + 1) * D_STATE].astype(bf16)
            hg = h0_ref[s, g * half:(g + 1) * half, :].astype(bf16)
            yoff_scr[s * SEQ_TILE:(s + 1) * SEQ_TILE, g * half:(g + 1) * half] = lax.dot_general(
                c8, hg, (((1,), (1,)), ((), ())), preferred_element_type=f32)
    y = y * yoff_scr[...] + xs * dx_ref[...]
    for d in range(4):
        b_sh = pltpu.roll(bm, d, axis=0) if d else bm
        prod = cm * b_sh
        cbs = [jnp.sum(prod[:, g * D_STATE:(g + 1) * D_STATE], axis=-1, keepdims=True) for g in range(N_GROUPS)]
        cbh = jnp.where(lanes < heads_per_group, cbs[0], cbs[1])
        if d:
            coef = cbh * jnp.exp(acs - pltpu.roll(acs, d, axis=0)) * pltpu.roll(dt, d, axis=0)
        else:
            coef = cbh * dt
        coef = jnp.where(pos >= TOK0 + d, coef, 0.0)
        y = y + expand(coef) * (pltpu.roll(xs, d, axis=0) if d else xs)

    yn = _gate_norm(y, z_ref[...], nw_ref[...]).astype(bf16)
    y_ref[...] = jnp.dot(sel_ref[...], yn, preferred_element_type=f32).astype(bf16)

    xw = xs * expand(dt * dend)
    hi, mid, lo = _split3(expand(cdec))
    pos_w = lax.broadcasted_iota(jnp.int32, (n_rows, D_SSM), 0) % SEQ_TILE
    lhs_scr[...] = jnp.where(pos_w == 0, hi, jnp.where(pos_w == 1, mid, jnp.where(pos_w == 2, lo, xw)))
    ones = jnp.where(pos < 3, 1.0, 0.0)
    for g in range(N_GROUPS):
        rhs_scr[:, 2 * g * D_STATE:(2 * g + 1) * D_STATE] = jnp.where(tok, bm[:, g * D_STATE:(g + 1) * D_STATE], 0.0)
        rhs_scr[:, (2 * g + 1) * D_STATE:(2 * g + 2) * D_STATE] = ones
    for s in range(SSD_BS):
        for g in range(N_GROUPS):
            l8 = lhs_scr[s * SEQ_TILE:(s + 1) * SEQ_TILE, g * half:(g + 1) * half].astype(bf16)
            r8 = rhs_scr[s * SEQ_TILE:(s + 1) * SEQ_TILE, 2 * g * D_STATE:(2 * g + 2) * D_STATE].astype(bf16)
            res = lax.dot_general(l8, r8, (((0,), (0,)), ((), ())), preferred_element_type=f32)
            hnew_ref[s, g * half:(g + 1) * half, :] = (
                h0_ref[s, g * half:(g + 1) * half, :] * res[:, D_STATE:] + res[:, :D_STATE])


def _ssd_sample(pz8, pxbc8, pdt8, h0, ssm_w, e_mat, sel):
    cw, cb, dtb, alog, dx, nw = ssm_w
    n_seq = h0.shape[0]
    n_rows = SSD_BS * SEQ_TILE
    n_out = SSD_BS * 4
    rows_map = lambda i: (i, 0)
    fixed = lambda i: (0, 0)
    return pl.pallas_call(
        _ssd_sample_kernel,
        grid=(n_seq // SSD_BS,),
        in_specs=[
            pl.BlockSpec((n_rows, D_SSM), rows_map),
            pl.BlockSpec((n_rows, CONV_DIM), rows_map),
            pl.BlockSpec((n_rows, LANES), rows_map),
            pl.BlockSpec((SSD_BS, D_SSM, D_STATE), lambda i: (i, 0, 0)),
            pl.BlockSpec((SSM_CONV, CONV_DIM), fixed),
            pl.BlockSpec((1, CONV_DIM), fixed),
            pl.BlockSpec((1, LANES), fixed),
            pl.BlockSpec((1, LANES), fixed),
            pl.BlockSpec((1, D_SSM), fixed),
            pl.BlockSpec((1, D_SSM), fixed),
            pl.BlockSpec((LANES, D_SSM), fixed),
            pl.BlockSpec((n_out, n_rows), fixed),
        ],
        out_specs=(
            pl.BlockSpec((n_out, D_SSM), rows_map),
            pl.BlockSpec((SSD_BS, D_SSM, D_STATE), lambda i: (i, 0, 0)),
        ),
        out_shape=(
            jax.ShapeDtypeStruct((n_seq * 4, D_SSM), bf16),
            jax.ShapeDtypeStruct((n_seq, D_SSM, D_STATE), f32),
        ),
        scratch_shapes=[
            pltpu.VMEM((n_rows, D_SSM), f32),
            pltpu.VMEM((n_rows, D_SSM), f32),
            pltpu.VMEM((n_rows, 2 * N_GROUPS * D_STATE), f32),
        ],
        compiler_params=_cparams(("arbitrary",)),
        name="ssd_sample",
    )(pz8, pxbc8, pdt8, h0, cw, cb, dtb, alog, dx, nw, e_mat, sel)


N_LT = D_CONF // LANES
HIST = 32
CONF_RC = 64


def _conf_norm_act(acc, g, beta):
    mu = jnp.mean(acc, axis=-1, keepdims=True)
    xc = acc - mu
    var = jnp.mean(xc * xc, axis=-1, keepdims=True)
    return _silu(xc * lax.rsqrt(var + EPS) * g + beta)


def _conf_tile_kernel(ca_ref, cg_ref, hist_ref, w_ref, p_ref, y_ref, tail_ref, s_scr, y_scr, *, tt, n_valid):
    t = pl.program_id(1)

    @pl.when(t == 0)
    def _():
        for lt in range(N_LT):
            s_scr[lt, 0:HIST, :] = hist_ref[:, lt * LANES:(lt + 1) * LANES]

    @pl.when(t > 0)
    def _():
        s_scr[:, 0:HIST, :] = s_scr[:, tt:tt + HIST, :]

    glu = ca_ref[...] * _sigmoid(cg_ref[...])
    for lt in range(N_LT):
        s_scr[lt, HIST:HIST + tt, :] = glu[:, lt * LANES:(lt + 1) * LANES]

    off = HIST - (CONF_KERNEL - 1)

    def lane_tile(lt, carry):
        for r0 in range(0, tt, CONF_RC):
            acc = jnp.broadcast_to(p_ref[lt, 0:1, :], (CONF_RC, LANES))
            for k in range(CONF_KERNEL):
                acc = acc + w_ref[lt, k:k + 1, :] * s_scr[lt, r0 + off + k:r0 + off + k + CONF_RC, :]
            y_scr[lt, r0:r0 + CONF_RC, :] = _conf_norm_act(acc, p_ref[lt, 1:2, :], p_ref[lt, 2:3, :])
        return carry

    lax.fori_loop(0, N_LT, lane_tile, 0)
    for lt in range(N_LT):
        y_ref[:, lt * LANES:(lt + 1) * LANES] = y_scr[lt].astype(bf16)
        tail_ref[:, lt * LANES:(lt + 1) * LANES] = s_scr[lt, n_valid:n_valid + HIST, :]


def _conf_tiles(pc, hist, conf_w, *, n_seq, n_tiles, tt, row_block0, n_valid):
    w3, p3 = conf_w
    in_specs = [
        pl.BlockSpec((tt, D_CONF), lambda b, t: (row_block0 + b * n_tiles + t, 0)),
        pl.BlockSpec((tt, D_CONF), lambda b, t: (row_block0 + b * n_tiles + t, 1)),
        pl.BlockSpec((HIST, D_CONF), lambda b, t: (0, 0)),
        pl.BlockSpec((N_LT, HIST, LANES), lambda b, t: (0, 0, 0)),
        pl.BlockSpec((N_LT, SUBLANES, LANES), lambda b, t: (0, 0, 0)),
    ]
    args = [pc, pc, hist, w3, p3]
    return pl.pallas_call(
        functools.partial(_conf_tile_kernel, tt=tt, n_valid=n_valid),
        grid=(n_seq, n_tiles),
        in_specs=in_specs,
        out_specs=(
            pl.BlockSpec((tt, D_CONF), lambda b, t: (b * n_tiles + t, 0)),
            pl.BlockSpec((None, HIST, D_CONF), lambda b, t: (b, 0, 0)),
        ),
        out_shape=(
            jax.ShapeDtypeStruct((n_seq * n_tiles * tt, D_CONF), bf16),
            jax.ShapeDtypeStruct((n_seq, HIST, D_CONF), f32),
        ),
        scratch_shapes=[
            pltpu.VMEM((N_LT, HIST + tt, LANES), f32),
            pltpu.VMEM((N_LT, tt, LANES), f32),
        ],
        compiler_params=_cparams(("arbitrary", "arbitrary")),
        name="conf_tiles",
    )(*args)


CONF_BS = 8
WIN = 40


def _conf_sample_kernel(ca_ref, cg_ref, buf_ref, w_ref, p_ref, sel_ref, y_ref, newbuf_ref, w_scr, y8_scr):
    n_hist = CONF_KERNEL - 1
    glu = ca_ref[...] * _sigmoid(cg_ref[...])
    for s in range(CONF_BS):
        w_scr[s, 0:SUBLANES, :] = jnp.zeros((SUBLANES, D_CONF), f32)
        w_scr[s, WIN - SEQ_TILE:WIN, :] = glu[s * SEQ_TILE:(s + 1) * SEQ_TILE, :]
        w_scr[s, WIN - 4 - n_hist:WIN - 4, :] = buf_ref[s]
        newbuf_ref[s] = w_scr[s, WIN - n_hist:WIN, :]
    off = WIN - SEQ_TILE - n_hist

    def one_seq(s, carry):
        row0 = pl.multiple_of(s * SEQ_TILE, SEQ_TILE)
        for lt in range(N_LT):
            sl = slice(lt * LANES, (lt + 1) * LANES)
            acc = jnp.broadcast_to(p_ref[lt, 0:1, :], (SEQ_TILE, LANES))
            for k in range(CONF_KERNEL):
                acc = acc + w_ref[lt, k:k + 1, :] * w_scr[s, off + k:off + k + SEQ_TILE, sl]
            y8_scr[pl.ds(row0, SEQ_TILE), sl] = _conf_norm_act(acc, p_ref[lt, 1:2, :], p_ref[lt, 2:3, :])
        return carry

    lax.fori_loop(0, CONF_BS, one_seq, 0)
    y_ref[...] = jnp.dot(sel_ref[...], y8_scr[...].astype(bf16), preferred_element_type=f32).astype(bf16)


def _conf_sample(pc8, buf, conf_w, sel):
    w3, p3 = conf_w
    n_seq = buf.shape[0]
    n_rows = CONF_BS * SEQ_TILE
    n_out = CONF_BS * 4
    n_hist = CONF_KERNEL - 1
    return pl.pallas_call(
        _conf_sample_kernel,
        grid=(n_seq // CONF_BS,),
        in_specs=[
            pl.BlockSpec((n_rows, D_CONF), lambda i: (i, 0)),
            pl.BlockSpec((n_rows, D_CONF), lambda i: (i, 1)),
            pl.BlockSpec((CONF_BS, n_hist, D_CONF), lambda i: (i, 0, 0)),
            pl.BlockSpec((N_LT, HIST, LANES), lambda i: (0, 0, 0)),
            pl.BlockSpec((N_LT, SUBLANES, LANES), lambda i: (0, 0, 0)),
            pl.BlockSpec((n_out, n_rows), lambda i: (0, 0)),
        ],
        out_specs=(
            pl.BlockSpec((n_out, D_CONF), lambda i: (i, 0)),
            pl.BlockSpec((CONF_BS, n_hist, D_CONF), lambda i: (i, 0, 0)),
        ),
        out_shape=(
            jax.ShapeDtypeStruct((n_seq * 4, D_CONF), bf16),
            jax.ShapeDtypeStruct((n_seq, n_hist, D_CONF), f32),
        ),
        scratch_shapes=[
            pltpu.VMEM((CONF_BS, WIN, D_CONF), f32),
            pltpu.VMEM((n_rows, D_CONF), f32),
        ],
        compiler_params=_cparams(("arbitrary",)),
        name="conf_sample",
    )(pc8, pc8, buf, w3, p3, sel)


def _seq_tiles(rows, n_seq, n_tok):
    w = rows.shape[-1]
    return jnp.pad(rows.reshape(n_seq, n_tok, w), ((0, 0), (SEQ_TILE - n_tok, 0), (0, 0))).reshape(n_seq * SEQ_TILE, w)


def kernel(x_prompt, x_sample, state_ssm, state_ssm_conv, state_conf_conv, meta_tokens, ffn1_norm, ffn1_w_gate, ffn1_w_up, ffn1_w_down, mix_norm, w_in, ssm_conv_w, ssm_conv_b, ssm_dt_bias, ssm_A_log, ssm_D, ssm_norm, conf_conv_w, conf_conv_b, conf_ln_g, conf_ln_b, w_out, ffn2_norm, ffn2_w_gate, ffn2_w_up, ffn2_w_down, final_norm):
    n_pb, n_pt = x_prompt.shape[0], x_prompt.shape[1]
    n_sb, n_st = x_sample.shape[0], x_sample.shape[1]
    assert (n_pb * n_pt, n_sb * n_st, n_st) == (N_PROMPT_ROWS, N_SAMPLE_ROWS, 4) and n_pt % CHUNK == 0

    row = lambda v: v.reshape(1, -1)
    x_rows = (x_prompt.reshape(N_PROMPT_ROWS, D_MODEL), x_sample.reshape(N_SAMPLE_ROWS, D_MODEL), meta_tokens)
    h, u = _ffn(x_rows, row(ffn1_norm[0]), ffn1_w_gate, ffn1_w_up, ffn1_w_down, row(mix_norm[0]),
                emit_res=True, post_dtype=bf16)

    s2, s3 = D_SSM + CONV_DIM, D_SSM + CONV_DIM + N_HEADS
    pzx = _matmul(u, w_in, n=s2, tn=1536)
    pdt = _matmul(u, w_in, n=LANES, tn=LANES, col0=s2)
    pc = _matmul(u, w_in[0, :, s3:].astype(bf16), n=2 * D_CONF, tn=1024)

    pad_heads = lambda v: jnp.pad(v.reshape(1, N_HEADS), ((0, 0), (0, LANES - N_HEADS)))
    ssm_w = (ssm_conv_w[0], row(ssm_conv_b[0]), pad_heads(ssm_dt_bias[0]), pad_heads(ssm_A_log[0]),
             row(jnp.repeat(ssm_D[0], HEAD_DIM)), row(ssm_norm[0]))
    tri = jnp.tril(jnp.ones((CHUNK, CHUNK), f32))
    e_mat = (jnp.arange(LANES)[:, None] == (jnp.arange(D_SSM)[None, :] // HEAD_DIM)).astype(bf16)
    lane_major = lambda v: v.reshape(v.shape[0], N_LT, LANES).transpose(1, 0, 2)
    conf_w = (lane_major(jnp.pad(conf_conv_w[0], ((0, HIST - CONF_KERNEL), (0, 0)))),
              lane_major(jnp.pad(jnp.stack([conf_conv_b[0], conf_ln_g[0], conf_ln_b[0]]), ((0, SUBLANES - 3), (0, 0)))))
    out_rows = jnp.arange(SSD_BS * 4)
    sel = ((out_rows // 4) * SEQ_TILE + TOK0 + out_rows % 4)[:, None] == jnp.arange(SSD_BS * SEQ_TILE)[None, :]
    sel = sel.astype(bf16)

    meta_block = ROW_META // CHUNK
    ys_m, h_meta, tail_meta = _ssd_chunks(
        pzx, pdt, jnp.zeros((SUBLANES, CONV_DIM), f32), jnp.zeros((D_STATE, D_SSM), f32), ssm_w, (tri, e_mat),
        n_seq=1, n_chunks=1, row_block0=meta_block, n_valid=N_META, transpose_out=False)
    yc_m, hist_meta = _conf_tiles(pc, jnp.zeros((HIST, D_CONF), f32), conf_w,
                                  n_seq=1, n_tiles=1, tt=CHUNK, row_block0=meta_block, n_valid=N_META)

    ys_p, p_ssm, p_tail = _ssd_chunks(
        pzx, pdt, tail_meta[0], h_meta[0], ssm_w, (tri, e_mat),
        n_seq=n_pb, n_chunks=n_pt // CHUNK, row_block0=0, n_valid=CHUNK, transpose_out=True)
    conf_tt = 256
    yc_p, p_hist = _conf_tiles(pc, hist_meta[0], conf_w,
                               n_seq=n_pb, n_tiles=n_pt // conf_tt, tt=conf_tt, row_block0=0, n_valid=conf_tt)

    smp = slice(ROW_SAMPLE, ROW_SAMPLE + N_SAMPLE_ROWS)
    pzx_s = pzx[smp]
    pxbc_s = pzx_s[:, D_SSM:].reshape(n_sb, n_st, CONV_DIM)
    pxbc8 = jnp.concatenate([jnp.zeros((n_sb, 1, CONV_DIM), f32), state_ssm_conv[0], pxbc_s], axis=1)
    ys_s, s_ssm = _ssd_sample(_seq_tiles(pzx_s[:, :D_SSM], n_sb, n_st), pxbc8.reshape(n_sb * SEQ_TILE, CONV_DIM),
                              _seq_tiles(pdt[smp], n_sb, n_st), state_ssm[0].reshape(n_sb, D_SSM, D_STATE),
                              ssm_w, e_mat, sel)
    yc_s, s_cconv = _conf_sample(_seq_tiles(pc[smp], n_sb, n_st), state_conf_conv[0], conf_w, sel)

    h2 = _outproj(h, (ys_p, ys_s, ys_m), (yc_p, yc_s, yc_m), w_out)
    out = _ffn((h2,), row(ffn2_norm[0]), ffn2_w_gate, ffn2_w_up, ffn2_w_down, row(final_norm),
               emit_res=False, post_dtype=f32)

    y_prompt = out[:N_PROMPT_ROWS].reshape(n_pb, n_pt, D_MODEL)
    y_sample = out[smp].reshape(n_sb, n_st, D_MODEL)
    n_hist = CONF_KERNEL - 1
    return (
        y_prompt, y_sample,
        p_ssm.reshape(1, n_pb, N_HEADS, HEAD_DIM, D_STATE),
        p_tail[:, SUBLANES - (SSM_CONV - 1):, :][None],
        p_hist[:, HIST - n_hist:, :][None],
        s_ssm.reshape(1, n_sb, N_HEADS, HEAD_DIM, D_STATE),
        pxbc_s[:, n_st - (SSM_CONV - 1):, :][None],
        s_cconv[None],
    )
```

```python
import functools

import jax
import jax.numpy as jnp
from jax import lax
from jax.experimental import pallas as pl
from jax.experimental.pallas import tpu as pltpu

f32 = jnp.float32
bf16 = jnp.bfloat16

D_MODEL = 2048
N_META = 16
D_SSM = 2048
D_CONF = 2048
HEAD_DIM = 64
N_HEADS = 32
N_GROUPS = 2
D_STATE = 128
SSM_CONV = 4
CONV_DIM = D_SSM + 2 * N_GROUPS * D_STATE
CONF_KERNEL = 31
CONF_GROUP = 128
D_FF = 5632
EPS = 1e-5

LANES = 128
SUBLANES = 8
CHUNK = 128
VMEM_LIMIT = 56 * 1024 * 1024

N_PROMPT_ROWS = 4 * 2048
N_SAMPLE_ROWS = 128 * 4
ROW_SAMPLE = N_PROMPT_ROWS
ROW_META = N_PROMPT_ROWS + N_SAMPLE_ROWS
N_ROWS = ROW_META + CHUNK
TM = 736
TF = 512


def _cparams(sem):
    return pltpu.CompilerParams(dimension_semantics=sem, vmem_limit_bytes=VMEM_LIMIT)


def _sigmoid(x):
    return 1.0 / (1.0 + jnp.exp(-x))


def _silu(x):
    return x * _sigmoid(x)


def _rms(x, g):
    return x * lax.rsqrt(jnp.mean(x * x, axis=-1, keepdims=True) + EPS) * g


def _softplus(x):
    return jnp.maximum(x, 0.0) + jnp.log1p(jnp.exp(-jnp.abs(x)))


def _split3(v):
    hi = v.astype(bf16).astype(f32)
    r1 = v - hi
    mid = r1.astype(bf16).astype(f32)
    return hi, mid, r1 - mid


def _row_segments(src_rows, n_total, tm):
    bounds, start = [], 0
    for k, n in enumerate(src_rows):
        bounds.append((k, start, n))
        start += n
    if start < n_total:
        bounds.append((None, start, n_total - start))
    table = []
    for t in range(n_total // tm):
        lo, hi = t * tm, (t + 1) * tm
        segs = []
        for k, s, n in bounds:
            a, b = max(lo, s), min(hi, s + n)
            if a < b:
                segs.append((k, a - s, a - lo, b - a))
        table.append(segs)
    return table


def _fill_rows(i, seg_table, groups, sem):
    for t, segs in enumerate(seg_table):
        @pl.when(i == t)
        def _(segs=segs):
            copies = []
            for srcs, dst_ref in groups:
                for k, r0, d0, n in segs:
                    if k is None:
                        dst_ref[d0:d0 + n, :] = jnp.zeros((n, dst_ref.shape[1]), dst_ref.dtype)
                    else:
                        cp = pltpu.make_async_copy(srcs[k].at[pl.ds(r0, n)], dst_ref.at[pl.ds(d0, n)], sem)
                        cp.start()
                        copies.append(cp)
            for cp in copies:
                cp.wait()


def _drain_rows(i, out_table, dsts, acc_scr, sems):
    def copies(t):
        return [pltpu.make_async_copy(acc_scr.at[t % 2, pl.ds(s0, n)], dsts[k].at[pl.ds(r0, n)], sems.at[t % 2])
                for k, r0, s0, n in out_table[t] if k is not None]

    n_tiles = len(out_table)
    for t in range(n_tiles):
        @pl.when(i == t)
        def _(t=t):
            if t > 0:
                for cp in copies(t - 1):
                    cp.wait()
            for cp in copies(t):
                cp.start()
            if t == n_tiles - 1:
                for cp in copies(t):
                    cp.wait()


def _ffn_kernel(*refs, seg_table, out_table, n_src):
    srcs = refs[:n_src]
    nw_ref, wg_ref, wu_ref, wd_ref, pw_ref = refs[n_src:n_src + 5]
    i = pl.program_id(0)
    f = pl.program_id(1)
    if out_table is None:
        acc, yn_ref, xn_ref, sem = refs[n_src + 5:]
    else:
        n_dst = 1 + max(k for segs in out_table for k, _, _, _ in segs if k is not None)
        dsts = refs[n_src + 5:n_src + 5 + n_dst]
        acc_scr, xn_ref, sem, out_sems = refs[n_src + 5 + n_dst:]
        acc = acc_scr.at[i % 2]

    @pl.when(f == 0)
    def _():
        _fill_rows(i, seg_table, [(srcs, acc)], sem)
        xn_ref[...] = _rms(acc[...], nw_ref[...]).astype(bf16)

    xn = xn_ref[...]
    g = jnp.dot(xn, wg_ref[...], preferred_element_type=f32)
    u = jnp.dot(xn, wu_ref[...], preferred_element_type=f32)
    a = (_silu(g) * u).astype(bf16)
    acc[...] += 0.5 * jnp.dot(a, wd_ref[...], preferred_element_type=f32)

    @pl.when(f == pl.num_programs(1) - 1)
    def _():
        if out_table is None:
            yn_ref[...] = _rms(acc[...], pw_ref[...]).astype(yn_ref.dtype)
        else:
            acc[...] = _rms(acc[...], pw_ref[...])
            _drain_rows(i, out_table, dsts, acc_scr, out_sems)


def _ffn(srcs, norm_w, wg, wu, wd, post_w, *, dst_rows=None):
    seg_table = _row_segments([s.shape[0] for s in srcs], N_ROWS, TM)
    grid = (N_ROWS // TM, D_FF // TF)
    row_spec = pl.BlockSpec((TM, D_MODEL), lambda i, f: (i, 0))
    vec_spec = pl.BlockSpec((1, D_MODEL), lambda i, f: (0, 0))
    scratch = [pltpu.VMEM((TM, D_MODEL), bf16), pltpu.SemaphoreType.DMA(())]
    if dst_rows is None:
        out_table = None
        out_shape = (jax.ShapeDtypeStruct((N_ROWS, D_MODEL), f32), jax.ShapeDtypeStruct((N_ROWS, D_MODEL), bf16))
        out_specs = (row_spec, row_spec)
    else:
        out_table = _row_segments(dst_rows, N_ROWS, TM)
        out_shape = tuple(jax.ShapeDtypeStruct((n, D_MODEL), f32) for n in dst_rows)
        out_specs = tuple(pl.BlockSpec(memory_space=pl.ANY) for _ in dst_rows)
        scratch = [pltpu.VMEM((2, TM, D_MODEL), f32)] + scratch + [pltpu.SemaphoreType.DMA((2,))]
    return pl.pallas_call(
        functools.partial(_ffn_kernel, seg_table=seg_table, out_table=out_table, n_src=len(srcs)),
        grid=grid,
        in_specs=[pl.BlockSpec(memory_space=pl.ANY)] * len(srcs) + [
            vec_spec,
            pl.BlockSpec((D_MODEL, TF), lambda i, f: (0, f)),
            pl.BlockSpec((D_MODEL, TF), lambda i, f: (0, f)),
            pl.BlockSpec((TF, D_MODEL), lambda i, f: (f, 0)),
            vec_spec,
        ],
        out_specs=out_specs,
        out_shape=out_shape,
        scratch_shapes=scratch,
        compiler_params=_cparams(("arbitrary", "arbitrary")),
        name="ffn",
    )(*srcs, norm_w, wg, wu, wd, post_w)


def _cast_kernel(x_ref, o_ref):
    o_ref[...] = x_ref[...].astype(bf16)


def _to_bf16(w, block_rows):
    _, r, c = w.shape
    return pl.pallas_call(
        _cast_kernel,
        grid=(r // block_rows,),
        in_specs=[pl.BlockSpec((None, block_rows, c), lambda i: (0, i, 0))],
        out_specs=pl.BlockSpec((block_rows, c), lambda i: (i, 0)),
        out_shape=jax.ShapeDtypeStruct((r, c), bf16),
        compiler_params=_cparams(("arbitrary",)),
        name="cast_bf16",
    )(w)


def _matmul_kernel(a_ref, w_ref, o_ref):
    o_ref[...] = jnp.dot(a_ref[...], w_ref[...].astype(bf16), preferred_element_type=f32)


def _matmul_rows(a, w, *, row_block0, n_row_blocks, n, tn, col0=0):
    k = a.shape[1]
    assert col0 % tn == 0 and n % tn == 0
    if w.ndim == 3:
        w_spec = pl.BlockSpec((None, k, tn), lambda j, i: (0, 0, col0 // tn + j))
    else:
        w_spec = pl.BlockSpec((k, tn), lambda j, i: (0, col0 // tn + j))
    return pl.pallas_call(
        _matmul_kernel,
        grid=(n // tn, n_row_blocks),
        in_specs=[pl.BlockSpec((CHUNK, k), lambda j, i: (row_block0 + i, 0)), w_spec],
        out_specs=pl.BlockSpec((CHUNK, tn), lambda j, i: (i, j)),
        out_shape=jax.ShapeDtypeStruct((n_row_blocks * CHUNK, n), f32),
        compiler_params=_cparams(("arbitrary", "arbitrary")),
        name="in_proj_rows",
    )(a, w)


def _outproj_kernel(*refs, seg_table, n_src):
    ys_srcs, yc_srcs = refs[:n_src], refs[n_src:2 * n_src]
    h_ref, w0_ref, w1_ref, o_ref, ys_scr, yc_scr, sem = refs[2 * n_src:]

    @pl.when(pl.program_id(1) == 0)
    def _():
        _fill_rows(pl.program_id(0), seg_table, [(ys_srcs, ys_scr), (yc_srcs, yc_scr)], sem)

    acc = jnp.dot(ys_scr[...], w0_ref[...], preferred_element_type=f32)
    acc = acc + jnp.dot(yc_scr[...], w1_ref[...], preferred_element_type=f32)
    o_ref[...] = h_ref[...] + acc


def _outproj(h, ys_srcs, yc_srcs, w_out, tn=1024):
    n_rows = h.shape[0]
    seg_table = _row_segments([s.shape[0] for s in ys_srcs], n_rows, TM)
    n_src = len(ys_srcs)
    return pl.pallas_call(
        functools.partial(_outproj_kernel, seg_table=seg_table, n_src=n_src),
        grid=(n_rows // TM, D_MODEL // tn),
        in_specs=[pl.BlockSpec(memory_space=pl.ANY)] * (2 * n_src) + [
            pl.BlockSpec((TM, tn), lambda i, j: (i, j)),
            pl.BlockSpec((D_SSM, tn), lambda i, j: (0, j)),
            pl.BlockSpec((D_CONF, tn), lambda i, j: (1, j)),
        ],
        out_specs=pl.BlockSpec((TM, tn), lambda i, j: (i, j)),
        out_shape=jax.ShapeDtypeStruct((n_rows, D_MODEL), f32),
        scratch_shapes=[pltpu.VMEM((TM, D_SSM), bf16), pltpu.VMEM((TM, D_CONF), bf16), pltpu.SemaphoreType.DMA(())],
        compiler_params=_cparams(("arbitrary", "arbitrary")),
        name="out_proj",
    )(*ys_srcs, *yc_srcs, h, w_out, w_out)


def _gate_norm(y, z, nw):
    y = y * _silu(z)
    half = D_SSM // N_GROUPS
    parts = []
    for g in range(N_GROUPS):
        yg = y[:, g * half:(g + 1) * half]
        ms = jnp.mean(yg * yg, axis=-1, keepdims=True)
        parts.append(yg * lax.rsqrt(ms + EPS) * nw[:, g * half:(g + 1) * half])
    return jnp.concatenate(parts, axis=1)


def _ssd_chunk_math(z, xbc, dt_raw, cinit_ref, hinit_ref, cw_ref, cb_ref, dtb_ref, alog_ref,
                    dx_ref, nw_ref, tri_ref, e_ref, y_ref, hout_ref, tail_ref, win_scr, ht_scr, y_scr,
                    *, n_valid, transpose_out, side_work=None):
    c = pl.program_id(1)

    @pl.when(c == 0)
    def _():
        ht_scr[...] = hinit_ref[...]
        win_scr[0:SUBLANES, :] = cinit_ref[...]

    win_scr[SUBLANES:SUBLANES + CHUNK, :] = xbc
    acc = cb_ref[...] + cw_ref[SSM_CONV - 1:SSM_CONV, :] * xbc
    for j in range(1, SSM_CONV):
        acc = acc + cw_ref[SSM_CONV - 1 - j:SSM_CONV - j, :] * win_scr[SUBLANES - j:SUBLANES - j + CHUNK, :]
    tail_ref[...] = win_scr[n_valid:n_valid + SUBLANES, :]
    win_scr[0:SUBLANES, :] = win_scr[CHUNK:CHUNK + SUBLANES, :]
    xc = _silu(acc)
    b_mats = [xc[:, D_SSM + g * D_STATE:D_SSM + (g + 1) * D_STATE] for g in range(N_GROUPS)]
    c_mats = [xc[:, D_SSM + (N_GROUPS + g) * D_STATE:D_SSM + (N_GROUPS + g + 1) * D_STATE] for g in range(N_GROUPS)]

    rows = lax.broadcasted_iota(jnp.int32, (CHUNK, LANES), 0)
    lanes = lax.broadcasted_iota(jnp.int32, (CHUNK, LANES), 1)
    dt = _softplus(dt_raw + dtb_ref[...])
    if n_valid < CHUNK:
        dt = jnp.where(rows < n_valid, dt, 0.0)
    a = dt * (-jnp.exp(alog_ref[...]))
    acs = jnp.dot(tri_ref[...], a, precision=lax.Precision.HIGHEST, preferred_element_type=f32)
    total = acs[CHUNK - 1:CHUNK, :]
    dend = jnp.exp(total - acs)
    acs_t = acs.T
    dt_t = dt.T
    w_t = (dt * dend).T
    hi, mid, lo = _split3(jnp.exp(total))
    rid = lax.broadcasted_iota(jnp.int32, (SUBLANES, LANES), 0)
    cd3 = jnp.where(rid == 0, hi, jnp.where(rid == 1, mid, jnp.where(rid == 2, lo, 0.0)))
    cdx = jnp.sum(jnp.dot(cd3.astype(bf16), e_ref[...], preferred_element_type=f32), axis=0, keepdims=True)

    cb_mats, bt_mats = [], []
    for g in range(N_GROUPS):
        cb_mats.append(lax.dot_general(c_mats[g].astype(bf16), b_mats[g].astype(bf16), (((1,), (1,)), ((), ())),
                                       preferred_element_type=f32))
        bt_mats.append(b_mats[g].T)

    lane_lo = lanes < HEAD_DIM
    tril = rows >= lanes
    heads_per_group = N_HEADS // N_GROUPS
    for j in range(N_HEADS // 2):
        if side_work is not None:
            side_work(j)
        sl = slice(j * LANES, (j + 1) * LANES)
        xs_pair = xc[:, sl]
        ht_pair = ht_scr[:, sl]
        y_pair = jnp.zeros((CHUNK, LANES), f32)
        upd = jnp.zeros((D_STATE, LANES), f32)
        for half in range(2):
            h = 2 * j + half
            g = h // heads_per_group
            keep = lane_lo if half == 0 else jnp.logical_not(lane_lo)
            xs_m = jnp.where(keep, xs_pair, 0.0).astype(bf16)
            ht_m = jnp.where(keep, ht_pair, 0.0).astype(bf16)
            acs_col = jnp.broadcast_to(acs[:, h:h + 1], (CHUNK, CHUNK))
            acs_row = jnp.broadcast_to(acs_t[h:h + 1, :], (CHUNK, CHUNK))
            lmat = jnp.exp(jnp.where(tril, acs_col - acs_row, -jnp.inf))
            m_h = cb_mats[g] * lmat * dt_t[h:h + 1, :]
            ce_h = c_mats[g] * jnp.exp(acs_col)
            lhs = jnp.concatenate([m_h, ce_h], axis=1).astype(bf16)
            rhs = jnp.concatenate([xs_m, ht_m], axis=0)
            y_pair = y_pair + jnp.dot(lhs, rhs, preferred_element_type=f32)
            bw = (bt_mats[g] * w_t[h:h + 1, :]).astype(bf16)
            upd = upd + jnp.dot(bw, xs_m, preferred_element_type=f32)
        y_scr[:, sl] = y_pair
        ht_scr[:, sl] = ht_pair * cdx[:, sl] + upd

    xs = xc[:, :D_SSM]
    y = y_scr[...] + xs * dx_ref[...]
    y_ref[...] = _gate_norm(y, z(), nw_ref[...]).astype(bf16)

    @pl.when(c == pl.num_programs(1) - 1)
    def _():
        if transpose_out:
            hout_ref[...] = ht_scr[...].T
        else:
            hout_ref[...] = ht_scr[...]


def _ssd_chunk_kernel(z_ref, xs_ref, bc_ref, dt_ref, *rest, n_valid, transpose_out):
    xbc = jnp.concatenate([xs_ref[...], bc_ref[...]], axis=1)
    _ssd_chunk_math(lambda: z_ref[...], xbc, dt_ref[...], *rest, n_valid=n_valid, transpose_out=transpose_out)


N_ZX = D_SSM + CONV_DIM + LANES
W_STAGE = 512


def _load_weight_cols(w_hbm, chunks, stage_scr, sem, consume):
    def copy(j):
        c0, wd = chunks[j]
        return pltpu.make_async_copy(w_hbm.at[0, :, pl.ds(c0, wd)], stage_scr.at[j % 2, :, pl.ds(0, wd)], sem.at[j % 2])

    copy(0).start()
    for j in range(len(chunks)):
        if j + 1 < len(chunks):
            copy(j + 1).start()
        copy(j).wait()
        consume(j, stage_scr[j % 2, :, 0:chunks[j][1]])


def _ssd_fused_kernel(u0_ref, un_ref, w_hbm, *rest, n_valid, transpose_out):
    rest, (wzx_scr, stage_scr, p_scr, z_scr, sem) = rest[:-5], rest[-5:]
    b = pl.program_id(0)
    c = pl.program_id(1)

    @pl.when((b == 0) & (c == 0))
    def _():
        chunks = [(c0, min(W_STAGE, N_ZX - c0)) for c0 in range(0, N_ZX, W_STAGE)]

        def consume(j, v):
            c0, wd = chunks[j]
            wzx_scr[:, c0:c0 + wd] = v.astype(bf16)

        _load_weight_cols(w_hbm, chunks, stage_scr, sem, consume)

    @pl.when(c == 0)
    def _():
        p_scr[...] = jnp.dot(u0_ref[...], wzx_scr[...], preferred_element_type=f32)

    xbc = p_scr[:, D_SSM:D_SSM + CONV_DIM]
    dt_raw = p_scr[:, D_SSM + CONV_DIM:]
    z_scr[...] = p_scr[:, :D_SSM]

    piece = 2 * LANES
    pieces = [(c0, min(piece, N_ZX - c0)) for c0 in range(0, N_ZX, piece)]
    n_pairs = N_HEADS // 2
    per_pair = [pieces[j::n_pairs] for j in range(n_pairs)]

    def side_work(j):
        for c0, wd in per_pair[j]:
            p_scr[:, c0:c0 + wd] = jnp.dot(un_ref[...], wzx_scr[:, c0:c0 + wd], preferred_element_type=f32)

    _ssd_chunk_math(lambda: z_scr[...], xbc, dt_raw, *rest, n_valid=n_valid, transpose_out=transpose_out,
                    side_work=side_work)


def _ssd_chunks(src, cinit, hinit, ssm_w, consts, *, n_seq, n_chunks, row_block0, n_valid, transpose_out):
    cw, cb, dtb, alog, dx, nw = ssm_w
    tri, e_mat = consts
    fused = src[1].ndim == 3
    bc_dim = CONV_DIM - D_SSM
    rows_map = lambda b, c: (row_block0 + b * n_chunks + c, 0)
    fixed = lambda b, c: (0, 0)
    if fused:
        u, w_in = src
        last = n_chunks - 1
        lead_specs = [
            pl.BlockSpec((CHUNK, D_MODEL), rows_map),
            pl.BlockSpec((CHUNK, D_MODEL), lambda b, c: (row_block0 + b * n_chunks + jnp.minimum(c + 1, last), 0)),
            pl.BlockSpec(memory_space=pl.ANY),
        ]
        lead_args = [u, u, w_in]
        extra_scratch = [
            pltpu.VMEM((D_MODEL, N_ZX), bf16),
            pltpu.VMEM((2, D_MODEL, W_STAGE), f32),
            pltpu.VMEM((CHUNK, N_ZX), f32),
            pltpu.VMEM((CHUNK, D_SSM), f32),
            pltpu.SemaphoreType.DMA((2,)),
        ]
        body = _ssd_fused_kernel
    else:
        pzx, pdt = src
        lead_specs = [
            pl.BlockSpec((CHUNK, D_SSM), rows_map),
            pl.BlockSpec((CHUNK, D_SSM), lambda b, c: (row_block0 + b * n_chunks + c, 1)),
            pl.BlockSpec((CHUNK, bc_dim), lambda b, c: (row_block0 + b * n_chunks + c, 2 * D_SSM // bc_dim)),
            pl.BlockSpec((CHUNK, LANES), rows_map),
        ]
        lead_args = [pzx, pzx, pzx, pdt]
        extra_scratch = []
        body = _ssd_chunk_kernel
    in_specs = lead_specs + [
        pl.BlockSpec((SUBLANES, CONV_DIM), fixed),
        pl.BlockSpec((D_STATE, D_SSM), fixed),
        pl.BlockSpec((SSM_CONV, CONV_DIM), fixed),
        pl.BlockSpec((1, CONV_DIM), fixed),
        pl.BlockSpec((1, LANES), fixed),
        pl.BlockSpec((1, LANES), fixed),
        pl.BlockSpec((1, D_SSM), fixed),
        pl.BlockSpec((1, D_SSM), fixed),
        pl.BlockSpec((CHUNK, CHUNK), fixed),
        pl.BlockSpec((LANES, D_SSM), fixed),
    ]
    args = lead_args + [cinit, hinit, cw, cb, dtb, alog, dx, nw, tri, e_mat]
    hshape = (n_seq, D_SSM, D_STATE) if transpose_out else (n_seq, D_STATE, D_SSM)
    return pl.pallas_call(
        functools.partial(body, n_valid=n_valid, transpose_out=transpose_out),
        grid=(n_seq, n_chunks),
        in_specs=in_specs,
        out_specs=(
            pl.BlockSpec((CHUNK, D_SSM), lambda b, c: (b * n_chunks + c, 0)),
            pl.BlockSpec((None,) + hshape[1:], lambda b, c: (b, 0, 0)),
            pl.BlockSpec((None, SUBLANES, CONV_DIM), lambda b, c: (b, 0, 0)),
        ),
        out_shape=(
            jax.ShapeDtypeStruct((n_seq * n_chunks * CHUNK, D_SSM), bf16),
            jax.ShapeDtypeStruct(hshape, f32),
            jax.ShapeDtypeStruct((n_seq, SUBLANES, CONV_DIM), f32),
        ),
        scratch_shapes=[
            pltpu.VMEM((SUBLANES + CHUNK, CONV_DIM), f32),
            pltpu.VMEM((D_STATE, D_SSM), f32),
            pltpu.VMEM((CHUNK, D_SSM), f32),
        ] + extra_scratch,
        compiler_params=_cparams(("arbitrary", "arbitrary")),
        name="ssd_fused" if fused else "ssd_chunks",
    )(*args)


SEQ_TILE = SUBLANES
TOK0 = SEQ_TILE - 4
SSD_BS = 8


def _ssd_sample_kernel(z_ref, xbc_ref, dt_ref, h0_ref, cw_ref, cb_ref, dtb_ref, alog_ref, dx_ref, nw_ref,
                       e_ref, sel_ref, y_ref, hnew_ref, yoff_scr, lhs_scr, rhs_scr):
    n_rows = SSD_BS * SEQ_TILE
    half = D_SSM // N_GROUPS
    x8 = xbc_ref[...]
    acc = cb_ref[...] + cw_ref[SSM_CONV - 1:SSM_CONV, :] * x8
    for j in range(1, SSM_CONV):
        acc = acc + cw_ref[SSM_CONV - 1 - j:SSM_CONV - j, :] * pltpu.roll(x8, j, axis=0)
    xc = _silu(acc)
    xs = xc[:, :D_SSM]
    bm = xc[:, D_SSM:D_SSM + N_GROUPS * D_STATE]
    cm = xc[:, D_SSM + N_GROUPS * D_STATE:]

    pos = lax.broadcasted_iota(jnp.int32, (n_rows, LANES), 0) % SEQ_TILE
    lanes = lax.broadcasted_iota(jnp.int32, (n_rows, LANES), 1)
    tok = pos >= TOK0
    dt = jnp.where(tok, _softplus(dt_ref[...] + dtb_ref[...]), 0.0)
    a = dt * (-jnp.exp(alog_ref[...]))
    acs = a
    for d in range(1, 4):
        acs = acs + pltpu.roll(a, d, axis=0)
    tot = jnp.sum(a.reshape(SSD_BS, SEQ_TILE, LANES), axis=1, keepdims=True)
    tot = jnp.broadcast_to(tot, (SSD_BS, SEQ_TILE, LANES)).reshape(n_rows, LANES)
    dend = jnp.where(tok, jnp.exp(tot - acs), 0.0)
    eacs = jnp.where(tok, jnp.exp(acs), 0.0)
    cdec = jnp.exp(tot)

    def expand(v):
        hi, mid, lo = _split3(v)
        stack = jnp.concatenate([hi, mid, lo], axis=0).astype(bf16)
        r = jnp.dot(stack, e_ref[...], preferred_element_type=f32)
        return r[0:n_rows] + r[n_rows:2 * n_rows] + r[2 * n_rows:3 * n_rows]

    heads_per_group = N_HEADS // N_GROUPS
    y = expand(eacs)
    for s in range(SSD_BS):
        for g in range(N_GROUPS):
            c8 = cm[s * SEQ_TILE:(s + 1) * SEQ_TILE, g * D_STATE:(g + 1) * D_STATE].astype(bf16)
            hg = h0_ref[s, g * half:(g + 1) * half, :].astype(bf16)
            yoff_scr[s * SEQ_TILE:(s + 1) * SEQ_TILE, g * half:(g + 1) * half] = lax.dot_general(
                c8, hg, (((1,), (1,)), ((), ())), preferred_element_type=f32)
    y = y * yoff_scr[...] + xs * dx_ref[...]
    for d in range(4):
        b_sh = pltpu.roll(bm, d, axis=0) if d else bm
        prod = cm * b_sh
        cbs = [jnp.sum(prod[:, g * D_STATE:(g + 1) * D_STATE], axis=-1, keepdims=True) for g in range(N_GROUPS)]
        cbh = jnp.where(lanes < heads_per_group, cbs[0], cbs[1])
        if d:
            coef = cbh * jnp.exp(acs - pltpu.roll(acs, d, axis=0)) * pltpu.roll(dt, d, axis=0)
        else:
            coef = cbh * dt
        coef = jnp.where(pos >= TOK0 + d, coef, 0.0)
        y = y + expand(coef) * (pltpu.roll(xs, d, axis=0) if d else xs)

    yn = _gate_norm(y, z_ref[...], nw_ref[...]).astype(bf16)
    y_ref[...] = jnp.dot(sel_ref[...], yn, preferred_element_type=f32).astype(bf16)

    xw = xs * expand(dt * dend)
    hi, mid, lo = _split3(expand(cdec))
    pos_w = lax.broadcasted_iota(jnp.int32, (n_rows, D_SSM), 0) % SEQ_TILE
    lhs_scr[...] = jnp.where(pos_w == 0, hi, jnp.where(pos_w == 1, mid, jnp.where(pos_w == 2, lo, xw)))
    ones = jnp.where(pos < 3, 1.0, 0.0)
    for g in range(N_GROUPS):
        rhs_scr[:, 2 * g * D_STATE:(2 * g + 1) * D_STATE] = jnp.where(tok, bm[:, g * D_STATE:(g + 1) * D_STATE], 0.0)
        rhs_scr[:, (2 * g + 1) * D_STATE:(2 * g + 2) * D_STATE] = ones
    for s in range(SSD_BS):
        for g in range(N_GROUPS):
            l8 = lhs_scr[s * SEQ_TILE:(s + 1) * SEQ_TILE, g * half:(g + 1) * half].astype(bf16)
            r8 = rhs_scr[s * SEQ_TILE:(s + 1) * SEQ_TILE, 2 * g * D_STATE:(2 * g + 2) * D_STATE].astype(bf16)
            res = lax.dot_general(l8, r8, (((0,), (0,)), ((), ())), preferred_element_type=f32)
            hnew_ref[s, g * half:(g + 1) * half, :] = (
                h0_ref[s, g * half:(g + 1) * half, :] * res[:, D_STATE:] + res[:, :D_STATE])


def _ssd_sample(pz8, pxbc8, pdt8, h0, ssm_w, e_mat, sel):
    cw, cb, dtb, alog, dx, nw = ssm_w
    n_seq = h0.shape[0]
    n_rows = SSD_BS * SEQ_TILE
    n_out = SSD_BS * 4
    rows_map = lambda i: (i, 0)
    fixed = lambda i: (0, 0)
    return pl.pallas_call(
        _ssd_sample_kernel,
        grid=(n_seq // SSD_BS,),
        in_specs=[
            pl.BlockSpec((n_rows, D_SSM), rows_map),
            pl.BlockSpec((n_rows, CONV_DIM), rows_map),
            pl.BlockSpec((n_rows, LANES), rows_map),
            pl.BlockSpec((SSD_BS, D_SSM, D_STATE), lambda i: (i, 0, 0)),
            pl.BlockSpec((SSM_CONV, CONV_DIM), fixed),
            pl.BlockSpec((1, CONV_DIM), fixed),
            pl.BlockSpec((1, LANES), fixed),
            pl.BlockSpec((1, LANES), fixed),
            pl.BlockSpec((1, D_SSM), fixed),
            pl.BlockSpec((1, D_SSM), fixed),
            pl.BlockSpec((LANES, D_SSM), fixed),
            pl.BlockSpec((n_out, n_rows), fixed),
        ],
        out_specs=(
            pl.BlockSpec((n_out, D_SSM), rows_map),
            pl.BlockSpec((SSD_BS, D_SSM, D_STATE), lambda i: (i, 0, 0)),
        ),
        out_shape=(
            jax.ShapeDtypeStruct((n_seq * 4, D_SSM), bf16),
            jax.ShapeDtypeStruct((n_seq, D_SSM, D_STATE), f32),
        ),
        scratch_shapes=[
            pltpu.VMEM((n_rows, D_SSM), f32),
            pltpu.VMEM((n_rows, D_SSM), f32),
            pltpu.VMEM((n_rows, 2 * N_GROUPS * D_STATE), f32),
        ],
        compiler_params=_cparams(("arbitrary",)),
        name="ssd_sample",
    )(pz8, pxbc8, pdt8, h0, cw, cb, dtb, alog, dx, nw, e_mat, sel)


N_LT = D_CONF // LANES
HIST = 32
CONF_RC = 64


def _conf_norm_act(acc, g, beta):
    mu = jnp.mean(acc, axis=-1, keepdims=True)
    xc = acc - mu
    var = jnp.mean(xc * xc, axis=-1, keepdims=True)
    return _silu(xc * lax.rsqrt(var + EPS) * g + beta)


def _conf_conv_tile(s_tile, w_ref, p_ref, lt, y_scr, tt):
    off = HIST - (CONF_KERNEL - 1)
    for r0 in range(0, tt, CONF_RC):
        acc = jnp.broadcast_to(p_ref[lt, 0:1, :], (CONF_RC, LANES))
        for k in range(CONF_KERNEL):
            acc = acc + w_ref[lt, k:k + 1, :] * s_tile[r0 + off + k:r0 + off + k + CONF_RC, :]
        y_scr[lt, r0:r0 + CONF_RC, :] = _conf_norm_act(acc, p_ref[lt, 1:2, :], p_ref[lt, 2:3, :])


def _conf_tile_kernel(ca_ref, cg_ref, hist_ref, w_ref, p_ref, y_ref, tail_ref, s_scr, y_scr, *, tt, n_valid):
    t = pl.program_id(1)

    @pl.when(t == 0)
    def _():
        for lt in range(N_LT):
            s_scr[lt, 0:HIST, :] = hist_ref[:, lt * LANES:(lt + 1) * LANES]

    @pl.when(t > 0)
    def _():
        s_scr[:, 0:HIST, :] = s_scr[:, tt:tt + HIST, :]

    glu = ca_ref[...] * _sigmoid(cg_ref[...])
    for lt in range(N_LT):
        s_scr[lt, HIST:HIST + tt, :] = glu[:, lt * LANES:(lt + 1) * LANES]

    def lane_tile(lt, carry):
        _conf_conv_tile(s_scr.at[lt], w_ref, p_ref, lt, y_scr, tt)
        return carry

    lax.fori_loop(0, N_LT, lane_tile, 0)
    for lt in range(N_LT):
        y_ref[:, lt * LANES:(lt + 1) * LANES] = y_scr[lt].astype(bf16)
        tail_ref[:, lt * LANES:(lt + 1) * LANES] = s_scr[lt, n_valid:n_valid + HIST, :]


C_IN0 = D_SSM + CONV_DIM
C_SHIFT = N_HEADS
C_WIN = W_STAGE + LANES


def _conf_fused_kernel(u0_ref, un_ref, w_hbm, wtail_ref, hist_ref, w_ref, p_ref, y_ref, tail_ref,
                       s_scr, y_scr, g_scr, wc_scr, stage_scr, sem, *, tt):
    b = pl.program_id(0)
    t = pl.program_id(1)
    n_win = D_CONF // W_STAGE

    @pl.when((b == 0) & (t == 0))
    def _():
        chunks = []
        for part in range(2):
            for j in range(n_win):
                short = part == 1 and j == n_win - 1
                chunks.append((C_IN0 + part * D_CONF + j * W_STAGE, W_STAGE if short else C_WIN))

        def consume(idx, v):
            part, j = divmod(idx, n_win)
            if v.shape[1] < C_WIN:
                v = jnp.concatenate([v, wtail_ref[...]], axis=1)
            sh = v[:, C_SHIFT:C_SHIFT + W_STAGE].astype(bf16)
            for q in range(W_STAGE // LANES):
                wc_scr[j * (W_STAGE // LANES) + q, :, part * LANES:(part + 1) * LANES] = sh[:, q * LANES:(q + 1) * LANES]

        _load_weight_cols(w_hbm, chunks, stage_scr, sem, consume)

    def glu_tile(u_ref, lt):
        p = jnp.dot(u_ref[...], wc_scr[lt], preferred_element_type=f32)
        return p[:, :LANES] * _sigmoid(p[:, LANES:])

    @pl.when(t == 0)
    def _():
        for lt in range(N_LT):
            s_scr[lt, 0:HIST, :] = hist_ref[:, lt * LANES:(lt + 1) * LANES]

        def first(lt, carry):
            g_scr[lt] = glu_tile(u0_ref, lt)
            return carry

        lax.fori_loop(0, N_LT, first, 0)

    @pl.when(t > 0)
    def _():
        s_scr[:, 0:HIST, :] = s_scr[:, tt:tt + HIST, :]

    s_scr[:, HIST:HIST + tt, :] = g_scr[...]

    def lane_tile(lt, carry):
        _conf_conv_tile(s_scr.at[lt], w_ref, p_ref, lt, y_scr, tt)
        g_scr[lt] = glu_tile(un_ref, lt)
        return carry

    lax.fori_loop(0, N_LT, lane_tile, 0)
    for lt in range(N_LT):
        y_ref[:, lt * LANES:(lt + 1) * LANES] = y_scr[lt].astype(bf16)
        tail_ref[:, lt * LANES:(lt + 1) * LANES] = s_scr[lt, tt:tt + HIST, :]


def _conf_tiles(src, hist, conf_w, *, n_seq, n_tiles, tt, row_block0, n_valid):
    w3, p3 = conf_w
    fused = len(src) == 3
    tail_specs = [
        pl.BlockSpec((HIST, D_CONF), lambda b, t: (0, 0)),
        pl.BlockSpec((N_LT, HIST, LANES), lambda b, t: (0, 0, 0)),
        pl.BlockSpec((N_LT, SUBLANES, LANES), lambda b, t: (0, 0, 0)),
    ]
    if fused:
        assert n_valid == tt
        u, w_in, w_tail = src
        last = n_tiles - 1
        in_specs = [
            pl.BlockSpec((tt, D_MODEL), lambda b, t: (row_block0 + b * n_tiles + t, 0)),
            pl.BlockSpec((tt, D_MODEL), lambda b, t: (row_block0 + b * n_tiles + jnp.minimum(t + 1, last), 0)),
            pl.BlockSpec(memory_space=pl.ANY),
            pl.BlockSpec((D_MODEL, LANES), lambda b, t: (0, 0)),
        ] + tail_specs
        args = [u, u, w_in, w_tail, hist, w3, p3]
        scratch = [
            pltpu.VMEM((N_LT, HIST + tt, LANES), f32),
            pltpu.VMEM((N_LT, tt, LANES), f32),
            pltpu.VMEM((N_LT, tt, LANES), f32),
            pltpu.VMEM((N_LT, D_MODEL, 2 * LANES), bf16),
            pltpu.VMEM((2, D_MODEL, C_WIN), f32),
            pltpu.SemaphoreType.DMA((2,)),
        ]
        body = functools.partial(_conf_fused_kernel, tt=tt)
    else:
        pc, = src
        in_specs = [
            pl.BlockSpec((tt, D_CONF), lambda b, t: (row_block0 + b * n_tiles + t, 0)),
            pl.BlockSpec((tt, D_CONF), lambda b, t: (row_block0 + b * n_tiles + t, 1)),
        ] + tail_specs
        args = [pc, pc, hist, w3, p3]
        scratch = [
            pltpu.VMEM((N_LT, HIST + tt, LANES), f32),
            pltpu.VMEM((N_LT, tt, LANES), f32),
        ]
        body = functools.partial(_conf_tile_kernel, tt=tt, n_valid=n_valid)
    return pl.pallas_call(
        body,
        grid=(n_seq, n_tiles),
        in_specs=in_specs,
        out_specs=(
            pl.BlockSpec((tt, D_CONF), lambda b, t: (b * n_tiles + t, 0)),
            pl.BlockSpec((None, HIST, D_CONF), lambda b, t: (b, 0, 0)),
        ),
        out_shape=(
            jax.ShapeDtypeStruct((n_seq * n_tiles * tt, D_CONF), bf16),
            jax.ShapeDtypeStruct((n_seq, HIST, D_CONF), f32),
        ),
        scratch_shapes=scratch,
        compiler_params=_cparams(("arbitrary", "arbitrary")),
        name="conf_fused" if fused else "conf_tiles",
    )(*args)


CONF_BS = 8
WIN = 40


def _conf_sample_kernel(ca_ref, cg_ref, buf_ref, w_ref, p_ref, sel_ref, y_ref, newbuf_ref, w_scr, y8_scr):
    n_hist = CONF_KERNEL - 1
    glu = ca_ref[...] * _sigmoid(cg_ref[...])
    for s in range(CONF_BS):
        w_scr[s, 0:SUBLANES, :] = jnp.zeros((SUBLANES, D_CONF), f32)
        w_scr[s, WIN - SEQ_TILE:WIN, :] = glu[s * SEQ_TILE:(s + 1) * SEQ_TILE, :]
        w_scr[s, WIN - 4 - n_hist:WIN - 4, :] = buf_ref[s]
        newbuf_ref[s] = w_scr[s, WIN - n_hist:WIN, :]
    off = WIN - SEQ_TILE - n_hist

    def one_seq(s, carry):
        row0 = pl.multiple_of(s * SEQ_TILE, SEQ_TILE)
        for lt in range(N_LT):
            sl = slice(lt * LANES, (lt + 1) * LANES)
            acc = jnp.broadcast_to(p_ref[lt, 0:1, :], (SEQ_TILE, LANES))
            for k in range(CONF_KERNEL):
                acc = acc + w_ref[lt, k:k + 1, :] * w_scr[s, off + k:off + k + SEQ_TILE, sl]
            y8_scr[pl.ds(row0, SEQ_TILE), sl] = _conf_norm_act(acc, p_ref[lt, 1:2, :], p_ref[lt, 2:3, :])
        return carry

    lax.fori_loop(0, CONF_BS, one_seq, 0)
    y_ref[...] = jnp.dot(sel_ref[...], y8_scr[...].astype(bf16), preferred_element_type=f32).astype(bf16)


def _conf_sample(pc8, buf, conf_w, sel):
    w3, p3 = conf_w
    n_seq = buf.shape[0]
    n_rows = CONF_BS * SEQ_TILE
    n_out = CONF_BS * 4
    n_hist = CONF_KERNEL - 1
    return pl.pallas_call(
        _conf_sample_kernel,
        grid=(n_seq // CONF_BS,),
        in_specs=[
            pl.BlockSpec((n_rows, D_CONF), lambda i: (i, 0)),
            pl.BlockSpec((n_rows, D_CONF), lambda i: (i, 1)),
            pl.BlockSpec((CONF_BS, n_hist, D_CONF), lambda i: (i, 0, 0)),
            pl.BlockSpec((N_LT, HIST, LANES), lambda i: (0, 0, 0)),
            pl.BlockSpec((N_LT, SUBLANES, LANES), lambda i: (0, 0, 0)),
            pl.BlockSpec((n_out, n_rows), lambda i: (0, 0)),
        ],
        out_specs=(
            pl.BlockSpec((n_out, D_CONF), lambda i: (i, 0)),
            pl.BlockSpec((CONF_BS, n_hist, D_CONF), lambda i: (i, 0, 0)),
        ),
        out_shape=(
            jax.ShapeDtypeStruct((n_seq * 4, D_CONF), bf16),
            jax.ShapeDtypeStruct((n_seq, n_hist, D_CONF), f32),
        ),
        scratch_shapes=[
            pltpu.VMEM((CONF_BS, WIN, D_CONF), f32),
            pltpu.VMEM((n_rows, D_CONF), f32),
        ],
        compiler_params=_cparams(("arbitrary",)),
        name="conf_sample",
    )(pc8, pc8, buf, w3, p3, sel)


def _seq_tiles(rows, n_seq, n_tok):
    w = rows.shape[-1]
    return jnp.pad(rows.reshape(n_seq, n_tok, w), ((0, 0), (SEQ_TILE - n_tok, 0), (0, 0))).reshape(n_seq * SEQ_TILE, w)


def kernel(x_prompt, x_sample, state_ssm, state_ssm_conv, state_conf_conv, meta_tokens, ffn1_norm, ffn1_w_gate, ffn1_w_up, ffn1_w_down, mix_norm, w_in, ssm_conv_w, ssm_conv_b, ssm_dt_bias, ssm_A_log, ssm_D, ssm_norm, conf_conv_w, conf_conv_b, conf_ln_g, conf_ln_b, w_out, ffn2_norm, ffn2_w_gate, ffn2_w_up, ffn2_w_down, final_norm):
    n_pb, n_pt = x_prompt.shape[0], x_prompt.shape[1]
    n_sb, n_st = x_sample.shape[0], x_sample.shape[1]
    assert (n_pb * n_pt, n_sb * n_st, n_st) == (N_PROMPT_ROWS, N_SAMPLE_ROWS, 4) and n_pt % CHUNK == 0

    row = lambda v: v.reshape(1, -1)
    x_rows = (x_prompt.reshape(N_PROMPT_ROWS, D_MODEL), x_sample.reshape(N_SAMPLE_ROWS, D_MODEL), meta_tokens)
    h, u = _ffn(x_rows, row(ffn1_norm[0]), _to_bf16(ffn1_w_gate, 256), _to_bf16(ffn1_w_up, 256),
                _to_bf16(ffn1_w_down, 704), row(mix_norm[0]))

    small0, n_small = ROW_SAMPLE // CHUNK, (N_ROWS - ROW_SAMPLE) // CHUNK
    w_tail = jnp.pad(w_in[0, :, C_IN0 + 2 * D_CONF:], ((0, 0), (0, LANES - C_SHIFT)))
    pzx = _matmul_rows(u, w_in, row_block0=small0, n_row_blocks=n_small, n=C_IN0, tn=1536)
    pc_sh = _matmul_rows(u, w_in, row_block0=small0, n_row_blocks=n_small, n=2 * D_CONF, tn=512, col0=C_IN0)
    pc_tail = _matmul_rows(u, w_tail, row_block0=small0, n_row_blocks=n_small, n=LANES, tn=LANES)
    pdt = pc_sh[:, :LANES]
    pc = jnp.concatenate([pc_sh[:, C_SHIFT:], pc_tail[:, :C_SHIFT]], axis=1)

    pad_heads = lambda v: jnp.pad(v.reshape(1, N_HEADS), ((0, 0), (0, LANES - N_HEADS)))
    ssm_w = (ssm_conv_w[0], row(ssm_conv_b[0]), pad_heads(ssm_dt_bias[0]), pad_heads(ssm_A_log[0]),
             row(jnp.repeat(ssm_D[0], HEAD_DIM)), row(ssm_norm[0]))
    tri = jnp.tril(jnp.ones((CHUNK, CHUNK), f32))
    e_mat = (jnp.arange(LANES)[:, None] == (jnp.arange(D_SSM)[None, :] // HEAD_DIM)).astype(bf16)
    lane_major = lambda v: v.reshape(v.shape[0], N_LT, LANES).transpose(1, 0, 2)
    conf_w = (lane_major(jnp.pad(conf_conv_w[0], ((0, HIST - CONF_KERNEL), (0, 0)))),
              lane_major(jnp.pad(jnp.stack([conf_conv_b[0], conf_ln_g[0], conf_ln_b[0]]), ((0, SUBLANES - 3), (0, 0)))))
    out_rows = jnp.arange(SSD_BS * 4)
    sel = ((out_rows // 4) * SEQ_TILE + TOK0 + out_rows % 4)[:, None] == jnp.arange(SSD_BS * SEQ_TILE)[None, :]
    sel = sel.astype(bf16)

    meta_block = (ROW_META - ROW_SAMPLE) // CHUNK
    ys_m, h_meta, tail_meta = _ssd_chunks(
        (pzx, pdt), jnp.zeros((SUBLANES, CONV_DIM), f32), jnp.zeros((D_STATE, D_SSM), f32), ssm_w, (tri, e_mat),
        n_seq=1, n_chunks=1, row_block0=meta_block, n_valid=N_META, transpose_out=False)
    yc_m, hist_meta = _conf_tiles((pc,), jnp.zeros((HIST, D_CONF), f32), conf_w,
                                  n_seq=1, n_tiles=1, tt=CHUNK, row_block0=meta_block, n_valid=N_META)

    ys_p, p_ssm, p_tail = _ssd_chunks(
        (u, w_in), tail_meta[0], h_meta[0], ssm_w, (tri, e_mat),
        n_seq=n_pb, n_chunks=n_pt // CHUNK, row_block0=0, n_valid=CHUNK, transpose_out=True)
    conf_tt = 256
    yc_p, p_hist = _conf_tiles((u, w_in, w_tail), hist_meta[0], conf_w,
                               n_seq=n_pb, n_tiles=n_pt // conf_tt, tt=conf_tt, row_block0=0, n_valid=conf_tt)

    smp = slice(0, N_SAMPLE_ROWS)
    pzx_s = pzx[smp]
    pxbc_s = pzx_s[:, D_SSM:].reshape(n_sb, n_st, CONV_DIM)
    pxbc8 = jnp.concatenate([jnp.zeros((n_sb, 1, CONV_DIM), f32), state_ssm_conv[0], pxbc_s], axis=1)
    ys_s, s_ssm = _ssd_sample(_seq_tiles(pzx_s[:, :D_SSM], n_sb, n_st), pxbc8.reshape(n_sb * SEQ_TILE, CONV_DIM),
                              _seq_tiles(pdt[smp], n_sb, n_st), state_ssm[0].reshape(n_sb, D_SSM, D_STATE),
                              ssm_w, e_mat, sel)
    yc_s, s_cconv = _conf_sample(_seq_tiles(pc[smp], n_sb, n_st), state_conf_conv[0], conf_w, sel)

    h2 = _outproj(h, (ys_p, ys_s, ys_m), (yc_p, yc_s, yc_m), _to_bf16(w_out, 512))
    y_prompt, y_sample = _ffn((h2,), row(ffn2_norm[0]), _to_bf16(ffn2_w_gate, 256), _to_bf16(ffn2_w_up, 256),
                              _to_bf16(ffn2_w_down, 704), row(final_norm), dst_rows=(N_PROMPT_ROWS, N_SAMPLE_ROWS))
    n_hist = CONF_KERNEL - 1
    return (
        y_prompt.reshape(n_pb, n_pt, D_MODEL), y_sample.reshape(n_sb, n_st, D_MODEL),
        p_ssm.reshape(1, n_pb, N_HEADS, HEAD_DIM, D_STATE),
        p_tail[:, SUBLANES - (SSM_CONV - 1):, :][None],
        p_hist[:, HIST - n_hist:, :][None],
        s_ssm.reshape(1, n_sb, N_HEADS, HEAD_DIM, D_STATE),
        pxbc_s[:, n_st - (SSM_CONV - 1):, :][None],
        s_cconv[None],
    )
```

```python
import functools

import jax
import jax.numpy as jnp
from jax import lax
from jax.experimental import pallas as pl
from jax.experimental.pallas import tpu as pltpu

f32 = jnp.float32
bf16 = jnp.bfloat16

D_MODEL = 2048
N_META = 16
D_SSM = 2048
D_CONF = 2048
HEAD_DIM = 64
N_HEADS = 32
N_GROUPS = 2
D_STATE = 128
SSM_CONV = 4
CONV_DIM = D_SSM + 2 * N_GROUPS * D_STATE
CONF_KERNEL = 31
CONF_GROUP = 128
D_FF = 5632
EPS = 1e-5

LANES = 128
SUBLANES = 8
CHUNK = 128
VMEM_LIMIT = 56 * 1024 * 1024

N_PROMPT_ROWS = 4 * 2048
N_SAMPLE_ROWS = 128 * 4
ROW_SAMPLE = N_PROMPT_ROWS
ROW_META = N_PROMPT_ROWS + N_SAMPLE_ROWS
N_ROWS = ROW_META + CHUNK
TM = 736
TF = 256


def _cparams(sem):
    return pltpu.CompilerParams(dimension_semantics=sem, vmem_limit_bytes=VMEM_LIMIT)


def _sigmoid(x):
    return 1.0 / (1.0 + jnp.exp(-x))


def _silu(x):
    return x * _sigmoid(x)


def _rms(x, g):
    return x * lax.rsqrt(jnp.mean(x * x, axis=-1, keepdims=True) + EPS) * g


def _softplus(x):
    return jnp.maximum(x, 0.0) + jnp.log1p(jnp.exp(-jnp.abs(x)))


def _split3(v):
    hi = v.astype(bf16).astype(f32)
    r1 = v - hi
    mid = r1.astype(bf16).astype(f32)
    return hi, mid, r1 - mid


def _row_segments(src_rows, n_total, tm):
    bounds, start = [], 0
    for k, n in enumerate(src_rows):
        bounds.append((k, start, n))
        start += n
    if start < n_total:
        bounds.append((None, start, n_total - start))
    table = []
    for t in range(n_total // tm):
        lo, hi = t * tm, (t + 1) * tm
        segs = []
        for k, s, n in bounds:
            a, b = max(lo, s), min(hi, s + n)
            if a < b:
                segs.append((k, a - s, a - lo, b - a))
        table.append(segs)
    return table


def _load_tile(i, seg_table, groups, sem):
    tm = groups[0][2].shape[0]
    special = [t for t, segs in enumerate(seg_table) if segs != [(0, t * tm, 0, tm)]]
    first = special[0] if special else len(seg_table)
    assert special == list(range(first, len(seg_table)))

    @pl.when(i < first)
    def _():
        for main_ref, _, dst_ref in groups:
            dst_ref[...] = main_ref[...]

    for t in special:
        @pl.when(i == t)
        def _(segs=seg_table[t]):
            copies = []
            for main_ref, tail_srcs, dst_ref in groups:
                for k, r0, d0, n in segs:
                    if k is None:
                        dst_ref[d0:d0 + n, :] = jnp.zeros((n, dst_ref.shape[1]), dst_ref.dtype)
                    elif k == 0:
                        dst_ref[d0:d0 + n, :] = main_ref[d0:d0 + n, :]
                    else:
                        cp = pltpu.make_async_copy(tail_srcs[k - 1].at[pl.ds(r0, n)], dst_ref.at[pl.ds(d0, n)], sem)
                        cp.start()
                        copies.append(cp)
            for cp in copies:
                cp.wait()


def _drain_rows(i, out_table, dsts, acc_scr, sems):
    def copies(t):
        return [pltpu.make_async_copy(acc_scr.at[t % 2, pl.ds(s0, n)], dsts[k].at[pl.ds(r0, n)], sems.at[t % 2])
                for k, r0, s0, n in out_table[t] if k is not None]

    n_tiles = len(out_table)
    for t in range(n_tiles):
        @pl.when(i == t)
        def _(t=t):
            if t > 0:
                for cp in copies(t - 1):
                    cp.wait()
            for cp in copies(t):
                cp.start()
            if t == n_tiles - 1:
                for cp in copies(t):
                    cp.wait()


def _ffn_kernel(*refs, seg_table, out_table, n_src):
    x_ref, tail_srcs = refs[0], refs[1:n_src]
    nw_ref, wg_ref, wu_ref, wd_ref, pw_ref = refs[n_src:n_src + 5]
    i = pl.program_id(0)
    f = pl.program_id(1)
    if out_table is None:
        acc, yn_ref, xn_ref, sem = refs[n_src + 5:]
    else:
        n_dst = 1 + max(k for segs in out_table for k, _, _, _ in segs if k is not None)
        dsts = refs[n_src + 5:n_src + 5 + n_dst]
        acc_scr, xn_ref, sem, out_sems = refs[n_src + 5 + n_dst:]
        acc = acc_scr.at[i % 2]

    @pl.when(f == 0)
    def _():
        _load_tile(i, seg_table, [(x_ref, tail_srcs, acc)], sem)
        xn_ref[...] = _rms(acc[...], nw_ref[...]).astype(bf16)

    xn = xn_ref[...]
    g = jnp.dot(xn, wg_ref[...], preferred_element_type=f32)
    u = jnp.dot(xn, wu_ref[...], preferred_element_type=f32)
    a = (_silu(g) * u).astype(bf16)
    acc[...] += 0.5 * jnp.dot(a, wd_ref[...], preferred_element_type=f32)

    @pl.when(f == pl.num_programs(1) - 1)
    def _():
        if out_table is None:
            yn_ref[...] = _rms(acc[...], pw_ref[...]).astype(yn_ref.dtype)
        else:
            acc[...] = _rms(acc[...], pw_ref[...])
            _drain_rows(i, out_table, dsts, acc_scr, out_sems)


def _ffn(srcs, norm_w, wg, wu, wd, post_w, *, dst_rows=None):
    seg_table = _row_segments([s.shape[0] for s in srcs], N_ROWS, TM)
    grid = (N_ROWS // TM, D_FF // TF)
    row_spec = pl.BlockSpec((TM, D_MODEL), lambda i, f: (i, 0))
    vec_spec = pl.BlockSpec((1, D_MODEL), lambda i, f: (0, 0))
    scratch = [pltpu.VMEM((TM, D_MODEL), bf16), pltpu.SemaphoreType.DMA(())]
    if dst_rows is None:
        out_table = None
        out_shape = (jax.ShapeDtypeStruct((N_ROWS, D_MODEL), f32), jax.ShapeDtypeStruct((N_ROWS, D_MODEL), bf16))
        out_specs = (row_spec, row_spec)
    else:
        out_table = _row_segments(dst_rows, N_ROWS, TM)
        out_shape = tuple(jax.ShapeDtypeStruct((n, D_MODEL), f32) for n in dst_rows)
        out_specs = tuple(pl.BlockSpec(memory_space=pl.ANY) for _ in dst_rows)
        scratch = [pltpu.VMEM((2, TM, D_MODEL), f32)] + scratch + [pltpu.SemaphoreType.DMA((2,))]
    return pl.pallas_call(
        functools.partial(_ffn_kernel, seg_table=seg_table, out_table=out_table, n_src=len(srcs)),
        grid=grid,
        in_specs=[row_spec] + [pl.BlockSpec(memory_space=pl.ANY)] * (len(srcs) - 1) + [
            vec_spec,
            pl.BlockSpec((D_MODEL, TF), lambda i, f: (0, f)),
            pl.BlockSpec((D_MODEL, TF), lambda i, f: (0, f)),
            pl.BlockSpec((TF, D_MODEL), lambda i, f: (f, 0)),
            vec_spec,
        ],
        out_specs=out_specs,
        out_shape=out_shape,
        scratch_shapes=scratch,
        compiler_params=_cparams(("arbitrary", "arbitrary")),
        name="ffn",
    )(*srcs, norm_w, wg, wu, wd, post_w)


def _cast_kernel(x_ref, o_ref):
    o_ref[...] = x_ref[...].astype(bf16)


def _to_bf16(w, block_rows):
    r, c = w.shape[-2:]
    if w.ndim == 3:
        in_spec = pl.BlockSpec((None, block_rows, c), lambda i: (0, i, 0))
    else:
        in_spec = pl.BlockSpec((block_rows, c), lambda i: (i, 0))
    return pl.pallas_call(
        _cast_kernel,
        grid=(r // block_rows,),
        in_specs=[in_spec],
        out_specs=pl.BlockSpec((block_rows, c), lambda i: (i, 0)),
        out_shape=jax.ShapeDtypeStruct((r, c), bf16),
        compiler_params=_cparams(("arbitrary",)),
        name="cast_bf16",
    )(w)


def _matmul_nt_kernel(a_ref, w_ref, o_ref):
    o_ref[...] = lax.dot_general(a_ref[...], w_ref[...], (((1,), (1,)), ((), ())), preferred_element_type=f32)


def _matmul_nt(a, w_t, *, row0, n, tn):
    n_rows, k = a.shape
    align = 2 * 16
    assert row0 % align == 0 and tn % align == 0 and n % tn == 0
    return pl.pallas_call(
        _matmul_nt_kernel,
        grid=(n_rows // TM, n // tn),
        in_specs=[
            pl.BlockSpec((TM, k), lambda i, j: (i, 0)),
            pl.BlockSpec((pl.Element(tn), pl.Element(k)), lambda i, j: (pl.multiple_of(row0 + tn * j, align), 0)),
        ],
        out_specs=pl.BlockSpec((TM, tn), lambda i, j: (i, j)),
        out_shape=jax.ShapeDtypeStruct((n_rows, n), f32),
        compiler_params=_cparams(("arbitrary", "arbitrary")),
        name="in_proj",
    )(a, w_t)


def _outproj_kernel(*refs, seg_table, n_src):
    ys_ref, ys_tails = refs[0], refs[1:n_src]
    yc_ref, yc_tails = refs[n_src], refs[n_src + 1:2 * n_src]
    h_ref, w0_ref, w1_ref, o_ref, ys_scr, yc_scr, sem = refs[2 * n_src:]

    @pl.when(pl.program_id(1) == 0)
    def _():
        _load_tile(pl.program_id(0), seg_table, [(ys_ref, ys_tails, ys_scr), (yc_ref, yc_tails, yc_scr)], sem)

    acc = jnp.dot(ys_scr[...], w0_ref[...], preferred_element_type=f32)
    acc = acc + jnp.dot(yc_scr[...], w1_ref[...], preferred_element_type=f32)
    o_ref[...] = h_ref[...] + acc


def _outproj(h, ys_srcs, yc_srcs, w_out, tn=1024):
    n_rows = h.shape[0]
    seg_table = _row_segments([s.shape[0] for s in ys_srcs], n_rows, TM)
    n_src = len(ys_srcs)
    y_specs = [pl.BlockSpec((TM, D_SSM), lambda i, j: (i, 0))] + [pl.BlockSpec(memory_space=pl.ANY)] * (n_src - 1)
    return pl.pallas_call(
        functools.partial(_outproj_kernel, seg_table=seg_table, n_src=n_src),
        grid=(n_rows // TM, D_MODEL // tn),
        in_specs=y_specs + y_specs + [
            pl.BlockSpec((TM, tn), lambda i, j: (i, j)),
            pl.BlockSpec((D_SSM, tn), lambda i, j: (0, j)),
            pl.BlockSpec((D_CONF, tn), lambda i, j: (1, j)),
        ],
        out_specs=pl.BlockSpec((TM, tn), lambda i, j: (i, j)),
        out_shape=jax.ShapeDtypeStruct((n_rows, D_MODEL), f32),
        scratch_shapes=[pltpu.VMEM((TM, D_SSM), bf16), pltpu.VMEM((TM, D_CONF), bf16), pltpu.SemaphoreType.DMA(())],
        compiler_params=_cparams(("arbitrary", "arbitrary")),
        name="out_proj",
    )(*ys_srcs, *yc_srcs, h, w_out, w_out)


def _gate_norm(y, z, nw):
    y = y * _silu(z)
    half = D_SSM // N_GROUPS
    parts = []
    for g in range(N_GROUPS):
        yg = y[:, g * half:(g + 1) * half]
        ms = jnp.mean(yg * yg, axis=-1, keepdims=True)
        parts.append(yg * lax.rsqrt(ms + EPS) * nw[:, g * half:(g + 1) * half])
    return jnp.concatenate(parts, axis=1)


def _ssd_chunk_math(z, xbc, dt_raw, cinit_ref, hinit_ref, cw_ref, cb_ref, dtb_ref, alog_ref,
                    dx_ref, nw_ref, tri_ref, e_ref, y_ref, hout_ref, tail_ref, win_scr, ht_scr, y_scr,
                    *, n_valid, transpose_out):
    c = pl.program_id(1)

    @pl.when(c == 0)
    def _():
        ht_scr[...] = hinit_ref[...]
        win_scr[0:SUBLANES, :] = cinit_ref[...]

    win_scr[SUBLANES:SUBLANES + CHUNK, :] = xbc
    acc = cb_ref[...] + cw_ref[SSM_CONV - 1:SSM_CONV, :] * xbc
    for j in range(1, SSM_CONV):
        acc = acc + cw_ref[SSM_CONV - 1 - j:SSM_CONV - j, :] * win_scr[SUBLANES - j:SUBLANES - j + CHUNK, :]
    tail_ref[...] = win_scr[n_valid:n_valid + SUBLANES, :]
    win_scr[0:SUBLANES, :] = win_scr[CHUNK:CHUNK + SUBLANES, :]
    xc = _silu(acc)
    b_mats = [xc[:, D_SSM + g * D_STATE:D_SSM + (g + 1) * D_STATE] for g in range(N_GROUPS)]
    c_mats = [xc[:, D_SSM + (N_GROUPS + g) * D_STATE:D_SSM + (N_GROUPS + g + 1) * D_STATE] for g in range(N_GROUPS)]

    rows = lax.broadcasted_iota(jnp.int32, (CHUNK, LANES), 0)
    lanes = lax.broadcasted_iota(jnp.int32, (CHUNK, LANES), 1)
    dt = _softplus(dt_raw + dtb_ref[...])
    if n_valid < CHUNK:
        dt = jnp.where(rows < n_valid, dt, 0.0)
    a = dt * (-jnp.exp(alog_ref[...]))
    acs = jnp.dot(tri_ref[...], a, precision=lax.Precision.HIGHEST, preferred_element_type=f32)
    total = acs[CHUNK - 1:CHUNK, :]
    dend = jnp.exp(total - acs)
    acs_t = acs.T
    dt_t = dt.T
    w_t = (dt * dend).T
    hi, mid, lo = _split3(jnp.exp(total))
    rid = lax.broadcasted_iota(jnp.int32, (SUBLANES, LANES), 0)
    cd3 = jnp.where(rid == 0, hi, jnp.where(rid == 1, mid, jnp.where(rid == 2, lo, 0.0)))
    cdx = jnp.sum(jnp.dot(cd3.astype(bf16), e_ref[...], preferred_element_type=f32), axis=0, keepdims=True)

    cb_mats, bt_mats = [], []
    for g in range(N_GROUPS):
        cb_mats.append(lax.dot_general(c_mats[g].astype(bf16), b_mats[g].astype(bf16), (((1,), (1,)), ((), ())),
                                       preferred_element_type=f32))
        bt_mats.append(b_mats[g].T)

    lane_lo = lanes < HEAD_DIM
    tril = rows >= lanes
    heads_per_group = N_HEADS // N_GROUPS
    for j in range(N_HEADS // 2):
        sl = slice(j * LANES, (j + 1) * LANES)
        xs_pair = xc[:, sl]
        ht_pair = ht_scr[:, sl]
        y_pair = jnp.zeros((CHUNK, LANES), f32)
        upd = jnp.zeros((D_STATE, LANES), f32)
        for half in range(2):
            h = 2 * j + half
            g = h // heads_per_group
            keep = lane_lo if half == 0 else jnp.logical_not(lane_lo)
            xs_m = jnp.where(keep, xs_pair, 0.0).astype(bf16)
            ht_m = jnp.where(keep, ht_pair, 0.0).astype(bf16)
            acs_col = jnp.broadcast_to(acs[:, h:h + 1], (CHUNK, CHUNK))
            acs_row = jnp.broadcast_to(acs_t[h:h + 1, :], (CHUNK, CHUNK))
            lmat = jnp.exp(jnp.where(tril, acs_col - acs_row, -jnp.inf))
            m_h = cb_mats[g] * lmat * dt_t[h:h + 1, :]
            ce_h = c_mats[g] * jnp.exp(acs_col)
            lhs = jnp.concatenate([m_h, ce_h], axis=1).astype(bf16)
            rhs = jnp.concatenate([xs_m, ht_m], axis=0)
            y_pair = y_pair + jnp.dot(lhs, rhs, preferred_element_type=f32)
            bw = (bt_mats[g] * w_t[h:h + 1, :]).astype(bf16)
            upd = upd + jnp.dot(bw, xs_m, preferred_element_type=f32)
        y_scr[:, sl] = y_pair
        ht_scr[:, sl] = ht_pair * cdx[:, sl] + upd

    xs = xc[:, :D_SSM]
    y = y_scr[...] + xs * dx_ref[...]
    y_ref[...] = _gate_norm(y, z(), nw_ref[...]).astype(bf16)

    @pl.when(c == pl.num_programs(1) - 1)
    def _():
        if transpose_out:
            hout_ref[...] = ht_scr[...].T
        else:
            hout_ref[...] = ht_scr[...]


def _ssd_chunk_kernel(z_ref, xs_ref, bc_ref, dt_ref, *rest, n_valid, transpose_out):
    xbc = jnp.concatenate([xs_ref[...], bc_ref[...]], axis=1)
    _ssd_chunk_math(lambda: z_ref[...], xbc, dt_ref[...], *rest, n_valid=n_valid, transpose_out=transpose_out)


def _ssd_chunks(pzx, pdt, cinit, hinit, ssm_w, consts, *, n_seq, n_chunks, row_block0, n_valid, transpose_out):
    cw, cb, dtb, alog, dx, nw = ssm_w
    tri, e_mat = consts
    bc_dim = CONV_DIM - D_SSM
    rows_map = lambda b, c: (row_block0 + b * n_chunks + c, 0)
    fixed = lambda b, c: (0, 0)
    in_specs = [
        pl.BlockSpec((CHUNK, D_SSM), rows_map),
        pl.BlockSpec((CHUNK, D_SSM), lambda b, c: (row_block0 + b * n_chunks + c, 1)),
        pl.BlockSpec((CHUNK, bc_dim), lambda b, c: (row_block0 + b * n_chunks + c, 2 * D_SSM // bc_dim)),
        pl.BlockSpec((CHUNK, LANES), rows_map),
        pl.BlockSpec((SUBLANES, CONV_DIM), fixed),
        pl.BlockSpec((D_STATE, D_SSM), fixed),
        pl.BlockSpec((SSM_CONV, CONV_DIM), fixed),
        pl.BlockSpec((1, CONV_DIM), fixed),
        pl.BlockSpec((1, LANES), fixed),
        pl.BlockSpec((1, LANES), fixed),
        pl.BlockSpec((1, D_SSM), fixed),
        pl.BlockSpec((1, D_SSM), fixed),
        pl.BlockSpec((CHUNK, CHUNK), fixed),
        pl.BlockSpec((LANES, D_SSM), fixed),
    ]
    args = [pzx, pzx, pzx, pdt, cinit, hinit, cw, cb, dtb, alog, dx, nw, tri, e_mat]
    hshape = (n_seq, D_SSM, D_STATE) if transpose_out else (n_seq, D_STATE, D_SSM)
    return pl.pallas_call(
        functools.partial(_ssd_chunk_kernel, n_valid=n_valid, transpose_out=transpose_out),
        grid=(n_seq, n_chunks),
        in_specs=in_specs,
        out_specs=(
            pl.BlockSpec((CHUNK, D_SSM), lambda b, c: (b * n_chunks + c, 0)),
            pl.BlockSpec((None,) + hshape[1:], lambda b, c: (b, 0, 0)),
            pl.BlockSpec((None, SUBLANES, CONV_DIM), lambda b, c: (b, 0, 0)),
        ),
        out_shape=(
            jax.ShapeDtypeStruct((n_seq * n_chunks * CHUNK, D_SSM), bf16),
            jax.ShapeDtypeStruct(hshape, f32),
            jax.ShapeDtypeStruct((n_seq, SUBLANES, CONV_DIM), f32),
        ),
        scratch_shapes=[
            pltpu.VMEM((SUBLANES + CHUNK, CONV_DIM), f32),
            pltpu.VMEM((D_STATE, D_SSM), f32),
            pltpu.VMEM((CHUNK, D_SSM), f32),
        ],
        compiler_params=_cparams(("arbitrary", "arbitrary")),
        name="ssd_chunks",
    )(*args)


SEQ_TILE = SUBLANES
TOK0 = SEQ_TILE - 4
SSD_BS = 8


def _ssd_sample_kernel(z_ref, xbc_ref, dt_ref, h0_ref, cw_ref, cb_ref, dtb_ref, alog_ref, dx_ref, nw_ref,
                       e_ref, sel_ref, y_ref, hnew_ref, yoff_scr, lhs_scr, rhs_scr):
    n_rows = SSD_BS * SEQ_TILE
    half = D_SSM // N_GROUPS
    x8 = xbc_ref[...]
    acc = cb_ref[...] + cw_ref[SSM_CONV - 1:SSM_CONV, :] * x8
    for j in range(1, SSM_CONV):
        acc = acc + cw_ref[SSM_CONV - 1 - j:SSM_CONV - j, :] * pltpu.roll(x8, j, axis=0)
    xc = _silu(acc)
    xs = xc[:, :D_SSM]
    bm = xc[:, D_SSM:D_SSM + N_GROUPS * D_STATE]
    cm = xc[:, D_SSM + N_GROUPS * D_STATE:]

    pos = lax.broadcasted_iota(jnp.int32, (n_rows, LANES), 0) % SEQ_TILE
    lanes = lax.broadcasted_iota(jnp.int32, (n_rows, LANES), 1)
    tok = pos >= TOK0
    dt = jnp.where(tok, _softplus(dt_ref[...] + dtb_ref[...]), 0.0)
    a = dt * (-jnp.exp(alog_ref[...]))
    acs = a
    for d in range(1, 4):
        acs = acs + pltpu.roll(a, d, axis=0)
    tot = jnp.sum(a.reshape(SSD_BS, SEQ_TILE, LANES), axis=1, keepdims=True)
    tot = jnp.broadcast_to(tot, (SSD_BS, SEQ_TILE, LANES)).reshape(n_rows, LANES)
    dend = jnp.where(tok, jnp.exp(tot - acs), 0.0)
    eacs = jnp.where(tok, jnp.exp(acs), 0.0)
    cdec = jnp.exp(tot)

    def expand(v):
        hi, mid, lo = _split3(v)
        stack = jnp.concatenate([hi, mid, lo], axis=0).astype(bf16)
        r = jnp.dot(stack, e_ref[...], preferred_element_type=f32)
        return r[0:n_rows] + r[n_rows:2 * n_rows] + r[2 * n_rows:3 * n_rows]

    heads_per_group = N_HEADS // N_GROUPS
    y = expand(eacs)
    for s in range(SSD_BS):
        for g in range(N_GROUPS):
            c8 = cm[s * SEQ_TILE:(s + 1) * SEQ_TILE, g * D_STATE:(g + 1) * D_STATE].astype(bf16)
            hg = h0_ref[s, g * half:(g + 1) * half, :].astype(bf16)
            yoff_scr[s * SEQ_TILE:(s + 1) * SEQ_TILE, g * half:(g + 1) * half] = lax.dot_general(
                c8, hg, (((1,), (1,)), ((), ())), preferred_element_type=f32)
    y = y * yoff_scr[...] + xs * dx_ref[...]
    for d in range(4):
        b_sh = pltpu.roll(bm, d, axis=0) if d else bm
        prod = cm * b_sh
        cbs = [jnp.sum(prod[:, g * D_STATE:(g + 1) * D_STATE], axis=-1, keepdims=True) for g in range(N_GROUPS)]
        cbh = jnp.where(lanes < heads_per_group, cbs[0], cbs[1])
        if d:
            coef = cbh * jnp.exp(acs - pltpu.roll(acs, d, axis=0)) * pltpu.roll(dt, d, axis=0)
        else:
            coef = cbh * dt
        coef = jnp.where(pos >= TOK0 + d, coef, 0.0)
        y = y + expand(coef) * (pltpu.roll(xs, d, axis=0) if d else xs)

    yn = _gate_norm(y, z_ref[...], nw_ref[...]).astype(bf16)
    y_ref[...] = jnp.dot(sel_ref[...], yn, preferred_element_type=f32).astype(bf16)

    xw = xs * expand(dt * dend)
    hi, mid, lo = _split3(expand(cdec))
    pos_w = lax.broadcasted_iota(jnp.int32, (n_rows, D_SSM), 0) % SEQ_TILE
    lhs_scr[...] = jnp.where(pos_w == 0, hi, jnp.where(pos_w == 1, mid, jnp.where(pos_w == 2, lo, xw)))
    ones = jnp.where(pos < 3, 1.0, 0.0)
    for g in range(N_GROUPS):
        rhs_scr[:, 2 * g * D_STATE:(2 * g + 1) * D_STATE] = jnp.where(tok, bm[:, g * D_STATE:(g + 1) * D_STATE], 0.0)
        rhs_scr[:, (2 * g + 1) * D_STATE:(2 * g + 2) * D_STATE] = ones
    for s in range(SSD_BS):
        for g in range(N_GROUPS):
            l8 = lhs_scr[s * SEQ_TILE:(s + 1) * SEQ_TILE, g * half:(g + 1) * half].astype(bf16)
            r8 = rhs_scr[s * SEQ_TILE:(s + 1) * SEQ_TILE, 2 * g * D_STATE:(2 * g + 2) * D_STATE].astype(bf16)
            res = lax.dot_general(l8, r8, (((0,), (0,)), ((), ())), preferred_element_type=f32)
            hnew_ref[s, g * half:(g + 1) * half, :] = (
                h0_ref[s, g * half:(g + 1) * half, :] * res[:, D_STATE:] + res[:, :D_STATE])


def _ssd_sample(pz8, pxbc8, pdt8, h0, ssm_w, e_mat, sel):
    cw, cb, dtb, alog, dx, nw = ssm_w
    n_seq = h0.shape[0]
    n_rows = SSD_BS * SEQ_TILE
    n_out = SSD_BS * 4
    rows_map = lambda i: (i, 0)
    fixed = lambda i: (0, 0)
    return pl.pallas_call(
        _ssd_sample_kernel,
        grid=(n_seq // SSD_BS,),
        in_specs=[
            pl.BlockSpec((n_rows, D_SSM), rows_map),
            pl.BlockSpec((n_rows, CONV_DIM), rows_map),
            pl.BlockSpec((n_rows, LANES), rows_map),
            pl.BlockSpec((SSD_BS, D_SSM, D_STATE), lambda i: (i, 0, 0)),
            pl.BlockSpec((SSM_CONV, CONV_DIM), fixed),
            pl.BlockSpec((1, CONV_DIM), fixed),
            pl.BlockSpec((1, LANES), fixed),
            pl.BlockSpec((1, LANES), fixed),
            pl.BlockSpec((1, D_SSM), fixed),
            pl.BlockSpec((1, D_SSM), fixed),
            pl.BlockSpec((LANES, D_SSM), fixed),
            pl.BlockSpec((n_out, n_rows), fixed),
        ],
        out_specs=(
            pl.BlockSpec((n_out, D_SSM), rows_map),
            pl.BlockSpec((SSD_BS, D_SSM, D_STATE), lambda i: (i, 0, 0)),
        ),
        out_shape=(
            jax.ShapeDtypeStruct((n_seq * 4, D_SSM), bf16),
            jax.ShapeDtypeStruct((n_seq, D_SSM, D_STATE), f32),
        ),
        scratch_shapes=[
            pltpu.VMEM((n_rows, D_SSM), f32),
            pltpu.VMEM((n_rows, D_SSM), f32),
            pltpu.VMEM((n_rows, 2 * N_GROUPS * D_STATE), f32),
        ],
        compiler_params=_cparams(("arbitrary",)),
        name="ssd_sample",
    )(pz8, pxbc8, pdt8, h0, cw, cb, dtb, alog, dx, nw, e_mat, sel)


N_LT = D_CONF // LANES
HIST = 32
CONF_RC = 64


def _conf_norm_act(acc, g, beta):
    mu = jnp.mean(acc, axis=-1, keepdims=True)
    xc = acc - mu
    var = jnp.mean(xc * xc, axis=-1, keepdims=True)
    return _silu(xc * lax.rsqrt(var + EPS) * g + beta)


def _conf_conv_tile(s_tile, w_ref, p_ref, lt, y_scr, tt):
    off = HIST - (CONF_KERNEL - 1)
    for r0 in range(0, tt, CONF_RC):
        acc = jnp.broadcast_to(p_ref[lt, 0:1, :], (CONF_RC, LANES))
        for k in range(CONF_KERNEL):
            acc = acc + w_ref[lt, k:k + 1, :] * s_tile[r0 + off + k:r0 + off + k + CONF_RC, :]
        y_scr[lt, r0:r0 + CONF_RC, :] = _conf_norm_act(acc, p_ref[lt, 1:2, :], p_ref[lt, 2:3, :])


def _conf_tile_kernel(ca_ref, cg_ref, hist_ref, w_ref, p_ref, y_ref, tail_ref, s_scr, y_scr, *, tt, n_valid):
    t = pl.program_id(1)

    @pl.when(t == 0)
    def _():
        for lt in range(N_LT):
            s_scr[lt, 0:HIST, :] = hist_ref[:, lt * LANES:(lt + 1) * LANES]

    @pl.when(t > 0)
    def _():
        s_scr[:, 0:HIST, :] = s_scr[:, tt:tt + HIST, :]

    glu = ca_ref[...] * _sigmoid(cg_ref[...])
    for lt in range(N_LT):
        s_scr[lt, HIST:HIST + tt, :] = glu[:, lt * LANES:(lt + 1) * LANES]

    def lane_tile(lt, carry):
        _conf_conv_tile(s_scr.at[lt], w_ref, p_ref, lt, y_scr, tt)
        return carry

    lax.fori_loop(0, N_LT, lane_tile, 0)
    for lt in range(N_LT):
        y_ref[:, lt * LANES:(lt + 1) * LANES] = y_scr[lt].astype(bf16)
        tail_ref[:, lt * LANES:(lt + 1) * LANES] = s_scr[lt, n_valid:n_valid + HIST, :]


def _conf_tiles(pc, hist, conf_w, *, n_seq, n_tiles, tt, row_block0, n_valid):
    w3, p3 = conf_w
    in_specs = [
        pl.BlockSpec((tt, D_CONF), lambda b, t: (row_block0 + b * n_tiles + t, 0)),
        pl.BlockSpec((tt, D_CONF), lambda b, t: (row_block0 + b * n_tiles + t, 1)),
        pl.BlockSpec((HIST, D_CONF), lambda b, t: (0, 0)),
        pl.BlockSpec((N_LT, HIST, LANES), lambda b, t: (0, 0, 0)),
        pl.BlockSpec((N_LT, SUBLANES, LANES), lambda b, t: (0, 0, 0)),
    ]
    return pl.pallas_call(
        functools.partial(_conf_tile_kernel, tt=tt, n_valid=n_valid),
        grid=(n_seq, n_tiles),
        in_specs=in_specs,
        out_specs=(
            pl.BlockSpec((tt, D_CONF), lambda b, t: (b * n_tiles + t, 0)),
            pl.BlockSpec((None, HIST, D_CONF), lambda b, t: (b, 0, 0)),
        ),
        out_shape=(
            jax.ShapeDtypeStruct((n_seq * n_tiles * tt, D_CONF), bf16),
            jax.ShapeDtypeStruct((n_seq, HIST, D_CONF), f32),
        ),
        scratch_shapes=[
            pltpu.VMEM((N_LT, HIST + tt, LANES), f32),
            pltpu.VMEM((N_LT, tt, LANES), f32),
        ],
        compiler_params=_cparams(("arbitrary", "arbitrary")),
        name="conf_tiles",
    )(pc, pc, hist, w3, p3)


CONF_BS = 8
WIN = 40


def _conf_sample_kernel(ca_ref, cg_ref, buf_ref, w_ref, p_ref, sel_ref, y_ref, newbuf_ref, w_scr, y8_scr):
    n_hist = CONF_KERNEL - 1
    glu = ca_ref[...] * _sigmoid(cg_ref[...])
    for s in range(CONF_BS):
        w_scr[s, 0:SUBLANES, :] = jnp.zeros((SUBLANES, D_CONF), f32)
        w_scr[s, WIN - SEQ_TILE:WIN, :] = glu[s * SEQ_TILE:(s + 1) * SEQ_TILE, :]
        w_scr[s, WIN - 4 - n_hist:WIN - 4, :] = buf_ref[s]
        newbuf_ref[s] = w_scr[s, WIN - n_hist:WIN, :]
    off = WIN - SEQ_TILE - n_hist

    def one_seq(s, carry):
        row0 = pl.multiple_of(s * SEQ_TILE, SEQ_TILE)
        for lt in range(N_LT):
            sl = slice(lt * LANES, (lt + 1) * LANES)
            acc = jnp.broadcast_to(p_ref[lt, 0:1, :], (SEQ_TILE, LANES))
            for k in range(CONF_KERNEL):
                acc = acc + w_ref[lt, k:k + 1, :] * w_scr[s, off + k:off + k + SEQ_TILE, sl]
            y8_scr[pl.ds(row0, SEQ_TILE), sl] = _conf_norm_act(acc, p_ref[lt, 1:2, :], p_ref[lt, 2:3, :])
        return carry

    lax.fori_loop(0, CONF_BS, one_seq, 0)
    y_ref[...] = jnp.dot(sel_ref[...], y8_scr[...].astype(bf16), preferred_element_type=f32).astype(bf16)


def _conf_sample(pc8, buf, conf_w, sel):
    w3, p3 = conf_w
    n_seq = buf.shape[0]
    n_rows = CONF_BS * SEQ_TILE
    n_out = CONF_BS * 4
    n_hist = CONF_KERNEL - 1
    return pl.pallas_call(
        _conf_sample_kernel,
        grid=(n_seq // CONF_BS,),
        in_specs=[
            pl.BlockSpec((n_rows, D_CONF), lambda i: (i, 0)),
            pl.BlockSpec((n_rows, D_CONF), lambda i: (i, 1)),
            pl.BlockSpec((CONF_BS, n_hist, D_CONF), lambda i: (i, 0, 0)),
            pl.BlockSpec((N_LT, HIST, LANES), lambda i: (0, 0, 0)),
            pl.BlockSpec((N_LT, SUBLANES, LANES), lambda i: (0, 0, 0)),
            pl.BlockSpec((n_out, n_rows), lambda i: (0, 0)),
        ],
        out_specs=(
            pl.BlockSpec((n_out, D_CONF), lambda i: (i, 0)),
            pl.BlockSpec((CONF_BS, n_hist, D_CONF), lambda i: (i, 0, 0)),
        ),
        out_shape=(
            jax.ShapeDtypeStruct((n_seq * 4, D_CONF), bf16),
            jax.ShapeDtypeStruct((n_seq, n_hist, D_CONF), f32),
        ),
        scratch_shapes=[
            pltpu.VMEM((CONF_BS, WIN, D_CONF), f32),
            pltpu.VMEM((n_rows, D_CONF), f32),
        ],
        compiler_params=_cparams(("arbitrary",)),
        name="conf_sample",
    )(pc8, pc8, buf, w3, p3, sel)


def _seq_tiles(rows, n_seq, n_tok):
    w = rows.shape[-1]
    return jnp.pad(rows.reshape(n_seq, n_tok, w), ((0, 0), (SEQ_TILE - n_tok, 0), (0, 0))).reshape(n_seq * SEQ_TILE, w)


def kernel(x_prompt, x_sample, state_ssm, state_ssm_conv, state_conf_conv, meta_tokens, ffn1_norm, ffn1_w_gate, ffn1_w_up, ffn1_w_down, mix_norm, w_in, ssm_conv_w, ssm_conv_b, ssm_dt_bias, ssm_A_log, ssm_D, ssm_norm, conf_conv_w, conf_conv_b, conf_ln_g, conf_ln_b, w_out, ffn2_norm, ffn2_w_gate, ffn2_w_up, ffn2_w_down, final_norm):
    n_pb, n_pt = x_prompt.shape[0], x_prompt.shape[1]
    n_sb, n_st = x_sample.shape[0], x_sample.shape[1]
    assert (n_pb * n_pt, n_sb * n_st, n_st) == (N_PROMPT_ROWS, N_SAMPLE_ROWS, 4) and n_pt % CHUNK == 0

    row = lambda v: v.reshape(1, -1)
    x_rows = (x_prompt.reshape(N_PROMPT_ROWS, D_MODEL), x_sample.reshape(N_SAMPLE_ROWS, D_MODEL), meta_tokens)
    h, u = _ffn(x_rows, row(ffn1_norm[0]), _to_bf16(ffn1_w_gate, 256), _to_bf16(ffn1_w_up, 256),
                _to_bf16(ffn1_w_down, 704), row(mix_norm[0]))

    w_t = _to_bf16(jnp.transpose(w_in[0]), 672)
    c_in0 = D_SSM + CONV_DIM
    pzx = _matmul_nt(u, w_t, row0=0, n=c_in0, tn=1536)
    pdt = _matmul_nt(u, w_t, row0=c_in0, n=LANES, tn=LANES)
    pc = _matmul_nt(u, w_t, row0=c_in0 + N_HEADS, n=2 * D_CONF, tn=1024)

    pad_heads = lambda v: jnp.pad(v.reshape(1, N_HEADS), ((0, 0), (0, LANES - N_HEADS)))
    ssm_w = (ssm_conv_w[0], row(ssm_conv_b[0]), pad_heads(ssm_dt_bias[0]), pad_heads(ssm_A_log[0]),
             row(jnp.repeat(ssm_D[0], HEAD_DIM)), row(ssm_norm[0]))
    tri = jnp.tril(jnp.ones((CHUNK, CHUNK), f32))
    e_mat = (jnp.arange(LANES)[:, None] == (jnp.arange(D_SSM)[None, :] // HEAD_DIM)).astype(bf16)
    lane_major = lambda v: v.reshape(v.shape[0], N_LT, LANES).transpose(1, 0, 2)
    conf_w = (lane_major(jnp.pad(conf_conv_w[0], ((0, HIST - CONF_KERNEL), (0, 0)))),
              lane_major(jnp.pad(jnp.stack([conf_conv_b[0], conf_ln_g[0], conf_ln_b[0]]), ((0, SUBLANES - 3), (0, 0)))))
    out_rows = jnp.arange(SSD_BS * 4)
    sel = ((out_rows // 4) * SEQ_TILE + TOK0 + out_rows % 4)[:, None] == jnp.arange(SSD_BS * SEQ_TILE)[None, :]
    sel = sel.astype(bf16)

    meta_block = ROW_META // CHUNK
    ys_m, h_meta, tail_meta = _ssd_chunks(
        pzx, pdt, jnp.zeros((SUBLANES, CONV_DIM), f32), jnp.zeros((D_STATE, D_SSM), f32), ssm_w, (tri, e_mat),
        n_seq=1, n_chunks=1, row_block0=meta_block, n_valid=N_META, transpose_out=False)
    yc_m, hist_meta = _conf_tiles(pc, jnp.zeros((HIST, D_CONF), f32), conf_w,
                                  n_seq=1, n_tiles=1, tt=CHUNK, row_block0=meta_block, n_valid=N_META)

    ys_p, p_ssm, p_tail = _ssd_chunks(
        pzx, pdt, tail_meta[0], h_meta[0], ssm_w, (tri, e_mat),
        n_seq=n_pb, n_chunks=n_pt // CHUNK, row_block0=0, n_valid=CHUNK, transpose_out=True)
    conf_tt = 256
    yc_p, p_hist = _conf_tiles(pc, hist_meta[0], conf_w,
                               n_seq=n_pb, n_tiles=n_pt // conf_tt, tt=conf_tt, row_block0=0, n_valid=conf_tt)

    smp = slice(ROW_SAMPLE, ROW_SAMPLE + N_SAMPLE_ROWS)
    pzx_s = pzx[smp]
    pxbc_s = pzx_s[:, D_SSM:].reshape(n_sb, n_st, CONV_DIM)
    pxbc8 = jnp.concatenate([jnp.zeros((n_sb, 1, CONV_DIM), f32), state_ssm_conv[0], pxbc_s], axis=1)
    ys_s, s_ssm = _ssd_sample(_seq_tiles(pzx_s[:, :D_SSM], n_sb, n_st), pxbc8.reshape(n_sb * SEQ_TILE, CONV_DIM),
                              _seq_tiles(pdt[smp], n_sb, n_st), state_ssm[0].reshape(n_sb, D_SSM, D_STATE),
                              ssm_w, e_mat, sel)
    yc_s, s_cconv = _conf_sample(_seq_tiles(pc[smp], n_sb, n_st), state_conf_conv[0], conf_w, sel)

    h2 = _outproj(h, (ys_p, ys_s, ys_m), (yc_p, yc_s, yc_m), _to_bf16(w_out, 512))
    y_prompt, y_sample = _ffn((h2,), row(ffn2_norm[0]), _to_bf16(ffn2_w_gate, 256), _to_bf16(ffn2_w_up, 256),
                              _to_bf16(ffn2_w_down, 704), row(final_norm), dst_rows=(N_PROMPT_ROWS, N_SAMPLE_ROWS))
    n_hist = CONF_KERNEL - 1
    return (
        y_prompt.reshape(n_pb, n_pt, D_MODEL), y_sample.reshape(n_sb, n_st, D_MODEL),
        p_ssm.reshape(1, n_pb, N_HEADS, HEAD_DIM, D_STATE),
        p_tail[:, SUBLANES - (SSM_CONV - 1):, :][None],
        p_hist[:, HIST - n_hist:, :][None],
        s_ssm.reshape(1, n_sb, N_HEADS, HEAD_DIM, D_STATE),
        pxbc_s[:, n_st - (SSM_CONV - 1):, :][None],
        s_cconv[None],
    )
```

```python
import functools

import jax
import jax.numpy as jnp
from jax import lax
from jax.experimental import pallas as pl
from jax.experimental.pallas import tpu as pltpu

f32 = jnp.float32
bf16 = jnp.bfloat16

D_MODEL = 2048
N_META = 16
D_SSM = 2048
D_CONF = 2048
HEAD_DIM = 64
N_HEADS = 32
N_GROUPS = 2
D_STATE = 128
SSM_CONV = 4
CONV_DIM = D_SSM + 2 * N_GROUPS * D_STATE
CONF_KERNEL = 31
CONF_GROUP = 128
D_FF = 5632
EPS = 1e-5

LANES = 128
SUBLANES = 8
CHUNK = 128
VMEM_LIMIT = 56 * 1024 * 1024

N_PROMPT_ROWS = 4 * 2048
N_SAMPLE_ROWS = 128 * 4
ROW_SAMPLE = N_PROMPT_ROWS
ROW_META = N_PROMPT_ROWS + N_SAMPLE_ROWS
N_ROWS = ROW_META + CHUNK
TM = 736
TF = 256


def _cparams(sem):
    return pltpu.CompilerParams(dimension_semantics=sem, vmem_limit_bytes=VMEM_LIMIT)


def _sigmoid(x):
    return 1.0 / (1.0 + jnp.exp(-x))


def _silu(x):
    return x * _sigmoid(x)


def _rms(x, g):
    return x * lax.rsqrt(jnp.mean(x * x, axis=-1, keepdims=True) + EPS) * g


def _softplus(x):
    return jnp.maximum(x, 0.0) + jnp.log1p(jnp.exp(-jnp.abs(x)))


def _split3(v):
    hi = v.astype(bf16).astype(f32)
    r1 = v - hi
    mid = r1.astype(bf16).astype(f32)
    return hi, mid, r1 - mid


def _row_segments(src_rows, n_total, tm):
    bounds, start = [], 0
    for k, n in enumerate(src_rows):
        bounds.append((k, start, n))
        start += n
    if start < n_total:
        bounds.append((None, start, n_total - start))
    table = []
    for t in range(n_total // tm):
        lo, hi = t * tm, (t + 1) * tm
        segs = []
        for k, s, n in bounds:
            a, b = max(lo, s), min(hi, s + n)
            if a < b:
                segs.append((k, a - s, a - lo, b - a))
        table.append(segs)
    return table


def _load_tile(i, seg_table, groups, sem):
    tm = groups[0][2].shape[0]
    special = [t for t, segs in enumerate(seg_table) if segs != [(0, t * tm, 0, tm)]]
    first = special[0] if special else len(seg_table)
    assert special == list(range(first, len(seg_table)))

    @pl.when(i < first)
    def _():
        for main_ref, _, dst_ref in groups:
            dst_ref[...] = main_ref[...]

    for t in special:
        @pl.when(i == t)
        def _(segs=seg_table[t]):
            copies = []
            for main_ref, tail_srcs, dst_ref in groups:
                for k, r0, d0, n in segs:
                    if k is None:
                        dst_ref[d0:d0 + n, :] = jnp.zeros((n, dst_ref.shape[1]), dst_ref.dtype)
                    elif k == 0:
                        dst_ref[d0:d0 + n, :] = main_ref[d0:d0 + n, :]
                    else:
                        cp = pltpu.make_async_copy(tail_srcs[k - 1].at[pl.ds(r0, n)], dst_ref.at[pl.ds(d0, n)], sem)
                        cp.start()
                        copies.append(cp)
            for cp in copies:
                cp.wait()


def _side_cast(w, block_rows, grid):
    r, c = w.shape[-2:]
    n_blocks = r // block_rows
    assert r % block_rows == 0 and n_blocks <= grid[0] * grid[1]
    blk = lambda i, j: jnp.minimum(i * grid[1] + j, n_blocks - 1)
    if w.ndim == 3:
        in_spec = pl.BlockSpec((None, block_rows, c), lambda i, j: (0, blk(i, j), 0))
    else:
        in_spec = pl.BlockSpec((block_rows, c), lambda i, j: (blk(i, j), 0))
    return in_spec, pl.BlockSpec((block_rows, c), lambda i, j: (blk(i, j), 0)), jax.ShapeDtypeStruct((r, c), bf16)


def _run_side_cast(side_refs):
    for src_ref, dst_ref in side_refs:
        dst_ref[...] = src_ref[...].astype(bf16)


def _drain_rows(i, out_table, dsts, acc_scr, sems):
    def copies(t):
        return [pltpu.make_async_copy(acc_scr.at[t % 2, pl.ds(s0, n)], dsts[k].at[pl.ds(r0, n)], sems.at[t % 2])
                for k, r0, s0, n in out_table[t] if k is not None]

    n_tiles = len(out_table)
    for t in range(n_tiles):
        @pl.when(i == t)
        def _(t=t):
            if t > 0:
                for cp in copies(t - 1):
                    cp.wait()
            for cp in copies(t):
                cp.start()
            if t == n_tiles - 1:
                for cp in copies(t):
                    cp.wait()


def _ffn_kernel(*refs, seg_table, out_table, n_src, n_side):
    x_ref, tail_srcs = refs[0], refs[1:n_src]
    nw_ref, wg_ref, wu_ref, wd_ref, pw_ref = refs[n_src:n_src + 5]
    pos = n_src + 5
    side_in, pos = refs[pos:pos + n_side], pos + n_side
    i = pl.program_id(0)
    f = pl.program_id(1)
    if out_table is None:
        (acc, yn_ref), pos = refs[pos:pos + 2], pos + 2
        side_out, pos = refs[pos:pos + n_side], pos + n_side
        xn_ref, sem = refs[pos:]
    else:
        n_dst = 1 + max(k for segs in out_table for k, _, _, _ in segs if k is not None)
        dsts, pos = refs[pos:pos + n_dst], pos + n_dst
        side_out, pos = refs[pos:pos + n_side], pos + n_side
        acc_scr, xn_ref, sem, out_sems = refs[pos:]
        acc = acc_scr.at[i % 2]
    _run_side_cast(zip(side_in, side_out))

    @pl.when(f == 0)
    def _():
        _load_tile(i, seg_table, [(x_ref, tail_srcs, acc)], sem)
        xn_ref[...] = _rms(acc[...], nw_ref[...]).astype(bf16)

    xn = xn_ref[...]
    g = jnp.dot(xn, wg_ref[...], preferred_element_type=f32)
    u = jnp.dot(xn, wu_ref[...], preferred_element_type=f32)
    a = (_silu(g) * u).astype(bf16)
    acc[...] += 0.5 * jnp.dot(a, wd_ref[...], preferred_element_type=f32)

    @pl.when(f == pl.num_programs(1) - 1)
    def _():
        if out_table is None:
            yn_ref[...] = _rms(acc[...], pw_ref[...]).astype(yn_ref.dtype)
        else:
            acc[...] = _rms(acc[...], pw_ref[...])
            _drain_rows(i, out_table, dsts, acc_scr, out_sems)


def _ffn(srcs, norm_w, wg, wu, wd, post_w, *, dst_rows=None, side=None):
    seg_table = _row_segments([s.shape[0] for s in srcs], N_ROWS, TM)
    grid = (N_ROWS // TM, D_FF // TF)
    side_specs = [_side_cast(*side, grid)] if side else []
    row_spec = pl.BlockSpec((TM, D_MODEL), lambda i, f: (i, 0))
    vec_spec = pl.BlockSpec((1, D_MODEL), lambda i, f: (0, 0))
    scratch = [pltpu.VMEM((TM, D_MODEL), bf16), pltpu.SemaphoreType.DMA(())]
    if dst_rows is None:
        out_table = None
        out_shape = (jax.ShapeDtypeStruct((N_ROWS, D_MODEL), f32), jax.ShapeDtypeStruct((N_ROWS, D_MODEL), bf16))
        out_specs = (row_spec, row_spec)
    else:
        out_table = _row_segments(dst_rows, N_ROWS, TM)
        out_shape = tuple(jax.ShapeDtypeStruct((n, D_MODEL), f32) for n in dst_rows)
        out_specs = tuple(pl.BlockSpec(memory_space=pl.ANY) for _ in dst_rows)
        scratch = [pltpu.VMEM((2, TM, D_MODEL), f32)] + scratch + [pltpu.SemaphoreType.DMA((2,))]
    return pl.pallas_call(
        functools.partial(_ffn_kernel, seg_table=seg_table, out_table=out_table, n_src=len(srcs),
                          n_side=len(side_specs)),
        grid=grid,
        in_specs=[row_spec] + [pl.BlockSpec(memory_space=pl.ANY)] * (len(srcs) - 1) + [
            vec_spec,
            pl.BlockSpec((D_MODEL, TF), lambda i, f: (0, f)),
            pl.BlockSpec((D_MODEL, TF), lambda i, f: (0, f)),
            pl.BlockSpec((TF, D_MODEL), lambda i, f: (f, 0)),
            vec_spec,
        ] + [s[0] for s in side_specs],
        out_specs=tuple(out_specs) + tuple(s[1] for s in side_specs),
        out_shape=tuple(out_shape) + tuple(s[2] for s in side_specs),
        scratch_shapes=scratch,
        compiler_params=_cparams(("arbitrary", "arbitrary")),
        name="ffn",
    )(*srcs, norm_w, wg, wu, wd, post_w, *([side[0]] if side else []))


def _cast_kernel(x_ref, o_ref):
    o_ref[...] = x_ref[...].astype(bf16)


def _to_bf16(w, block_rows):
    r, c = w.shape[-2:]
    if w.ndim == 3:
        in_spec = pl.BlockSpec((None, block_rows, c), lambda i: (0, i, 0))
    else:
        in_spec = pl.BlockSpec((block_rows, c), lambda i: (i, 0))
    return pl.pallas_call(
        _cast_kernel,
        grid=(r // block_rows,),
        in_specs=[in_spec],
        out_specs=pl.BlockSpec((block_rows, c), lambda i: (i, 0)),
        out_shape=jax.ShapeDtypeStruct((r, c), bf16),
        compiler_params=_cparams(("arbitrary",)),
        name="cast_bf16",
    )(w)


def _matmul_nt_kernel(a_ref, w_ref, *rest):
    n_side = len(rest) // 2
    o_ref = rest[n_side]
    _run_side_cast(zip(rest[:n_side], rest[n_side + 1:]))
    o_ref[...] = lax.dot_general(a_ref[...], w_ref[...], (((1,), (1,)), ((), ())), preferred_element_type=f32)


def _matmul_nt(a, w_t, *, row0, n, tn, side=None):
    n_rows, k = a.shape
    align = 2 * 16
    assert row0 % align == 0 and tn % align == 0 and n % tn == 0
    grid = (n_rows // TM, n // tn)
    side_specs = [_side_cast(*side, grid)] if side else []
    out = pl.pallas_call(
        _matmul_nt_kernel,
        grid=grid,
        in_specs=[
            pl.BlockSpec((TM, k), lambda i, j: (i, 0)),
            pl.BlockSpec((pl.Element(tn), pl.Element(k)), lambda i, j: (pl.multiple_of(row0 + tn * j, align), 0)),
        ] + [s[0] for s in side_specs],
        out_specs=(pl.BlockSpec((TM, tn), lambda i, j: (i, j)),) + tuple(s[1] for s in side_specs),
        out_shape=(jax.ShapeDtypeStruct((n_rows, n), f32),) + tuple(s[2] for s in side_specs),
        compiler_params=_cparams(("arbitrary", "arbitrary")),
        name="in_proj",
    )(a, w_t, *([side[0]] if side else []))
    return out if side else out[0]


def _outproj_kernel(*refs, seg_table, n_src, n_side):
    ys_ref, ys_tails = refs[0], refs[1:n_src]
    yc_ref, yc_tails = refs[n_src], refs[n_src + 1:2 * n_src]
    h_ref, w0_ref, w1_ref = refs[2 * n_src:2 * n_src + 3]
    pos = 2 * n_src + 3
    side_in, o_ref, side_out = refs[pos:pos + n_side], refs[pos + n_side], refs[pos + n_side + 1:pos + 2 * n_side + 1]
    ys_scr, yc_scr, sem = refs[pos + 2 * n_side + 1:]
    _run_side_cast(zip(side_in, side_out))

    @pl.when(pl.program_id(1) == 0)
    def _():
        _load_tile(pl.program_id(0), seg_table, [(ys_ref, ys_tails, ys_scr), (yc_ref, yc_tails, yc_scr)], sem)

    acc = jnp.dot(ys_scr[...], w0_ref[...], preferred_element_type=f32)
    acc = acc + jnp.dot(yc_scr[...], w1_ref[...], preferred_element_type=f32)
    o_ref[...] = h_ref[...] + acc


def _outproj(h, ys_srcs, yc_srcs, w_out, tn=1024, side=None):
    n_rows = h.shape[0]
    seg_table = _row_segments([s.shape[0] for s in ys_srcs], n_rows, TM)
    n_src = len(ys_srcs)
    grid = (n_rows // TM, D_MODEL // tn)
    side_specs = [_side_cast(*side, grid)] if side else []
    y_specs = [pl.BlockSpec((TM, D_SSM), lambda i, j: (i, 0))] + [pl.BlockSpec(memory_space=pl.ANY)] * (n_src - 1)
    out = pl.pallas_call(
        functools.partial(_outproj_kernel, seg_table=seg_table, n_src=n_src, n_side=len(side_specs)),
        grid=grid,
        in_specs=y_specs + y_specs + [
            pl.BlockSpec((TM, tn), lambda i, j: (i, j)),
            pl.BlockSpec((D_SSM, tn), lambda i, j: (0, j)),
            pl.BlockSpec((D_CONF, tn), lambda i, j: (1, j)),
        ] + [s[0] for s in side_specs],
        out_specs=(pl.BlockSpec((TM, tn), lambda i, j: (i, j)),) + tuple(s[1] for s in side_specs),
        out_shape=(jax.ShapeDtypeStruct((n_rows, D_MODEL), f32),) + tuple(s[2] for s in side_specs),
        scratch_shapes=[pltpu.VMEM((TM, D_SSM), bf16), pltpu.VMEM((TM, D_CONF), bf16), pltpu.SemaphoreType.DMA(())],
        compiler_params=_cparams(("arbitrary", "arbitrary")),
        name="out_proj",
    )(*ys_srcs, *yc_srcs, h, w_out, w_out, *([side[0]] if side else []))
    return out if side else out[0]


def _gate_norm(y, z, nw):
    y = y * _silu(z)
    half = D_SSM // N_GROUPS
    parts = []
    for g in range(N_GROUPS):
        yg = y[:, g * half:(g + 1) * half]
        ms = jnp.mean(yg * yg, axis=-1, keepdims=True)
        parts.append(yg * lax.rsqrt(ms + EPS) * nw[:, g * half:(g + 1) * half])
    return jnp.concatenate(parts, axis=1)


def _ssd_chunk_math(z, xbc, dt_raw, cinit_ref, hinit_ref, cw_ref, cb_ref, dtb_ref, alog_ref,
                    dx_ref, nw_ref, tri_ref, e_ref, y_ref, hout_ref, tail_ref, win_scr, ht_scr, y_scr,
                    *, n_valid, transpose_out):
    c = pl.program_id(1)

    @pl.when(c == 0)
    def _():
        ht_scr[...] = hinit_ref[...]
        win_scr[0:SUBLANES, :] = cinit_ref[...]

    win_scr[SUBLANES:SUBLANES + CHUNK, :] = xbc
    acc = cb_ref[...] + cw_ref[SSM_CONV - 1:SSM_CONV, :] * xbc
    for j in range(1, SSM_CONV):
        acc = acc + cw_ref[SSM_CONV - 1 - j:SSM_CONV - j, :] * win_scr[SUBLANES - j:SUBLANES - j + CHUNK, :]
    tail_ref[...] = win_scr[n_valid:n_valid + SUBLANES, :]
    win_scr[0:SUBLANES, :] = win_scr[CHUNK:CHUNK + SUBLANES, :]
    xc = _silu(acc)
    b_mats = [xc[:, D_SSM + g * D_STATE:D_SSM + (g + 1) * D_STATE] for g in range(N_GROUPS)]
    c_mats = [xc[:, D_SSM + (N_GROUPS + g) * D_STATE:D_SSM + (N_GROUPS + g + 1) * D_STATE] for g in range(N_GROUPS)]

    rows = lax.broadcasted_iota(jnp.int32, (CHUNK, LANES), 0)
    lanes = lax.broadcasted_iota(jnp.int32, (CHUNK, LANES), 1)
    dt = _softplus(dt_raw + dtb_ref[...])
    if n_valid < CHUNK:
        dt = jnp.where(rows < n_valid, dt, 0.0)
    a = dt * (-jnp.exp(alog_ref[...]))
    acs = jnp.dot(tri_ref[...], a, precision=lax.Precision.HIGHEST, preferred_element_type=f32)
    total = acs[CHUNK - 1:CHUNK, :]
    dend = jnp.exp(total - acs)
    acs_t = acs.T
    dt_t = dt.T
    w_t = (dt * dend).T
    hi, mid, lo = _split3(jnp.exp(total))
    rid = lax.broadcasted_iota(jnp.int32, (SUBLANES, LANES), 0)
    cd3 = jnp.where(rid == 0, hi, jnp.where(rid == 1, mid, jnp.where(rid == 2, lo, 0.0)))
    cdx = jnp.sum(jnp.dot(cd3.astype(bf16), e_ref[...], preferred_element_type=f32), axis=0, keepdims=True)

    cb_mats, bt_mats = [], []
    for g in range(N_GROUPS):
        cb_mats.append(lax.dot_general(c_mats[g].astype(bf16), b_mats[g].astype(bf16), (((1,), (1,)), ((), ())),
                                       preferred_element_type=f32))
        bt_mats.append(b_mats[g].T)

    lane_lo = lanes < HEAD_DIM
    tril = rows >= lanes
    heads_per_group = N_HEADS // N_GROUPS
    for j in range(N_HEADS // 2):
        sl = slice(j * LANES, (j + 1) * LANES)
        xs_pair = xc[:, sl]
        ht_pair = ht_scr[:, sl]
        y_pair = jnp.zeros((CHUNK, LANES), f32)
        upd = jnp.zeros((D_STATE, LANES), f32)
        for half in range(2):
            h = 2 * j + half
            g = h // heads_per_group
            keep = lane_lo if half == 0 else jnp.logical_not(lane_lo)
            xs_m = jnp.where(keep, xs_pair, 0.0).astype(bf16)
            ht_m = jnp.where(keep, ht_pair, 0.0).astype(bf16)
            acs_col = jnp.broadcast_to(acs[:, h:h + 1], (CHUNK, CHUNK))
            acs_row = jnp.broadcast_to(acs_t[h:h + 1, :], (CHUNK, CHUNK))
            lmat = jnp.exp(jnp.where(tril, acs_col - acs_row, -jnp.inf))
            m_h = cb_mats[g] * lmat * dt_t[h:h + 1, :]
            ce_h = c_mats[g] * jnp.exp(acs_col)
            lhs = jnp.concatenate([m_h, ce_h], axis=1).astype(bf16)
            rhs = jnp.concatenate([xs_m, ht_m], axis=0)
            y_pair = y_pair + jnp.dot(lhs, rhs, preferred_element_type=f32)
            bw = (bt_mats[g] * w_t[h:h + 1, :]).astype(bf16)
            upd = upd + jnp.dot(bw, xs_m, preferred_element_type=f32)
        y_scr[:, sl] = y_pair
        ht_scr[:, sl] = ht_pair * cdx[:, sl] + upd

    xs = xc[:, :D_SSM]
    y = y_scr[...] + xs * dx_ref[...]
    y_ref[...] = _gate_norm(y, z(), nw_ref[...]).astype(bf16)

    @pl.when(c == pl.num_programs(1) - 1)
    def _():
        if transpose_out:
            hout_ref[...] = ht_scr[...].T
        else:
            hout_ref[...] = ht_scr[...]


N_SSD_PARAMS = 10


def _ssd_chunk_kernel(z_ref, xs_ref, bc_ref, dt_ref, *rest, n_valid, transpose_out, n_side):
    params, rest = rest[:N_SSD_PARAMS], rest[N_SSD_PARAMS:]
    side_in, outs = rest[:n_side], rest[n_side:n_side + 3]
    side_out, scratch = rest[n_side + 3:2 * n_side + 3], rest[2 * n_side + 3:]
    _run_side_cast(zip(side_in, side_out))
    xbc = jnp.concatenate([xs_ref[...], bc_ref[...]], axis=1)
    _ssd_chunk_math(lambda: z_ref[...], xbc, dt_ref[...], *params, *outs, *scratch,
                    n_valid=n_valid, transpose_out=transpose_out)


def _ssd_chunks(pzx, pdt, cinit, hinit, ssm_w, consts, *, n_seq, n_chunks, row_block0, n_valid, transpose_out,
                side=None):
    cw, cb, dtb, alog, dx, nw = ssm_w
    tri, e_mat = consts
    bc_dim = CONV_DIM - D_SSM
    side_specs = [_side_cast(*side, (n_seq, n_chunks))] if side else []
    rows_map = lambda b, c: (row_block0 + b * n_chunks + c, 0)
    fixed = lambda b, c: (0, 0)
    in_specs = [
        pl.BlockSpec((CHUNK, D_SSM), rows_map),
        pl.BlockSpec((CHUNK, D_SSM), lambda b, c: (row_block0 + b * n_chunks + c, 1)),
        pl.BlockSpec((CHUNK, bc_dim), lambda b, c: (row_block0 + b * n_chunks + c, 2 * D_SSM // bc_dim)),
        pl.BlockSpec((CHUNK, LANES), rows_map),
        pl.BlockSpec((SUBLANES, CONV_DIM), fixed),
        pl.BlockSpec((D_STATE, D_SSM), fixed),
        pl.BlockSpec((SSM_CONV, CONV_DIM), fixed),
        pl.BlockSpec((1, CONV_DIM), fixed),
        pl.BlockSpec((1, LANES), fixed),
        pl.BlockSpec((1, LANES), fixed),
        pl.BlockSpec((1, D_SSM), fixed),
        pl.BlockSpec((1, D_SSM), fixed),
        pl.BlockSpec((CHUNK, CHUNK), fixed),
        pl.BlockSpec((LANES, D_SSM), fixed),
    ]
    args = [pzx, pzx, pzx, pdt, cinit, hinit, cw, cb, dtb, alog, dx, nw, tri, e_mat] + ([side[0]] if side else [])
    hshape = (n_seq, D_SSM, D_STATE) if transpose_out else (n_seq, D_STATE, D_SSM)
    return pl.pallas_call(
        functools.partial(_ssd_chunk_kernel, n_valid=n_valid, transpose_out=transpose_out, n_side=len(side_specs)),
        grid=(n_seq, n_chunks),
        in_specs=in_specs + [s[0] for s in side_specs],
        out_specs=(
            pl.BlockSpec((CHUNK, D_SSM), lambda b, c: (b * n_chunks + c, 0)),
            pl.BlockSpec((None,) + hshape[1:], lambda b, c: (b, 0, 0)),
            pl.BlockSpec((None, SUBLANES, CONV_DIM), lambda b, c: (b, 0, 0)),
        ) + tuple(s[1] for s in side_specs),
        out_shape=(
            jax.ShapeDtypeStruct((n_seq * n_chunks * CHUNK, D_SSM), bf16),
            jax.ShapeDtypeStruct(hshape, f32),
            jax.ShapeDtypeStruct((n_seq, SUBLANES, CONV_DIM), f32),
        ) + tuple(s[2] for s in side_specs),
        scratch_shapes=[
            pltpu.VMEM((SUBLANES + CHUNK, CONV_DIM), f32),
            pltpu.VMEM((D_STATE, D_SSM), f32),
            pltpu.VMEM((CHUNK, D_SSM), f32),
        ],
        compiler_params=_cparams(("arbitrary", "arbitrary")),
        name="ssd_chunks",
    )(*args)


SEQ_TILE = SUBLANES
TOK0 = SEQ_TILE - 4
SSD_BS = 8


def _ssd_sample_kernel(z_ref, xbc_ref, dt_ref, h0_ref, cw_ref, cb_ref, dtb_ref, alog_ref, dx_ref, nw_ref,
                       e_ref, sel_ref, y_ref, hnew_ref, yoff_scr, lhs_scr, rhs_scr):
    n_rows = SSD_BS * SEQ_TILE
    half = D_SSM // N_GROUPS
    x8 = xbc_ref[...]
    acc = cb_ref[...] + cw_ref[SSM_CONV - 1:SSM_CONV, :] * x8
    for j in range(1, SSM_CONV):
        acc = acc + cw_ref[SSM_CONV - 1 - j:SSM_CONV - j, :] * pltpu.roll(x8, j, axis=0)
    xc = _silu(acc)
    xs = xc[:, :D_SSM]
    bm = xc[:, D_SSM:D_SSM + N_GROUPS * D_STATE]
    cm = xc[:, D_SSM + N_GROUPS * D_STATE:]

    pos = lax.broadcasted_iota(jnp.int32, (n_rows, LANES), 0) % SEQ_TILE
    lanes = lax.broadcasted_iota(jnp.int32, (n_rows, LANES), 1)
    tok = pos >= TOK0
    dt = jnp.where(tok, _softplus(dt_ref[...] + dtb_ref[...]), 0.0)
    a = dt * (-jnp.exp(alog_ref[...]))
    acs = a
    for d in range(1, 4):
        acs = acs + pltpu.roll(a, d, axis=0)
    tot = jnp.sum(a.reshape(SSD_BS, SEQ_TILE, LANES), axis=1, keepdims=True)
    tot = jnp.broadcast_to(tot, (SSD_BS, SEQ_TILE, LANES)).reshape(n_rows, LANES)
    dend = jnp.where(tok, jnp.exp(tot - acs), 0.0)
    eacs = jnp.where(tok, jnp.exp(acs), 0.0)
    cdec = jnp.exp(tot)

    def expand(v):
        hi, mid, lo = _split3(v)
        stack = jnp.concatenate([hi, mid, lo], axis=0).astype(bf16)
        r = jnp.dot(stack, e_ref[...], preferred_element_type=f32)
        return r[0:n_rows] + r[n_rows:2 * n_rows] + r[2 * n_rows:3 * n_rows]

    heads_per_group = N_HEADS // N_GROUPS
    y = expand(eacs)
    for s in range(SSD_BS):
        for g in range(N_GROUPS):
            c8 = cm[s * SEQ_TILE:(s + 1) * SEQ_TILE, g * D_STATE:(g + 1) * D_STATE].astype(bf16)
            hg = h0_ref[s, g * half:(g + 1) * half, :].astype(bf16)
            yoff_scr[s * SEQ_TILE:(s + 1) * SEQ_TILE, g * half:(g + 1) * half] = lax.dot_general(
                c8, hg, (((1,), (1,)), ((), ())), preferred_element_type=f32)
    y = y * yoff_scr[...] + xs * dx_ref[...]
    for d in range(4):
        b_sh = pltpu.roll(bm, d, axis=0) if d else bm
        prod = cm * b_sh
        cbs = [jnp.sum(prod[:, g * D_STATE:(g + 1) * D_STATE], axis=-1, keepdims=True) for g in range(N_GROUPS)]
        cbh = jnp.where(lanes < heads_per_group, cbs[0], cbs[1])
        if d:
            coef = cbh * jnp.exp(acs - pltpu.roll(acs, d, axis=0)) * pltpu.roll(dt, d, axis=0)
        else:
            coef = cbh * dt
        coef = jnp.where(pos >= TOK0 + d, coef, 0.0)
        y = y + expand(coef) * (pltpu.roll(xs, d, axis=0) if d else xs)

    yn = _gate_norm(y, z_ref[...], nw_ref[...]).astype(bf16)
    y_ref[...] = jnp.dot(sel_ref[...], yn, preferred_element_type=f32).astype(bf16)

    xw = xs * expand(dt * dend)
    hi, mid, lo = _split3(expand(cdec))
    pos_w = lax.broadcasted_iota(jnp.int32, (n_rows, D_SSM), 0) % SEQ_TILE
    lhs_scr[...] = jnp.where(pos_w == 0, hi, jnp.where(pos_w == 1, mid, jnp.where(pos_w == 2, lo, xw)))
    ones = jnp.where(pos < 3, 1.0, 0.0)
    for g in range(N_GROUPS):
        rhs_scr[:, 2 * g * D_STATE:(2 * g + 1) * D_STATE] = jnp.where(tok, bm[:, g * D_STATE:(g + 1) * D_STATE], 0.0)
        rhs_scr[:, (2 * g + 1) * D_STATE:(2 * g + 2) * D_STATE] = ones
    for s in range(SSD_BS):
        for g in range(N_GROUPS):
            l8 = lhs_scr[s * SEQ_TILE:(s + 1) * SEQ_TILE, g * half:(g + 1) * half].astype(bf16)
            r8 = rhs_scr[s * SEQ_TILE:(s + 1) * SEQ_TILE, 2 * g * D_STATE:(2 * g + 2) * D_STATE].astype(bf16)
            res = lax.dot_general(l8, r8, (((0,), (0,)), ((), ())), preferred_element_type=f32)
            hnew_ref[s, g * half:(g + 1) * half, :] = (
                h0_ref[s, g * half:(g + 1) * half, :] * res[:, D_STATE:] + res[:, :D_STATE])


def _ssd_sample(pz8, pxbc8, pdt8, h0, ssm_w, e_mat, sel):
    cw, cb, dtb, alog, dx, nw = ssm_w
    n_seq = h0.shape[0]
    n_rows = SSD_BS * SEQ_TILE
    n_out = SSD_BS * 4
    rows_map = lambda i: (i, 0)
    fixed = lambda i: (0, 0)
    return pl.pallas_call(
        _ssd_sample_kernel,
        grid=(n_seq // SSD_BS,),
        in_specs=[
            pl.BlockSpec((n_rows, D_SSM), rows_map),
            pl.BlockSpec((n_rows, CONV_DIM), rows_map),
            pl.BlockSpec((n_rows, LANES), rows_map),
            pl.BlockSpec((SSD_BS, D_SSM, D_STATE), lambda i: (i, 0, 0)),
            pl.BlockSpec((SSM_CONV, CONV_DIM), fixed),
            pl.BlockSpec((1, CONV_DIM), fixed),
            pl.BlockSpec((1, LANES), fixed),
            pl.BlockSpec((1, LANES), fixed),
            pl.BlockSpec((1, D_SSM), fixed),
            pl.BlockSpec((1, D_SSM), fixed),
            pl.BlockSpec((LANES, D_SSM), fixed),
            pl.BlockSpec((n_out, n_rows), fixed),
        ],
        out_specs=(
            pl.BlockSpec((n_out, D_SSM), rows_map),
            pl.BlockSpec((SSD_BS, D_SSM, D_STATE), lambda i: (i, 0, 0)),
        ),
        out_shape=(
            jax.ShapeDtypeStruct((n_seq * 4, D_SSM), bf16),
            jax.ShapeDtypeStruct((n_seq, D_SSM, D_STATE), f32),
        ),
        scratch_shapes=[
            pltpu.VMEM((n_rows, D_SSM), f32),
            pltpu.VMEM((n_rows, D_SSM), f32),
            pltpu.VMEM((n_rows, 2 * N_GROUPS * D_STATE), f32),
        ],
        compiler_params=_cparams(("arbitrary",)),
        name="ssd_sample",
    )(pz8, pxbc8, pdt8, h0, cw, cb, dtb, alog, dx, nw, e_mat, sel)


N_LT = D_CONF // LANES
HIST = 32
CONF_RC = 64


def _conf_norm_act(acc, g, beta):
    mu = jnp.mean(acc, axis=-1, keepdims=True)
    xc = acc - mu
    var = jnp.mean(xc * xc, axis=-1, keepdims=True)
    return _silu(xc * lax.rsqrt(var + EPS) * g + beta)


def _conf_conv_tile(s_tile, w_ref, p_ref, lt, y_scr, tt):
    off = HIST - (CONF_KERNEL - 1)
    for r0 in range(0, tt, CONF_RC):
        acc = jnp.broadcast_to(p_ref[lt, 0:1, :], (CONF_RC, LANES))
        for k in range(CONF_KERNEL):
            acc = acc + w_ref[lt, k:k + 1, :] * s_tile[r0 + off + k:r0 + off + k + CONF_RC, :]
        y_scr[lt, r0:r0 + CONF_RC, :] = _conf_norm_act(acc, p_ref[lt, 1:2, :], p_ref[lt, 2:3, :])


def _conf_tile_kernel(ca_ref, cg_ref, hist_ref, w_ref, p_ref, y_ref, tail_ref, s_scr, y_scr, *, tt, n_valid):
    t = pl.program_id(1)

    @pl.when(t == 0)
    def _():
        for lt in range(N_LT):
            s_scr[lt, 0:HIST, :] = hist_ref[:, lt * LANES:(lt + 1) * LANES]

    @pl.when(t > 0)
    def _():
        s_scr[:, 0:HIST, :] = s_scr[:, tt:tt + HIST, :]

    glu = ca_ref[...] * _sigmoid(cg_ref[...])
    for lt in range(N_LT):
        s_scr[lt, HIST:HIST + tt, :] = glu[:, lt * LANES:(lt + 1) * LANES]

    def lane_tile(lt, carry):
        _conf_conv_tile(s_scr.at[lt], w_ref, p_ref, lt, y_scr, tt)
        return carry

    lax.fori_loop(0, N_LT, lane_tile, 0)
    for lt in range(N_LT):
        y_ref[:, lt * LANES:(lt + 1) * LANES] = y_scr[lt].astype(bf16)
        tail_ref[:, lt * LANES:(lt + 1) * LANES] = s_scr[lt, n_valid:n_valid + HIST, :]


def _conf_tiles(pc, hist, conf_w, *, n_seq, n_tiles, tt, row_block0, n_valid):
    w3, p3 = conf_w
    in_specs = [
        pl.BlockSpec((tt, D_CONF), lambda b, t: (row_block0 + b * n_tiles + t, 0)),
        pl.BlockSpec((tt, D_CONF), lambda b, t: (row_block0 + b * n_tiles + t, 1)),
        pl.BlockSpec((HIST, D_CONF), lambda b, t: (0, 0)),
        pl.BlockSpec((N_LT, HIST, LANES), lambda b, t: (0, 0, 0)),
        pl.BlockSpec((N_LT, SUBLANES, LANES), lambda b, t: (0, 0, 0)),
    ]
    return pl.pallas_call(
        functools.partial(_conf_tile_kernel, tt=tt, n_valid=n_valid),
        grid=(n_seq, n_tiles),
        in_specs=in_specs,
        out_specs=(
            pl.BlockSpec((tt, D_CONF), lambda b, t: (b * n_tiles + t, 0)),
            pl.BlockSpec((None, HIST, D_CONF), lambda b, t: (b, 0, 0)),
        ),
        out_shape=(
            jax.ShapeDtypeStruct((n_seq * n_tiles * tt, D_CONF), bf16),
            jax.ShapeDtypeStruct((n_seq, HIST, D_CONF), f32),
        ),
        scratch_shapes=[
            pltpu.VMEM((N_LT, HIST + tt, LANES), f32),
            pltpu.VMEM((N_LT, tt, LANES), f32),
        ],
        compiler_params=_cparams(("arbitrary", "arbitrary")),
        name="conf_tiles",
    )(pc, pc, hist, w3, p3)


CONF_BS = 16


def _conf_sample_kernel(ca_ref, cg_ref, buf_ref, w_ref, p_ref, y_ref, newbuf_ref):
    n_hist = CONF_KERNEL - 1
    n_tok = ca_ref.shape[0]
    glu = [ca_ref[t] * _sigmoid(cg_ref[t]) for t in range(n_tok)]
    win = lambda j: buf_ref[j] if j < n_hist else glu[j - n_hist]
    for j in range(n_hist):
        newbuf_ref[j] = win(j + n_tok)
    for t in range(n_tok):
        acc = jnp.broadcast_to(p_ref[0:1, :], (CONF_BS, D_CONF))
        for k in range(CONF_KERNEL):
            acc = acc + w_ref[k:k + 1, :] * win(t + k)
        for lt in range(N_LT):
            sl = slice(lt * LANES, (lt + 1) * LANES)
            y_ref[t, :, sl] = _conf_norm_act(acc[:, sl], p_ref[1:2, sl], p_ref[2:3, sl]).astype(bf16)


def _conf_sample(pc_t, buf_t, w2, p2):
    n_tok, n_seq = pc_t.shape[:2]
    n_hist = CONF_KERNEL - 1
    seq_blocks = lambda col: (lambda i: (0, i, col))
    return pl.pallas_call(
        _conf_sample_kernel,
        grid=(n_seq // CONF_BS,),
        in_specs=[
            pl.BlockSpec((n_tok, CONF_BS, D_CONF), seq_blocks(0)),
            pl.BlockSpec((n_tok, CONF_BS, D_CONF), seq_blocks(1)),
            pl.BlockSpec((n_hist, CONF_BS, D_CONF), seq_blocks(0)),
            pl.BlockSpec((CONF_KERNEL, D_CONF), lambda i: (0, 0)),
            pl.BlockSpec((3, D_CONF), lambda i: (0, 0)),
        ],
        out_specs=(
            pl.BlockSpec((n_tok, CONF_BS, D_CONF), seq_blocks(0)),
            pl.BlockSpec((n_hist, CONF_BS, D_CONF), seq_blocks(0)),
        ),
        out_shape=(
            jax.ShapeDtypeStruct((n_tok, n_seq, D_CONF), bf16),
            jax.ShapeDtypeStruct((n_hist, n_seq, D_CONF), f32),
        ),
        compiler_params=_cparams(("arbitrary",)),
        name="conf_sample",
    )(pc_t, pc_t, buf_t, w2, p2)


def _seq_tiles(rows, n_seq, n_tok):
    w = rows.shape[-1]
    return jnp.pad(rows.reshape(n_seq, n_tok, w), ((0, 0), (SEQ_TILE - n_tok, 0), (0, 0))).reshape(n_seq * SEQ_TILE, w)


def kernel(x_prompt, x_sample, state_ssm, state_ssm_conv, state_conf_conv, meta_tokens, ffn1_norm, ffn1_w_gate, ffn1_w_up, ffn1_w_down, mix_norm, w_in, ssm_conv_w, ssm_conv_b, ssm_dt_bias, ssm_A_log, ssm_D, ssm_norm, conf_conv_w, conf_conv_b, conf_ln_g, conf_ln_b, w_out, ffn2_norm, ffn2_w_gate, ffn2_w_up, ffn2_w_down, final_norm):
    n_pb, n_pt = x_prompt.shape[0], x_prompt.shape[1]
    n_sb, n_st = x_sample.shape[0], x_sample.shape[1]
    assert (n_pb * n_pt, n_sb * n_st, n_st) == (N_PROMPT_ROWS, N_SAMPLE_ROWS, 4) and n_pt % CHUNK == 0

    row = lambda v: v.reshape(1, -1)
    x_rows = (x_prompt.reshape(N_PROMPT_ROWS, D_MODEL), x_sample.reshape(N_SAMPLE_ROWS, D_MODEL), meta_tokens)
    h, u, w_t = _ffn(x_rows, row(ffn1_norm[0]), _to_bf16(ffn1_w_gate, 256), _to_bf16(ffn1_w_up, 256),
                     _to_bf16(ffn1_w_down, 704), row(mix_norm[0]), side=(jnp.transpose(w_in[0]), 96))

    c_in0 = D_SSM + CONV_DIM
    pzx, wg2 = _matmul_nt(u, w_t, row0=0, n=c_in0, tn=1536, side=(ffn2_w_gate, 64))
    pdt = _matmul_nt(u, w_t, row0=c_in0, n=LANES, tn=LANES)
    pc, wu2 = _matmul_nt(u, w_t, row0=c_in0 + N_HEADS, n=2 * D_CONF, tn=1024, side=(ffn2_w_up, 64))

    pad_heads = lambda v: jnp.pad(v.reshape(1, N_HEADS), ((0, 0), (0, LANES - N_HEADS)))
    ssm_w = (ssm_conv_w[0], row(ssm_conv_b[0]), pad_heads(ssm_dt_bias[0]), pad_heads(ssm_A_log[0]),
             row(jnp.repeat(ssm_D[0], HEAD_DIM)), row(ssm_norm[0]))
    tri = jnp.tril(jnp.ones((CHUNK, CHUNK), f32))
    e_mat = (jnp.arange(LANES)[:, None] == (jnp.arange(D_SSM)[None, :] // HEAD_DIM)).astype(bf16)
    lane_major = lambda v: v.reshape(v.shape[0], N_LT, LANES).transpose(1, 0, 2)
    conf_w = (lane_major(jnp.pad(conf_conv_w[0], ((0, HIST - CONF_KERNEL), (0, 0)))),
              lane_major(jnp.pad(jnp.stack([conf_conv_b[0], conf_ln_g[0], conf_ln_b[0]]), ((0, SUBLANES - 3), (0, 0)))))
    out_rows = jnp.arange(SSD_BS * 4)
    sel = ((out_rows // 4) * SEQ_TILE + TOK0 + out_rows % 4)[:, None] == jnp.arange(SSD_BS * SEQ_TILE)[None, :]
    sel = sel.astype(bf16)

    meta_block = ROW_META // CHUNK
    ys_m, h_meta, tail_meta = _ssd_chunks(
        pzx, pdt, jnp.zeros((SUBLANES, CONV_DIM), f32), jnp.zeros((D_STATE, D_SSM), f32), ssm_w, (tri, e_mat),
        n_seq=1, n_chunks=1, row_block0=meta_block, n_valid=N_META, transpose_out=False)
    yc_m, hist_meta = _conf_tiles(pc, jnp.zeros((HIST, D_CONF), f32), conf_w,
                                  n_seq=1, n_tiles=1, tt=CHUNK, row_block0=meta_block, n_valid=N_META)

    ys_p, p_ssm, p_tail, wo = _ssd_chunks(
        pzx, pdt, tail_meta[0], h_meta[0], ssm_w, (tri, e_mat),
        n_seq=n_pb, n_chunks=n_pt // CHUNK, row_block0=0, n_valid=CHUNK, transpose_out=True, side=(w_out, 64))
    conf_tt = 256
    yc_p, p_hist = _conf_tiles(pc, hist_meta[0], conf_w,
                               n_seq=n_pb, n_tiles=n_pt // conf_tt, tt=conf_tt, row_block0=0, n_valid=conf_tt)

    smp = slice(ROW_SAMPLE, ROW_SAMPLE + N_SAMPLE_ROWS)
    pzx_s = pzx[smp]
    pxbc_s = pzx_s[:, D_SSM:].reshape(n_sb, n_st, CONV_DIM)
    pxbc8 = jnp.concatenate([jnp.zeros((n_sb, 1, CONV_DIM), f32), state_ssm_conv[0], pxbc_s], axis=1)
    ys_s, s_ssm = _ssd_sample(_seq_tiles(pzx_s[:, :D_SSM], n_sb, n_st), pxbc8.reshape(n_sb * SEQ_TILE, CONV_DIM),
                              _seq_tiles(pdt[smp], n_sb, n_st), state_ssm[0].reshape(n_sb, D_SSM, D_STATE),
                              ssm_w, e_mat, sel)
    to_time_major = lambda v: jnp.transpose(v, (1, 0, 2))
    yc_t, cconv_t = _conf_sample(to_time_major(pc[smp].reshape(n_sb, n_st, 2 * D_CONF)), to_time_major(state_conf_conv[0]),
                                 conf_conv_w[0], jnp.stack([conf_conv_b[0], conf_ln_g[0], conf_ln_b[0]]))
    yc_s = to_time_major(yc_t).reshape(N_SAMPLE_ROWS, D_CONF)
    s_cconv = to_time_major(cconv_t)

    h2, wd2 = _outproj(h, (ys_p, ys_s, ys_m), (yc_p, yc_s, yc_m), wo, side=(ffn2_w_down, 256))
    y_prompt, y_sample = _ffn((h2,), row(ffn2_norm[0]), wg2, wu2, wd2, row(final_norm),
                              dst_rows=(N_PROMPT_ROWS, N_SAMPLE_ROWS))
    n_hist = CONF_KERNEL - 1
    return (
        y_prompt.reshape(n_pb, n_pt, D_MODEL), y_sample.reshape(n_sb, n_st, D_MODEL),
        p_ssm.reshape(1, n_pb, N_HEADS, HEAD_DIM, D_STATE),
        p_tail[:, SUBLANES - (SSM_CONV - 1):, :][None],
        p_hist[:, HIST - n_hist:, :][None],
        s_ssm.reshape(1, n_sb, N_HEADS, HEAD_DIM, D_STATE),
        pxbc_s[:, n_st - (SSM_CONV - 1):, :][None],
        s_cconv[None],
    )
```

```python
import functools

import jax
import jax.numpy as jnp
from jax import lax
from jax.experimental import pallas as pl
from jax.experimental.pallas import tpu as pltpu

f32 = jnp.float32
bf16 = jnp.bfloat16

D_MODEL = 2048
N_META = 16
D_SSM = 2048
D_CONF = 2048
HEAD_DIM = 64
N_HEADS = 32
N_GROUPS = 2
D_STATE = 128
SSM_CONV = 4
CONV_DIM = D_SSM + 2 * N_GROUPS * D_STATE
CONF_KERNEL = 31
CONF_GROUP = 128
D_FF = 5632
EPS = 1e-5
LOG2E = 1.4426950408889634

LANES = 128
SUBLANES = 8
CHUNK = 128
VMEM_LIMIT = 56 * 1024 * 1024

N_PROMPT_ROWS = 4 * 2048
N_SAMPLE_ROWS = 128 * 4
ROW_SAMPLE = N_PROMPT_ROWS
ROW_META = N_PROMPT_ROWS + N_SAMPLE_ROWS
N_ROWS = ROW_META + CHUNK
TM = 736
TF = 256


def _cparams(sem):
    return pltpu.CompilerParams(dimension_semantics=sem, vmem_limit_bytes=VMEM_LIMIT)


def _sigmoid(x):
    return 1.0 / (1.0 + jnp.exp(-x))


def _silu(x):
    return x * _sigmoid(x)


def _rms(x, g):
    return x * lax.rsqrt(jnp.mean(x * x, axis=-1, keepdims=True) + EPS) * g


def _softplus(x):
    return jnp.maximum(x, 0.0) + jnp.log1p(jnp.exp(-jnp.abs(x)))


def _split3(v):
    hi = v.astype(bf16).astype(f32)
    r1 = v - hi
    mid = r1.astype(bf16).astype(f32)
    return hi, mid, r1 - mid


def _row_segments(src_rows, n_total, tm):
    bounds, start = [], 0
    for k, n in enumerate(src_rows):
        bounds.append((k, start, n))
        start += n
    if start < n_total:
        bounds.append((None, start, n_total - start))
    table = []
    for t in range(n_total // tm):
        lo, hi = t * tm, (t + 1) * tm
        segs = []
        for k, s, n in bounds:
            a, b = max(lo, s), min(hi, s + n)
            if a < b:
                segs.append((k, a - s, a - lo, b - a))
        table.append(segs)
    return table


def _load_tile(i, seg_table, groups, sem):
    tm = groups[0][2].shape[0]
    special = [t for t, segs in enumerate(seg_table) if segs != [(0, t * tm, 0, tm)]]
    first = special[0] if special else len(seg_table)
    assert special == list(range(first, len(seg_table)))

    @pl.when(i < first)
    def _():
        for main_ref, _, dst_ref in groups:
            dst_ref[...] = main_ref[...]

    for t in special:
        @pl.when(i == t)
        def _(segs=seg_table[t]):
            copies = []
            for main_ref, tail_srcs, dst_ref in groups:
                for k, r0, d0, n in segs:
                    if k is None:
                        dst_ref[d0:d0 + n, :] = jnp.zeros((n, dst_ref.shape[1]), dst_ref.dtype)
                    elif k == 0:
                        dst_ref[d0:d0 + n, :] = main_ref[d0:d0 + n, :]
                    else:
                        cp = pltpu.make_async_copy(tail_srcs[k - 1].at[pl.ds(r0, n)], dst_ref.at[pl.ds(d0, n)], sem)
                        cp.start()
                        copies.append(cp)
            for cp in copies:
                cp.wait()


def _side_cast(w, block_rows, grid):
    r, c = w.shape[-2:]
    n_blocks = r // block_rows
    assert r % block_rows == 0 and n_blocks <= grid[0] * grid[1]
    blk = lambda i, j: jnp.minimum(i * grid[1] + j, n_blocks - 1)
    if w.ndim == 3:
        in_spec = pl.BlockSpec((None, block_rows, c), lambda i, j: (0, blk(i, j), 0))
    else:
        in_spec = pl.BlockSpec((block_rows, c), lambda i, j: (blk(i, j), 0))
    return in_spec, pl.BlockSpec((block_rows, c), lambda i, j: (blk(i, j), 0)), jax.ShapeDtypeStruct((r, c), bf16)


def _run_side_cast(side_refs):
    for src_ref, dst_ref in side_refs:
        dst_ref[...] = src_ref[...].astype(bf16)


def _drain_rows(i, out_table, dsts, acc_scr, sems):
    def copies(t):
        return [pltpu.make_async_copy(acc_scr.at[t % 2, pl.ds(s0, n)], dsts[k].at[pl.ds(r0, n)], sems.at[t % 2])
                for k, r0, s0, n in out_table[t] if k is not None]

    n_tiles = len(out_table)
    for t in range(n_tiles):
        @pl.when(i == t)
        def _(t=t):
            if t > 0:
                for cp in copies(t - 1):
                    cp.wait()
            for cp in copies(t):
                cp.start()
            if t == n_tiles - 1:
                for cp in copies(t):
                    cp.wait()


def _ffn_kernel(*refs, seg_table, out_table, n_src, n_side):
    x_ref, tail_srcs = refs[0], refs[1:n_src]
    nw_ref, wg_ref, wu_ref, wd_ref, pw_ref = refs[n_src:n_src + 5]
    pos = n_src + 5
    side_in, pos = refs[pos:pos + n_side], pos + n_side
    i = pl.program_id(0)
    f = pl.program_id(1)
    if out_table is None:
        (acc, yn_ref), pos = refs[pos:pos + 2], pos + 2
        side_out, pos = refs[pos:pos + n_side], pos + n_side
        xn_ref, sem = refs[pos:]
    else:
        n_dst = 1 + max(k for segs in out_table for k, _, _, _ in segs if k is not None)
        dsts, pos = refs[pos:pos + n_dst], pos + n_dst
        side_out, pos = refs[pos:pos + n_side], pos + n_side
        acc_scr, xn_ref, sem, out_sems = refs[pos:]
        acc = acc_scr.at[i % 2]
    _run_side_cast(zip(side_in, side_out))

    @pl.when(f == 0)
    def _():
        _load_tile(i, seg_table, [(x_ref, tail_srcs, acc)], sem)
        xn_ref[...] = _rms(acc[...], nw_ref[...]).astype(bf16)

    xn = xn_ref[...]
    g = jnp.dot(xn, wg_ref[...], preferred_element_type=f32)
    u = jnp.dot(xn, wu_ref[...], preferred_element_type=f32)
    a = (_silu(g) * u).astype(bf16)
    acc[...] += 0.5 * jnp.dot(a, wd_ref[...], preferred_element_type=f32)

    @pl.when(f == pl.num_programs(1) - 1)
    def _():
        if out_table is None:
            yn_ref[...] = _rms(acc[...], pw_ref[...]).astype(yn_ref.dtype)
        else:
            acc[...] = _rms(acc[...], pw_ref[...])
            _drain_rows(i, out_table, dsts, acc_scr, out_sems)


def _ffn(srcs, norm_w, wg, wu, wd, post_w, *, dst_rows=None, side=None, tf=TF):
    seg_table = _row_segments([s.shape[0] for s in srcs], N_ROWS, TM)
    grid = (N_ROWS // TM, D_FF // tf)
    side_specs = [_side_cast(*side, grid)] if side else []
    row_spec = pl.BlockSpec((TM, D_MODEL), lambda i, f: (i, 0))
    vec_spec = pl.BlockSpec((1, D_MODEL), lambda i, f: (0, 0))
    scratch = [pltpu.VMEM((TM, D_MODEL), bf16), pltpu.SemaphoreType.DMA(())]
    if dst_rows is None:
        out_table = None
        out_shape = (jax.ShapeDtypeStruct((N_ROWS, D_MODEL), f32), jax.ShapeDtypeStruct((N_ROWS, D_MODEL), bf16))
        out_specs = (row_spec, row_spec)
    else:
        out_table = _row_segments(dst_rows, N_ROWS, TM)
        out_shape = tuple(jax.ShapeDtypeStruct((n, D_MODEL), f32) for n in dst_rows)
        out_specs = tuple(pl.BlockSpec(memory_space=pl.ANY) for _ in dst_rows)
        scratch = [pltpu.VMEM((2, TM, D_MODEL), f32)] + scratch + [pltpu.SemaphoreType.DMA((2,))]
    return pl.pallas_call(
        functools.partial(_ffn_kernel, seg_table=seg_table, out_table=out_table, n_src=len(srcs),
                          n_side=len(side_specs)),
        grid=grid,
        in_specs=[row_spec] + [pl.BlockSpec(memory_space=pl.ANY)] * (len(srcs) - 1) + [
            vec_spec,
            pl.BlockSpec((D_MODEL, tf), lambda i, f: (0, f)),
            pl.BlockSpec((D_MODEL, tf), lambda i, f: (0, f)),
            pl.BlockSpec((tf, D_MODEL), lambda i, f: (f, 0)),
            vec_spec,
        ] + [s[0] for s in side_specs],
        out_specs=tuple(out_specs) + tuple(s[1] for s in side_specs),
        out_shape=tuple(out_shape) + tuple(s[2] for s in side_specs),
        scratch_shapes=scratch,
        compiler_params=_cparams(("arbitrary", "arbitrary")),
        name="ffn",
    )(*srcs, norm_w, wg, wu, wd, post_w, *([side[0]] if side else []))


def _cast_kernel(x_ref, o_ref):
    o_ref[...] = x_ref[...].astype(bf16)


def _to_bf16(w, block_rows):
    r, c = w.shape[-2:]
    if w.ndim == 3:
        in_spec = pl.BlockSpec((None, block_rows, c), lambda i: (0, i, 0))
    else:
        in_spec = pl.BlockSpec((block_rows, c), lambda i: (i, 0))
    return pl.pallas_call(
        _cast_kernel,
        grid=(r // block_rows,),
        in_specs=[in_spec],
        out_specs=pl.BlockSpec((block_rows, c), lambda i: (i, 0)),
        out_shape=jax.ShapeDtypeStruct((r, c), bf16),
        compiler_params=_cparams(("arbitrary",)),
        name="cast_bf16",
    )(w)


def _matmul_nt_kernel(a_ref, w_ref, *rest):
    n_side = len(rest) // 2
    o_ref = rest[n_side]
    _run_side_cast(zip(rest[:n_side], rest[n_side + 1:]))
    o_ref[...] = lax.dot_general(a_ref[...], w_ref[...], (((1,), (1,)), ((), ())), preferred_element_type=f32)


def _matmul_nt(a, w_t, *, row0, n, tn, tm=TM, side=None):
    n_rows, k = a.shape
    align = 2 * 16
    assert row0 % align == 0 and tn % align == 0 and n % tn == 0 and n_rows % tm == 0
    grid = (n_rows // tm, n // tn)
    side_specs = [_side_cast(*side, grid)] if side else []
    out = pl.pallas_call(
        _matmul_nt_kernel,
        grid=grid,
        in_specs=[
            pl.BlockSpec((tm, k), lambda i, j: (i, 0)),
            pl.BlockSpec((pl.Element(tn), pl.Element(k)), lambda i, j: (pl.multiple_of(row0 + tn * j, align), 0)),
        ] + [s[0] for s in side_specs],
        out_specs=(pl.BlockSpec((tm, tn), lambda i, j: (i, j)),) + tuple(s[1] for s in side_specs),
        out_shape=(jax.ShapeDtypeStruct((n_rows, n), f32),) + tuple(s[2] for s in side_specs),
        compiler_params=_cparams(("arbitrary", "arbitrary")),
        name="in_proj",
    )(a, w_t, *([side[0]] if side else []))
    return out if side else out[0]


def _outproj_kernel(*refs, seg_table, n_src, n_side):
    ys_ref, ys_tails = refs[0], refs[1:n_src]
    yc_ref, yc_tails = refs[n_src], refs[n_src + 1:2 * n_src]
    h_ref, w0_ref, w1_ref = refs[2 * n_src:2 * n_src + 3]
    pos = 2 * n_src + 3
    side_in, o_ref, side_out = refs[pos:pos + n_side], refs[pos + n_side], refs[pos + n_side + 1:pos + 2 * n_side + 1]
    ys_scr, yc_scr, sem = refs[pos + 2 * n_side + 1:]
    _run_side_cast(zip(side_in, side_out))

    @pl.when(pl.program_id(1) == 0)
    def _():
        _load_tile(pl.program_id(0), seg_table, [(ys_ref, ys_tails, ys_scr), (yc_ref, yc_tails, yc_scr)], sem)

    acc = jnp.dot(ys_scr[...], w0_ref[...], preferred_element_type=f32)
    acc = acc + jnp.dot(yc_scr[...], w1_ref[...], preferred_element_type=f32)
    o_ref[...] = h_ref[...] + acc


def _outproj(h, ys_srcs, yc_srcs, w_out, tn=1024, side=None):
    n_rows = h.shape[0]
    seg_table = _row_segments([s.shape[0] for s in ys_srcs], n_rows, TM)
    n_src = len(ys_srcs)
    grid = (n_rows // TM, D_MODEL // tn)
    side_specs = [_side_cast(*side, grid)] if side else []
    y_specs = [pl.BlockSpec((TM, D_SSM), lambda i, j: (i, 0))] + [pl.BlockSpec(memory_space=pl.ANY)] * (n_src - 1)
    out = pl.pallas_call(
        functools.partial(_outproj_kernel, seg_table=seg_table, n_src=n_src, n_side=len(side_specs)),
        grid=grid,
        in_specs=y_specs + y_specs + [
            pl.BlockSpec((TM, tn), lambda i, j: (i, j)),
            pl.BlockSpec((D_SSM, tn), lambda i, j: (0, j)),
            pl.BlockSpec((D_CONF, tn), lambda i, j: (1, j)),
        ] + [s[0] for s in side_specs],
        out_specs=(pl.BlockSpec((TM, tn), lambda i, j: (i, j)),) + tuple(s[1] for s in side_specs),
        out_shape=(jax.ShapeDtypeStruct((n_rows, D_MODEL), f32),) + tuple(s[2] for s in side_specs),
        scratch_shapes=[pltpu.VMEM((TM, D_SSM), bf16), pltpu.VMEM((TM, D_CONF), bf16), pltpu.SemaphoreType.DMA(())],
        compiler_params=_cparams(("arbitrary", "arbitrary")),
        name="out_proj",
    )(*ys_srcs, *yc_srcs, h, w_out, w_out, *([side[0]] if side else []))
    return out if side else out[0]


def _gate_norm(y, z, nw):
    y = y * _silu(z)
    half = D_SSM // N_GROUPS
    parts = []
    for g in range(N_GROUPS):
        yg = y[:, g * half:(g + 1) * half]
        ms = jnp.mean(yg * yg, axis=-1, keepdims=True)
        parts.append(yg * lax.rsqrt(ms + EPS) * nw[:, g * half:(g + 1) * half])
    return jnp.concatenate(parts, axis=1)


def _ssd_chunk_math(z, xbc, dt_raw, cinit_ref, hinit_ref, cw_ref, cb_ref, dtb_ref, alog_ref,
                    dx_ref, nw_ref, tri_ref, e_ref, y_ref, hout_ref, tail_ref, win_scr, ht_scr, y_scr,
                    *, n_valid, transpose_out):
    c = pl.program_id(1)

    @pl.when(c == 0)
    def _():
        ht_scr[...] = hinit_ref[...]
        win_scr[0:SUBLANES, :] = cinit_ref[...]

    win_scr[SUBLANES:SUBLANES + CHUNK, :] = xbc
    acc = cb_ref[...] + cw_ref[SSM_CONV - 1:SSM_CONV, :] * xbc
    for j in range(1, SSM_CONV):
        acc = acc + cw_ref[SSM_CONV - 1 - j:SSM_CONV - j, :] * win_scr[SUBLANES - j:SUBLANES - j + CHUNK, :]
    tail_ref[...] = win_scr[n_valid:n_valid + SUBLANES, :]
    win_scr[0:SUBLANES, :] = win_scr[CHUNK:CHUNK + SUBLANES, :]
    xc = _silu(acc)
    b_mats = [xc[:, D_SSM + g * D_STATE:D_SSM + (g + 1) * D_STATE] for g in range(N_GROUPS)]
    c_mats = [xc[:, D_SSM + (N_GROUPS + g) * D_STATE:D_SSM + (N_GROUPS + g + 1) * D_STATE] for g in range(N_GROUPS)]

    rows = lax.broadcasted_iota(jnp.int32, (CHUNK, LANES), 0)
    lanes = lax.broadcasted_iota(jnp.int32, (CHUNK, LANES), 1)
    dt = _softplus(dt_raw + dtb_ref[...])
    if n_valid < CHUNK:
        dt = jnp.where(rows < n_valid, dt, 0.0)
    a = dt * (-jnp.exp(alog_ref[...]))
    acs = jnp.dot(tri_ref[...], a, precision=lax.Precision.HIGHEST, preferred_element_type=f32) * LOG2E
    total = acs[CHUNK - 1:CHUNK, :]
    dend = jnp.exp2(total - acs)
    acs_t = acs.T
    dt_t = dt.T
    w_t = (dt * dend).T
    hi, mid, lo = _split3(jnp.exp2(total))
    rid = lax.broadcasted_iota(jnp.int32, (SUBLANES, LANES), 0)
    cd3 = jnp.where(rid == 0, hi, jnp.where(rid == 1, mid, jnp.where(rid == 2, lo, 0.0)))
    cdx = jnp.sum(jnp.dot(cd3.astype(bf16), e_ref[...], preferred_element_type=f32), axis=0, keepdims=True)

    cb_mats, bt_mats = [], []
    for g in range(N_GROUPS):
        cb_mats.append(lax.dot_general(c_mats[g].astype(bf16), b_mats[g].astype(bf16), (((1,), (1,)), ((), ())),
                                       preferred_element_type=f32))
        bt_mats.append(b_mats[g].T)

    lane_lo = lanes < HEAD_DIM
    tril = rows >= lanes
    heads_per_group = N_HEADS // N_GROUPS
    for j in range(N_HEADS // 2):
        sl = slice(j * LANES, (j + 1) * LANES)
        xs_pair = xc[:, sl]
        ht_pair = ht_scr[:, sl]
        y_pair = jnp.zeros((CHUNK, LANES), f32)
        upd = jnp.zeros((D_STATE, LANES), f32)
        for half in range(2):
            h = 2 * j + half
            g = h // heads_per_group
            keep = lane_lo if half == 0 else jnp.logical_not(lane_lo)
            xs_m = jnp.where(keep, xs_pair, 0.0).astype(bf16)
            ht_m = jnp.where(keep, ht_pair, 0.0).astype(bf16)
            acs_col = jnp.broadcast_to(acs[:, h:h + 1], (CHUNK, CHUNK))
            acs_row = jnp.broadcast_to(acs_t[h:h + 1, :], (CHUNK, CHUNK))
            lmat = jnp.exp2(jnp.where(tril, acs_col - acs_row, -jnp.inf))
            m_h = cb_mats[g] * lmat * dt_t[h:h + 1, :]
            ce_h = c_mats[g] * jnp.exp2(acs_col)
            lhs = jnp.concatenate([m_h, ce_h], axis=1).astype(bf16)
            rhs = jnp.concatenate([xs_m, ht_m], axis=0)
            y_pair = y_pair + jnp.dot(lhs, rhs, preferred_element_type=f32)
            bw = (bt_mats[g] * w_t[h:h + 1, :]).astype(bf16)
            upd = upd + jnp.dot(bw, xs_m, preferred_element_type=f32)
        y_scr[:, sl] = y_pair
        ht_scr[:, sl] = ht_pair * cdx[:, sl] + upd

    xs = xc[:, :D_SSM]
    y = y_scr[...] + xs * dx_ref[...]
    y_ref[...] = _gate_norm(y, z(), nw_ref[...]).astype(bf16)

    @pl.when(c == pl.num_programs(1) - 1)
    def _():
        if transpose_out:
            hout_ref[...] = ht_scr[...].T
        else:
            hout_ref[...] = ht_scr[...]


N_SSD_PARAMS = 10


def _ssd_chunk_kernel(z_ref, xs_ref, bc_ref, dt_ref, *rest, n_valid, transpose_out, n_side):
    params, rest = rest[:N_SSD_PARAMS], rest[N_SSD_PARAMS:]
    side_in, outs = rest[:n_side], rest[n_side:n_side + 3]
    side_out, scratch = rest[n_side + 3:2 * n_side + 3], rest[2 * n_side + 3:]
    _run_side_cast(zip(side_in, side_out))
    xbc = jnp.concatenate([xs_ref[...], bc_ref[...]], axis=1)
    _ssd_chunk_math(lambda: z_ref[...], xbc, dt_ref[...], *params, *outs, *scratch,
                    n_valid=n_valid, transpose_out=transpose_out)


def _ssd_chunks(pzx, pdt, cinit, hinit, ssm_w, consts, *, n_seq, n_chunks, row_block0, n_valid, transpose_out,
                side=None):
    cw, cb, dtb, alog, dx, nw = ssm_w
    tri, e_mat = consts
    bc_dim = CONV_DIM - D_SSM
    side_specs = [_side_cast(*side, (n_seq, n_chunks))] if side else []
    rows_map = lambda b, c: (row_block0 + b * n_chunks + c, 0)
    fixed = lambda b, c: (0, 0)
    in_specs = [
        pl.BlockSpec((CHUNK, D_SSM), rows_map),
        pl.BlockSpec((CHUNK, D_SSM), lambda b, c: (row_block0 + b * n_chunks + c, 1)),
        pl.BlockSpec((CHUNK, bc_dim), lambda b, c: (row_block0 + b * n_chunks + c, 2 * D_SSM // bc_dim)),
        pl.BlockSpec((CHUNK, LANES), rows_map),
        pl.BlockSpec((SUBLANES, CONV_DIM), fixed),
        pl.BlockSpec((D_STATE, D_SSM), fixed),
        pl.BlockSpec((SSM_CONV, CONV_DIM), fixed),
        pl.BlockSpec((1, CONV_DIM), fixed),
        pl.BlockSpec((1, LANES), fixed),
        pl.BlockSpec((1, LANES), fixed),
        pl.BlockSpec((1, D_SSM), fixed),
        pl.BlockSpec((1, D_SSM), fixed),
        pl.BlockSpec((CHUNK, CHUNK), fixed),
        pl.BlockSpec((LANES, D_SSM), fixed),
    ]
    args = [pzx, pzx, pzx, pdt, cinit, hinit, cw, cb, dtb, alog, dx, nw, tri, e_mat] + ([side[0]] if side else [])
    hshape = (n_seq, D_SSM, D_STATE) if transpose_out else (n_seq, D_STATE, D_SSM)
    return pl.pallas_call(
        functools.partial(_ssd_chunk_kernel, n_valid=n_valid, transpose_out=transpose_out, n_side=len(side_specs)),
        grid=(n_seq, n_chunks),
        in_specs=in_specs + [s[0] for s in side_specs],
        out_specs=(
            pl.BlockSpec((CHUNK, D_SSM), lambda b, c: (b * n_chunks + c, 0)),
            pl.BlockSpec((None,) + hshape[1:], lambda b, c: (b, 0, 0)),
            pl.BlockSpec((None, SUBLANES, CONV_DIM), lambda b, c: (b, 0, 0)),
        ) + tuple(s[1] for s in side_specs),
        out_shape=(
            jax.ShapeDtypeStruct((n_seq * n_chunks * CHUNK, D_SSM), bf16),
            jax.ShapeDtypeStruct(hshape, f32),
            jax.ShapeDtypeStruct((n_seq, SUBLANES, CONV_DIM), f32),
        ) + tuple(s[2] for s in side_specs),
        scratch_shapes=[
            pltpu.VMEM((SUBLANES + CHUNK, CONV_DIM), f32),
            pltpu.VMEM((D_STATE, D_SSM), f32),
            pltpu.VMEM((CHUNK, D_SSM), f32),
        ],
        compiler_params=_cparams(("arbitrary", "arbitrary")),
        name="ssd_chunks",
    )(*args)


SEQ_TILE = SUBLANES
TOK0 = SEQ_TILE - 4
SSD_BS = 8


def _ssd_sample_kernel(z_ref, xbc_ref, dt_ref, h0_ref, cw_ref, cb_ref, dtb_ref, alog_ref, dx_ref, nw_ref,
                       e_ref, sel_ref, y_ref, hnew_ref, yoff_scr, lhs_scr, rhs_scr):
    n_rows = SSD_BS * SEQ_TILE
    half = D_SSM // N_GROUPS
    x8 = xbc_ref[...]
    acc = cb_ref[...] + cw_ref[SSM_CONV - 1:SSM_CONV, :] * x8
    for j in range(1, SSM_CONV):
        acc = acc + cw_ref[SSM_CONV - 1 - j:SSM_CONV - j, :] * pltpu.roll(x8, j, axis=0)
    xc = _silu(acc)
    xs = xc[:, :D_SSM]
    bm = xc[:, D_SSM:D_SSM + N_GROUPS * D_STATE]
    cm = xc[:, D_SSM + N_GROUPS * D_STATE:]

    pos = lax.broadcasted_iota(jnp.int32, (n_rows, LANES), 0) % SEQ_TILE
    lanes = lax.broadcasted_iota(jnp.int32, (n_rows, LANES), 1)
    tok = pos >= TOK0
    dt = jnp.where(tok, _softplus(dt_ref[...] + dtb_ref[...]), 0.0)
    a = dt * (-jnp.exp(alog_ref[...]))
    acs = a
    for d in range(1, 4):
        acs = acs + pltpu.roll(a, d, axis=0)
    tot = jnp.sum(a.reshape(SSD_BS, SEQ_TILE, LANES), axis=1, keepdims=True)
    tot = jnp.broadcast_to(tot, (SSD_BS, SEQ_TILE, LANES)).reshape(n_rows, LANES)
    dend = jnp.where(tok, jnp.exp(tot - acs), 0.0)
    eacs = jnp.where(tok, jnp.exp(acs), 0.0)
    cdec = jnp.exp(tot)

    def expand(v):
        hi, mid, lo = _split3(v)
        stack = jnp.concatenate([hi, mid, lo], axis=0).astype(bf16)
        r = jnp.dot(stack, e_ref[...], preferred_element_type=f32)
        return r[0:n_rows] + r[n_rows:2 * n_rows] + r[2 * n_rows:3 * n_rows]

    heads_per_group = N_HEADS // N_GROUPS
    y = expand(eacs)
    for s in range(SSD_BS):
        for g in range(N_GROUPS):
            c8 = cm[s * SEQ_TILE:(s + 1) * SEQ_TILE, g * D_STATE:(g + 1) * D_STATE].astype(bf16)
            hg = h0_ref[s, g * half:(g + 1) * half, :].astype(bf16)
            yoff_scr[s * SEQ_TILE:(s + 1) * SEQ_TILE, g * half:(g + 1) * half] = lax.dot_general(
                c8, hg, (((1,), (1,)), ((), ())), preferred_element_type=f32)
    y = y * yoff_scr[...] + xs * dx_ref[...]
    for d in range(4):
        b_sh = pltpu.roll(bm, d, axis=0) if d else bm
        prod = cm * b_sh
        cbs = [jnp.sum(prod[:, g * D_STATE:(g + 1) * D_STATE], axis=-1, keepdims=True) for g in range(N_GROUPS)]
        cbh = jnp.where(lanes < heads_per_group, cbs[0], cbs[1])
        if d:
            coef = cbh * jnp.exp(acs - pltpu.roll(acs, d, axis=0)) * pltpu.roll(dt, d, axis=0)
        else:
            coef = cbh * dt
        coef = jnp.where(pos >= TOK0 + d, coef, 0.0)
        y = y + expand(coef) * (pltpu.roll(xs, d, axis=0) if d else xs)

    yn = _gate_norm(y, z_ref[...], nw_ref[...]).astype(bf16)
    y_ref[...] = jnp.dot(sel_ref[...], yn, preferred_element_type=f32).astype(bf16)

    xw = xs * expand(dt * dend)
    hi, mid, lo = _split3(expand(cdec))
    pos_w = lax.broadcasted_iota(jnp.int32, (n_rows, D_SSM), 0) % SEQ_TILE
    lhs_scr[...] = jnp.where(pos_w == 0, hi, jnp.where(pos_w == 1, mid, jnp.where(pos_w == 2, lo, xw)))
    ones = jnp.where(pos < 3, 1.0, 0.0)
    for g in range(N_GROUPS):
        rhs_scr[:, 2 * g * D_STATE:(2 * g + 1) * D_STATE] = jnp.where(tok, bm[:, g * D_STATE:(g + 1) * D_STATE], 0.0)
        rhs_scr[:, (2 * g + 1) * D_STATE:(2 * g + 2) * D_STATE] = ones
    for s in range(SSD_BS):
        for g in range(N_GROUPS):
            l8 = lhs_scr[s * SEQ_TILE:(s + 1) * SEQ_TILE, g * half:(g + 1) * half].astype(bf16)
            r8 = rhs_scr[s * SEQ_TILE:(s + 1) * SEQ_TILE, 2 * g * D_STATE:(2 * g + 2) * D_STATE].astype(bf16)
            res = lax.dot_general(l8, r8, (((0,), (0,)), ((), ())), preferred_element_type=f32)
            hnew_ref[s, g * half:(g + 1) * half, :] = (
                h0_ref[s, g * half:(g + 1) * half, :] * res[:, D_STATE:] + res[:, :D_STATE])


def _ssd_sample(pz8, pxbc8, pdt8, h0, ssm_w, e_mat, sel):
    cw, cb, dtb, alog, dx, nw = ssm_w
    n_seq = h0.shape[0]
    n_rows = SSD_BS * SEQ_TILE
    n_out = SSD_BS * 4
    rows_map = lambda i: (i, 0)
    fixed = lambda i: (0, 0)
    return pl.pallas_call(
        _ssd_sample_kernel,
        grid=(n_seq // SSD_BS,),
        in_specs=[
            pl.BlockSpec((n_rows, D_SSM), rows_map),
            pl.BlockSpec((n_rows, CONV_DIM), rows_map),
            pl.BlockSpec((n_rows, LANES), rows_map),
            pl.BlockSpec((SSD_BS, D_SSM, D_STATE), lambda i: (i, 0, 0)),
            pl.BlockSpec((SSM_CONV, CONV_DIM), fixed),
            pl.BlockSpec((1, CONV_DIM), fixed),
            pl.BlockSpec((1, LANES), fixed),
            pl.BlockSpec((1, LANES), fixed),
            pl.BlockSpec((1, D_SSM), fixed),
            pl.BlockSpec((1, D_SSM), fixed),
            pl.BlockSpec((LANES, D_SSM), fixed),
            pl.BlockSpec((n_out, n_rows), fixed),
        ],
        out_specs=(
            pl.BlockSpec((n_out, D_SSM), rows_map),
            pl.BlockSpec((SSD_BS, D_SSM, D_STATE), lambda i: (i, 0, 0)),
        ),
        out_shape=(
            jax.ShapeDtypeStruct((n_seq * 4, D_SSM), bf16),
            jax.ShapeDtypeStruct((n_seq, D_SSM, D_STATE), f32),
        ),
        scratch_shapes=[
            pltpu.VMEM((n_rows, D_SSM), f32),
            pltpu.VMEM((n_rows, D_SSM), f32),
            pltpu.VMEM((n_rows, 2 * N_GROUPS * D_STATE), f32),
        ],
        compiler_params=_cparams(("arbitrary",)),
        name="ssd_sample",
    )(pz8, pxbc8, pdt8, h0, cw, cb, dtb, alog, dx, nw, e_mat, sel)


N_LT = D_CONF // LANES
HIST = 32
CONF_RC = 64


def _conf_norm_act(acc, g, beta):
    mu = jnp.mean(acc, axis=-1, keepdims=True)
    xc = acc - mu
    var = jnp.mean(xc * xc, axis=-1, keepdims=True)
    return _silu(xc * lax.rsqrt(var + EPS) * g + beta)


def _conf_conv_tile(s_tile, w_ref, p_ref, lt, y_scr, tt):
    off = HIST - (CONF_KERNEL - 1)
    for r0 in range(0, tt, CONF_RC):
        acc = jnp.broadcast_to(p_ref[lt, 0:1, :], (CONF_RC, LANES))
        for k in range(CONF_KERNEL):
            acc = acc + w_ref[lt, k:k + 1, :] * s_tile[r0 + off + k:r0 + off + k + CONF_RC, :]
        y_scr[lt, r0:r0 + CONF_RC, :] = _conf_norm_act(acc, p_ref[lt, 1:2, :], p_ref[lt, 2:3, :])


def _conf_tile_kernel(ca_ref, cg_ref, hist_ref, w_ref, p_ref, y_ref, tail_ref, s_scr, y_scr, *, tt, n_valid):
    t = pl.program_id(1)

    @pl.when(t == 0)
    def _():
        for lt in range(N_LT):
            s_scr[lt, 0:HIST, :] = hist_ref[:, lt * LANES:(lt + 1) * LANES]

    @pl.when(t > 0)
    def _():
        s_scr[:, 0:HIST, :] = s_scr[:, tt:tt + HIST, :]

    glu = ca_ref[...] * _sigmoid(cg_ref[...])
    for lt in range(N_LT):
        s_scr[lt, HIST:HIST + tt, :] = glu[:, lt * LANES:(lt + 1) * LANES]

    def lane_tile(lt, carry):
        _conf_conv_tile(s_scr.at[lt], w_ref, p_ref, lt, y_scr, tt)
        return carry

    lax.fori_loop(0, N_LT, lane_tile, 0)
    for lt in range(N_LT):
        y_ref[:, lt * LANES:(lt + 1) * LANES] = y_scr[lt].astype(bf16)
        tail_ref[:, lt * LANES:(lt + 1) * LANES] = s_scr[lt, n_valid:n_valid + HIST, :]


def _conf_tiles(pc, hist, conf_w, *, n_seq, n_tiles, tt, row_block0, n_valid):
    w3, p3 = conf_w
    in_specs = [
        pl.BlockSpec((tt, D_CONF), lambda b, t: (row_block0 + b * n_tiles + t, 0)),
        pl.BlockSpec((tt, D_CONF), lambda b, t: (row_block0 + b * n_tiles + t, 1)),
        pl.BlockSpec((HIST, D_CONF), lambda b, t: (0, 0)),
        pl.BlockSpec((N_LT, HIST, LANES), lambda b, t: (0, 0, 0)),
        pl.BlockSpec((N_LT, SUBLANES, LANES), lambda b, t: (0, 0, 0)),
    ]
    return pl.pallas_call(
        functools.partial(_conf_tile_kernel, tt=tt, n_valid=n_valid),
        grid=(n_seq, n_tiles),
        in_specs=in_specs,
        out_specs=(
            pl.BlockSpec((tt, D_CONF), lambda b, t: (b * n_tiles + t, 0)),
            pl.BlockSpec((None, HIST, D_CONF), lambda b, t: (b, 0, 0)),
        ),
        out_shape=(
            jax.ShapeDtypeStruct((n_seq * n_tiles * tt, D_CONF), bf16),
            jax.ShapeDtypeStruct((n_seq, HIST, D_CONF), f32),
        ),
        scratch_shapes=[
            pltpu.VMEM((N_LT, HIST + tt, LANES), f32),
            pltpu.VMEM((N_LT, tt, LANES), f32),
        ],
        compiler_params=_cparams(("arbitrary", "arbitrary")),
        name="conf_tiles",
    )(pc, pc, hist, w3, p3)


CONF_BS = 16


def _conf_sample_kernel(ca_ref, cg_ref, buf_ref, w_ref, p_ref, y_ref, newbuf_ref):
    n_hist = CONF_KERNEL - 1
    n_tok = ca_ref.shape[0]
    glu = [ca_ref[t] * _sigmoid(cg_ref[t]) for t in range(n_tok)]
    win = lambda j: buf_ref[j] if j < n_hist else glu[j - n_hist]
    for j in range(n_hist):
        newbuf_ref[j] = win(j + n_tok)
    for t in range(n_tok):
        acc = jnp.broadcast_to(p_ref[0:1, :], (CONF_BS, D_CONF))
        for k in range(CONF_KERNEL):
            acc = acc + w_ref[k:k + 1, :] * win(t + k)
        for lt in range(N_LT):
            sl = slice(lt * LANES, (lt + 1) * LANES)
            y_ref[t, :, sl] = _conf_norm_act(acc[:, sl], p_ref[1:2, sl], p_ref[2:3, sl]).astype(bf16)


def _conf_sample(pc_t, buf_t, w2, p2):
    n_tok, n_seq = pc_t.shape[:2]
    n_hist = CONF_KERNEL - 1
    seq_blocks = lambda col: (lambda i: (0, i, col))
    return pl.pallas_call(
        _conf_sample_kernel,
        grid=(n_seq // CONF_BS,),
        in_specs=[
            pl.BlockSpec((n_tok, CONF_BS, D_CONF), seq_blocks(0)),
            pl.BlockSpec((n_tok, CONF_BS, D_CONF), seq_blocks(1)),
            pl.BlockSpec((n_hist, CONF_BS, D_CONF), seq_blocks(0)),
            pl.BlockSpec((CONF_KERNEL, D_CONF), lambda i: (0, 0)),
            pl.BlockSpec((3, D_CONF), lambda i: (0, 0)),
        ],
        out_specs=(
            pl.BlockSpec((n_tok, CONF_BS, D_CONF), seq_blocks(0)),
            pl.BlockSpec((n_hist, CONF_BS, D_CONF), seq_blocks(0)),
        ),
        out_shape=(
            jax.ShapeDtypeStruct((n_tok, n_seq, D_CONF), bf16),
            jax.ShapeDtypeStruct((n_hist, n_seq, D_CONF), f32),
        ),
        compiler_params=_cparams(("arbitrary",)),
        name="conf_sample",
    )(pc_t, pc_t, buf_t, w2, p2)


def _seq_tiles(rows, n_seq, n_tok):
    w = rows.shape[-1]
    return jnp.pad(rows.reshape(n_seq, n_tok, w), ((0, 0), (SEQ_TILE - n_tok, 0), (0, 0))).reshape(n_seq * SEQ_TILE, w)


def kernel(x_prompt, x_sample, state_ssm, state_ssm_conv, state_conf_conv, meta_tokens, ffn1_norm, ffn1_w_gate, ffn1_w_up, ffn1_w_down, mix_norm, w_in, ssm_conv_w, ssm_conv_b, ssm_dt_bias, ssm_A_log, ssm_D, ssm_norm, conf_conv_w, conf_conv_b, conf_ln_g, conf_ln_b, w_out, ffn2_norm, ffn2_w_gate, ffn2_w_up, ffn2_w_down, final_norm):
    n_pb, n_pt = x_prompt.shape[0], x_prompt.shape[1]
    n_sb, n_st = x_sample.shape[0], x_sample.shape[1]
    assert (n_pb * n_pt, n_sb * n_st, n_st) == (N_PROMPT_ROWS, N_SAMPLE_ROWS, 4) and n_pt % CHUNK == 0

    row = lambda v: v.reshape(1, -1)
    x_rows = (x_prompt.reshape(N_PROMPT_ROWS, D_MODEL), x_sample.reshape(N_SAMPLE_ROWS, D_MODEL), meta_tokens)
    h, u, w_t = _ffn(x_rows, row(ffn1_norm[0]), _to_bf16(ffn1_w_gate, 256), _to_bf16(ffn1_w_up, 256),
                     _to_bf16(ffn1_w_down, 704), row(mix_norm[0]), side=(jnp.transpose(w_in[0]), 96))

    c_in0 = D_SSM + CONV_DIM
    pzx, wg2 = _matmul_nt(u, w_t, row0=0, n=c_in0, tn=1536, side=(ffn2_w_gate, 64))
    pdt = _matmul_nt(u, w_t, row0=c_in0, n=LANES, tn=LANES)
    pc, wu2 = _matmul_nt(u, w_t, row0=c_in0 + N_HEADS, n=2 * D_CONF, tn=1024, tm=1104, side=(ffn2_w_up, 64))

    pad_heads = lambda v: jnp.pad(v.reshape(1, N_HEADS), ((0, 0), (0, LANES - N_HEADS)))
    ssm_w = (ssm_conv_w[0], row(ssm_conv_b[0]), pad_heads(ssm_dt_bias[0]), pad_heads(ssm_A_log[0]),
             row(jnp.repeat(ssm_D[0], HEAD_DIM)), row(ssm_norm[0]))
    tri = jnp.tril(jnp.ones((CHUNK, CHUNK), f32))
    e_mat = (jnp.arange(LANES)[:, None] == (jnp.arange(D_SSM)[None, :] // HEAD_DIM)).astype(bf16)
    lane_major = lambda v: v.reshape(v.shape[0], N_LT, LANES).transpose(1, 0, 2)
    conf_w = (lane_major(jnp.pad(conf_conv_w[0], ((0, HIST - CONF_KERNEL), (0, 0)))),
              lane_major(jnp.pad(jnp.stack([conf_conv_b[0], conf_ln_g[0], conf_ln_b[0]]), ((0, SUBLANES - 3), (0, 0)))))
    out_rows = jnp.arange(SSD_BS * 4)
    sel = ((out_rows // 4) * SEQ_TILE + TOK0 + out_rows % 4)[:, None] == jnp.arange(SSD_BS * SEQ_TILE)[None, :]
    sel = sel.astype(bf16)

    meta_block = ROW_META // CHUNK
    ys_m, h_meta, tail_meta = _ssd_chunks(
        pzx, pdt, jnp.zeros((SUBLANES, CONV_DIM), f32), jnp.zeros((D_STATE, D_SSM), f32), ssm_w, (tri, e_mat),
        n_seq=1, n_chunks=1, row_block0=meta_block, n_valid=N_META, transpose_out=False)
    yc_m, hist_meta = _conf_tiles(pc, jnp.zeros((HIST, D_CONF), f32), conf_w,
                                  n_seq=1, n_tiles=1, tt=CHUNK, row_block0=meta_block, n_valid=N_META)

    ys_p, p_ssm, p_tail, wo = _ssd_chunks(
        pzx, pdt, tail_meta[0], h_meta[0], ssm_w, (tri, e_mat),
        n_seq=n_pb, n_chunks=n_pt // CHUNK, row_block0=0, n_valid=CHUNK, transpose_out=True, side=(w_out, 64))
    conf_tt = 256
    yc_p, p_hist = _conf_tiles(pc, hist_meta[0], conf_w,
                               n_seq=n_pb, n_tiles=n_pt // conf_tt, tt=conf_tt, row_block0=0, n_valid=conf_tt)

    smp = slice(ROW_SAMPLE, ROW_SAMPLE + N_SAMPLE_ROWS)
    pzx_s = pzx[smp]
    pxbc_s = pzx_s[:, D_SSM:].reshape(n_sb, n_st, CONV_DIM)
    pxbc8 = jnp.concatenate([jnp.zeros((n_sb, 1, CONV_DIM), f32), state_ssm_conv[0], pxbc_s], axis=1)
    ys_s, s_ssm = _ssd_sample(_seq_tiles(pzx_s[:, :D_SSM], n_sb, n_st), pxbc8.reshape(n_sb * SEQ_TILE, CONV_DIM),
                              _seq_tiles(pdt[smp], n_sb, n_st), state_ssm[0].reshape(n_sb, D_SSM, D_STATE),
                              ssm_w, e_mat, sel)
    to_time_major = lambda v: jnp.transpose(v, (1, 0, 2))
    yc_t, cconv_t = _conf_sample(to_time_major(pc[smp].reshape(n_sb, n_st, 2 * D_CONF)), to_time_major(state_conf_conv[0]),
                                 conf_conv_w[0], jnp.stack([conf_conv_b[0], conf_ln_g[0], conf_ln_b[0]]))
    yc_s = to_time_major(yc_t).reshape(N_SAMPLE_ROWS, D_CONF)
    s_cconv = to_time_major(cconv_t)

    h2, wd2 = _outproj(h, (ys_p, ys_s, ys_m), (yc_p, yc_s, yc_m), wo, side=(ffn2_w_down, 256))
    y_prompt, y_sample = _ffn((h2,), row(ffn2_norm[0]), wg2, wu2, wd2, row(final_norm),
                              dst_rows=(N_PROMPT_ROWS, N_SAMPLE_ROWS), tf=512)
    n_hist = CONF_KERNEL - 1
    return (
        y_prompt.reshape(n_pb, n_pt, D_MODEL), y_sample.reshape(n_sb, n_st, D_MODEL),
        p_ssm.reshape(1, n_pb, N_HEADS, HEAD_DIM, D_STATE),
        p_tail[:, SUBLANES - (SSM_CONV - 1):, :][None],
        p_hist[:, HIST - n_hist:, :][None],
        s_ssm.reshape(1, n_sb, N_HEADS, HEAD_DIM, D_STATE),
        pxbc_s[:, n_st - (SSM_CONV - 1):, :][None],
        s_cconv[None],
    )
```

```python
import functools

import jax
import jax.numpy as jnp
from jax import lax
from jax.experimental import pallas as pl
from jax.experimental.pallas import tpu as pltpu

f32 = jnp.float32
bf16 = jnp.bfloat16

D_MODEL = 2048
N_META = 16
D_SSM = 2048
D_CONF = 2048
HEAD_DIM = 64
N_HEADS = 32
N_GROUPS = 2
D_STATE = 128
SSM_CONV = 4
CONV_DIM = D_SSM + 2 * N_GROUPS * D_STATE
CONF_KERNEL = 31
CONF_GROUP = 128
D_FF = 5632
EPS = 1e-5
LOG2E = 1.4426950408889634

LANES = 128
SUBLANES = 8
CHUNK = 128
VMEM_LIMIT = 60 * 1024 * 1024

N_PROMPT_ROWS = 4 * 2048
N_SAMPLE_ROWS = 128 * 4
ROW_SAMPLE = N_PROMPT_ROWS
ROW_META = N_PROMPT_ROWS + N_SAMPLE_ROWS
N_ROWS = ROW_META + CHUNK
TM = 736
TM_PROJ = 1104
TF = 512


def _cparams(sem):
    return pltpu.CompilerParams(dimension_semantics=sem, vmem_limit_bytes=VMEM_LIMIT)


def _sigmoid(x):
    return 1.0 / (1.0 + jnp.exp(-x))


def _silu(x):
    return x * _sigmoid(x)


def _rms(x, g):
    return x * lax.rsqrt(jnp.mean(x * x, axis=-1, keepdims=True) + EPS) * g


def _softplus(x):
    return jnp.maximum(x, 0.0) + jnp.log1p(jnp.exp(-jnp.abs(x)))


def _split3(v):
    hi = v.astype(bf16).astype(f32)
    r1 = v - hi
    mid = r1.astype(bf16).astype(f32)
    return hi, mid, r1 - mid


def _row_segments(src_rows, n_total, tm):
    bounds, start = [], 0
    for k, n in enumerate(src_rows):
        bounds.append((k, start, n))
        start += n
    if start < n_total:
        bounds.append((None, start, n_total - start))
    table = []
    for t in range(n_total // tm):
        lo, hi = t * tm, (t + 1) * tm
        segs = []
        for k, s, n in bounds:
            a, b = max(lo, s), min(hi, s + n)
            if a < b:
                segs.append((k, a - s, a - lo, b - a))
        table.append(segs)
    return table


def _load_tile(i, seg_table, groups, sem):
    tm = groups[0][2].shape[0]
    special = [t for t, segs in enumerate(seg_table) if segs != [(0, t * tm, 0, tm)]]
    first = special[0] if special else len(seg_table)
    assert special == list(range(first, len(seg_table)))

    @pl.when(i < first)
    def _():
        for main_ref, _, dst_ref in groups:
            dst_ref[...] = main_ref[...]

    for t in special:
        @pl.when(i == t)
        def _(segs=seg_table[t]):
            copies = []
            for main_ref, tail_srcs, dst_ref in groups:
                for k, r0, d0, n in segs:
                    if k is None:
                        dst_ref[d0:d0 + n, :] = jnp.zeros((n, dst_ref.shape[1]), dst_ref.dtype)
                    elif k == 0:
                        dst_ref[d0:d0 + n, :] = main_ref[d0:d0 + n, :]
                    else:
                        cp = pltpu.make_async_copy(tail_srcs[k - 1].at[pl.ds(r0, n)], dst_ref.at[pl.ds(d0, n)], sem)
                        cp.start()
                        copies.append(cp)
            for cp in copies:
                cp.wait()


def _side_cast(w, block_rows, grid):
    r, c = w.shape[-2:]
    n_blocks = r // block_rows
    assert r % block_rows == 0 and n_blocks <= grid[0] * grid[1]
    blk = lambda i, j: jnp.minimum(i * grid[1] + j, n_blocks - 1)
    if w.ndim == 3:
        in_spec = pl.BlockSpec((None, block_rows, c), lambda i, j: (0, blk(i, j), 0))
    else:
        in_spec = pl.BlockSpec((block_rows, c), lambda i, j: (blk(i, j), 0))
    return in_spec, pl.BlockSpec((block_rows, c), lambda i, j: (blk(i, j), 0)), jax.ShapeDtypeStruct((r, c), bf16)


def _run_side_cast(side_refs):
    for src_ref, dst_ref in side_refs:
        dst_ref[...] = src_ref[...].astype(bf16)


def _drain_rows(i, out_table, dsts, acc_scr, sems):
    def copies(t):
        return [pltpu.make_async_copy(acc_scr.at[t % 2, pl.ds(s0, n)], dsts[k].at[pl.ds(r0, n)], sems.at[t % 2])
                for k, r0, s0, n in out_table[t] if k is not None]

    n_tiles = len(out_table)
    for t in range(n_tiles):
        @pl.when(i == t)
        def _(t=t):
            if t > 0:
                for cp in copies(t - 1):
                    cp.wait()
            for cp in copies(t):
                cp.start()
            if t == n_tiles - 1:
                for cp in copies(t):
                    cp.wait()


def _ffn_kernel(*refs, seg_table, out_table, n_src, n_side):
    x_ref, tail_srcs = refs[0], refs[1:n_src]
    nw_ref, wg_ref, wu_ref, wd_ref, pw_ref = refs[n_src:n_src + 5]
    pos = n_src + 5
    side_in, pos = refs[pos:pos + n_side], pos + n_side
    i = pl.program_id(0)
    f = pl.program_id(1)
    if out_table is None:
        (acc, yn_ref), pos = refs[pos:pos + 2], pos + 2
        side_out, pos = refs[pos:pos + n_side], pos + n_side
        xn_ref, sem = refs[pos:]
    else:
        n_dst = 1 + max(k for segs in out_table for k, _, _, _ in segs if k is not None)
        dsts, pos = refs[pos:pos + n_dst], pos + n_dst
        side_out, pos = refs[pos:pos + n_side], pos + n_side
        acc_scr, xn_ref, sem, out_sems = refs[pos:]
        acc = acc_scr.at[i % 2]
    _run_side_cast(zip(side_in, side_out))

    @pl.when(f == 0)
    def _():
        _load_tile(i, seg_table, [(x_ref, tail_srcs, acc)], sem)
        xn_ref[...] = _rms(acc[...], nw_ref[...]).astype(bf16)

    xn = xn_ref[...]
    g = jnp.dot(xn, wg_ref[...], preferred_element_type=f32)
    u = jnp.dot(xn, wu_ref[...], preferred_element_type=f32)
    a = (_silu(g) * u).astype(bf16)
    acc[...] += 0.5 * jnp.dot(a, wd_ref[...], preferred_element_type=f32)

    @pl.when(f == pl.num_programs(1) - 1)
    def _():
        if out_table is None:
            yn_ref[...] = _rms(acc[...], pw_ref[...]).astype(yn_ref.dtype)
        else:
            acc[...] = _rms(acc[...], pw_ref[...])
            _drain_rows(i, out_table, dsts, acc_scr, out_sems)


def _ffn(srcs, norm_w, wg, wu, wd, post_w, *, dst_rows=None, side=None, tf=TF):
    seg_table = _row_segments([s.shape[0] for s in srcs], N_ROWS, TM)
    grid = (N_ROWS // TM, D_FF // tf)
    side_specs = [_side_cast(*side, grid)] if side else []
    row_spec = pl.BlockSpec((TM, D_MODEL), lambda i, f: (i, 0))
    vec_spec = pl.BlockSpec((1, D_MODEL), lambda i, f: (0, 0))
    scratch = [pltpu.VMEM((TM, D_MODEL), bf16), pltpu.SemaphoreType.DMA(())]
    if dst_rows is None:
        out_table = None
        out_shape = (jax.ShapeDtypeStruct((N_ROWS, D_MODEL), f32), jax.ShapeDtypeStruct((N_ROWS, D_MODEL), bf16))
        out_specs = (row_spec, row_spec)
    else:
        out_table = _row_segments(dst_rows, N_ROWS, TM)
        out_shape = tuple(jax.ShapeDtypeStruct((n, D_MODEL), f32) for n in dst_rows)
        out_specs = tuple(pl.BlockSpec(memory_space=pl.ANY) for _ in dst_rows)
        scratch = [pltpu.VMEM((2, TM, D_MODEL), f32)] + scratch + [pltpu.SemaphoreType.DMA((2,))]
    return pl.pallas_call(
        functools.partial(_ffn_kernel, seg_table=seg_table, out_table=out_table, n_src=len(srcs),
                          n_side=len(side_specs)),
        grid=grid,
        in_specs=[row_spec] + [pl.BlockSpec(memory_space=pl.ANY)] * (len(srcs) - 1) + [
            vec_spec,
            pl.BlockSpec((D_MODEL, tf), lambda i, f: (0, f)),
            pl.BlockSpec((D_MODEL, tf), lambda i, f: (0, f)),
            pl.BlockSpec((tf, D_MODEL), lambda i, f: (f, 0)),
            vec_spec,
        ] + [s[0] for s in side_specs],
        out_specs=tuple(out_specs) + tuple(s[1] for s in side_specs),
        out_shape=tuple(out_shape) + tuple(s[2] for s in side_specs),
        scratch_shapes=scratch,
        compiler_params=_cparams(("arbitrary", "arbitrary")),
        name="ffn",
    )(*srcs, norm_w, wg, wu, wd, post_w, *([side[0]] if side else []))


def _cast_kernel(x_ref, o_ref):
    o_ref[...] = x_ref[...].astype(bf16)


def _to_bf16(w, block_rows):
    r, c = w.shape[-2:]
    if w.ndim == 3:
        in_spec = pl.BlockSpec((None, block_rows, c), lambda i: (0, i, 0))
    else:
        in_spec = pl.BlockSpec((block_rows, c), lambda i: (i, 0))
    return pl.pallas_call(
        _cast_kernel,
        grid=(r // block_rows,),
        in_specs=[in_spec],
        out_specs=pl.BlockSpec((block_rows, c), lambda i: (i, 0)),
        out_shape=jax.ShapeDtypeStruct((r, c), bf16),
        compiler_params=_cparams(("arbitrary",)),
        name="cast_bf16",
    )(w)


def _matmul_nt_kernel(a_ref, w_ref, *rest):
    n_side = len(rest) // 2
    o_ref = rest[n_side]
    _run_side_cast(zip(rest[:n_side], rest[n_side + 1:]))
    o_ref[...] = lax.dot_general(a_ref[...], w_ref[...], (((1,), (1,)), ((), ())), preferred_element_type=f32)


def _matmul_nt(a, w_t, *, row0, n, tn, tm=TM, side=None):
    n_rows, k = a.shape
    align = 2 * 16
    assert row0 % align == 0 and tn % align == 0 and n % tn == 0 and n_rows % tm == 0
    grid = (n_rows // tm, n // tn)
    side_specs = [_side_cast(*side, grid)] if side else []
    out = pl.pallas_call(
        _matmul_nt_kernel,
        grid=grid,
        in_specs=[
            pl.BlockSpec((tm, k), lambda i, j: (i, 0)),
            pl.BlockSpec((pl.Element(tn), pl.Element(k)), lambda i, j: (pl.multiple_of(row0 + tn * j, align), 0)),
        ] + [s[0] for s in side_specs],
        out_specs=(pl.BlockSpec((tm, tn), lambda i, j: (i, j)),) + tuple(s[1] for s in side_specs),
        out_shape=(jax.ShapeDtypeStruct((n_rows, n), f32),) + tuple(s[2] for s in side_specs),
        compiler_params=_cparams(("arbitrary", "arbitrary")),
        name="in_proj",
    )(a, w_t, *([side[0]] if side else []))
    return out if side else out[0]


def _outproj_kernel(*refs, seg_table, n_src, n_side):
    ys_ref, ys_tails = refs[0], refs[1:n_src]
    yc_ref, yc_tails = refs[n_src], refs[n_src + 1:2 * n_src]
    h_ref, w0_ref, w1_ref = refs[2 * n_src:2 * n_src + 3]
    pos = 2 * n_src + 3
    side_in, o_ref, side_out = refs[pos:pos + n_side], refs[pos + n_side], refs[pos + n_side + 1:pos + 2 * n_side + 1]
    ys_scr, yc_scr, sem = refs[pos + 2 * n_side + 1:]
    _run_side_cast(zip(side_in, side_out))

    @pl.when(pl.program_id(1) == 0)
    def _():
        _load_tile(pl.program_id(0), seg_table, [(ys_ref, ys_tails, ys_scr), (yc_ref, yc_tails, yc_scr)], sem)

    acc = jnp.dot(ys_scr[...], w0_ref[...], preferred_element_type=f32)
    acc = acc + jnp.dot(yc_scr[...], w1_ref[...], preferred_element_type=f32)
    o_ref[...] = h_ref[...] + acc


def _outproj(h, ys_srcs, yc_srcs, w_out, *, tm, tn, side=None):
    n_rows = h.shape[0]
    assert n_rows % tm == 0
    seg_table = _row_segments([s.shape[0] for s in ys_srcs], n_rows, tm)
    n_src = len(ys_srcs)
    grid = (n_rows // tm, D_MODEL // tn)
    side_specs = [_side_cast(*side, grid)] if side else []
    y_specs = [pl.BlockSpec((tm, D_SSM), lambda i, j: (i, 0))] + [pl.BlockSpec(memory_space=pl.ANY)] * (n_src - 1)
    out = pl.pallas_call(
        functools.partial(_outproj_kernel, seg_table=seg_table, n_src=n_src, n_side=len(side_specs)),
        grid=grid,
        in_specs=y_specs + y_specs + [
            pl.BlockSpec((tm, tn), lambda i, j: (i, j)),
            pl.BlockSpec((D_SSM, tn), lambda i, j: (0, j)),
            pl.BlockSpec((D_CONF, tn), lambda i, j: (1, j)),
        ] + [s[0] for s in side_specs],
        out_specs=(pl.BlockSpec((tm, tn), lambda i, j: (i, j)),) + tuple(s[1] for s in side_specs),
        out_shape=(jax.ShapeDtypeStruct((n_rows, D_MODEL), f32),) + tuple(s[2] for s in side_specs),
        scratch_shapes=[pltpu.VMEM((tm, D_SSM), bf16), pltpu.VMEM((tm, D_CONF), bf16), pltpu.SemaphoreType.DMA(())],
        compiler_params=_cparams(("arbitrary", "arbitrary")),
        name="out_proj",
    )(*ys_srcs, *yc_srcs, h, w_out, w_out, *([side[0]] if side else []))
    return out if side else out[0]


def _gate_norm(y, z, nw):
    y = y * _silu(z)
    half = D_SSM // N_GROUPS
    parts = []
    for g in range(N_GROUPS):
        yg = y[:, g * half:(g + 1) * half]
        ms = jnp.mean(yg * yg, axis=-1, keepdims=True)
        parts.append(yg * lax.rsqrt(ms + EPS) * nw[:, g * half:(g + 1) * half])
    return jnp.concatenate(parts, axis=1)


def _ssd_chunk_math(z, xbc, dt_raw, cinit_ref, hinit_ref, cw_ref, cb_ref, dtb_ref, alog_ref,
                    dx_ref, nw_ref, tri_ref, e_ref, y_ref, hout_ref, tail_ref, win_scr, ht_scr, y_scr,
                    *, n_valid, transpose_out):
    c = pl.program_id(1)

    @pl.when(c == 0)
    def _():
        ht_scr[...] = hinit_ref[...]
        win_scr[0:SUBLANES, :] = cinit_ref[...]

    win_scr[SUBLANES:SUBLANES + CHUNK, :] = xbc
    acc = cb_ref[...] + cw_ref[SSM_CONV - 1:SSM_CONV, :] * xbc
    for j in range(1, SSM_CONV):
        acc = acc + cw_ref[SSM_CONV - 1 - j:SSM_CONV - j, :] * win_scr[SUBLANES - j:SUBLANES - j + CHUNK, :]
    tail_ref[...] = win_scr[n_valid:n_valid + SUBLANES, :]
    win_scr[0:SUBLANES, :] = win_scr[CHUNK:CHUNK + SUBLANES, :]
    xc = _silu(acc)
    b_mats = [xc[:, D_SSM + g * D_STATE:D_SSM + (g + 1) * D_STATE] for g in range(N_GROUPS)]
    c_mats = [xc[:, D_SSM + (N_GROUPS + g) * D_STATE:D_SSM + (N_GROUPS + g + 1) * D_STATE] for g in range(N_GROUPS)]

    rows = lax.broadcasted_iota(jnp.int32, (CHUNK, LANES), 0)
    lanes = lax.broadcasted_iota(jnp.int32, (CHUNK, LANES), 1)
    dt = _softplus(dt_raw + dtb_ref[...])
    if n_valid < CHUNK:
        dt = jnp.where(rows < n_valid, dt, 0.0)
    a = dt * (-jnp.exp(alog_ref[...]))
    acs = jnp.dot(tri_ref[...], a, precision=lax.Precision.HIGHEST, preferred_element_type=f32) * LOG2E
    total = acs[CHUNK - 1:CHUNK, :]
    dend = jnp.exp2(total - acs)
    acs_t = acs.T
    dt_t = dt.T
    w_t = (dt * dend).T
    hi, mid, lo = _split3(jnp.exp2(total))
    rid = lax.broadcasted_iota(jnp.int32, (SUBLANES, LANES), 0)
    cd3 = jnp.where(rid == 0, hi, jnp.where(rid == 1, mid, jnp.where(rid == 2, lo, 0.0)))
    cdx = jnp.sum(jnp.dot(cd3.astype(bf16), e_ref[...], preferred_element_type=f32), axis=0, keepdims=True)

    cb_mats, bt_mats = [], []
    for g in range(N_GROUPS):
        cb_mats.append(lax.dot_general(c_mats[g].astype(bf16), b_mats[g].astype(bf16), (((1,), (1,)), ((), ())),
                                       preferred_element_type=f32))
        bt_mats.append(b_mats[g].T)

    lane_lo = lanes < HEAD_DIM
    tril = rows >= lanes
    heads_per_group = N_HEADS // N_GROUPS
    for j in range(N_HEADS // 2):
        sl = slice(j * LANES, (j + 1) * LANES)
        xs_pair = xc[:, sl]
        ht_pair = ht_scr[:, sl]
        y_pair = jnp.zeros((CHUNK, LANES), f32)
        upd = jnp.zeros((D_STATE, LANES), f32)
        for half in range(2):
            h = 2 * j + half
            g = h // heads_per_group
            keep = lane_lo if half == 0 else jnp.logical_not(lane_lo)
            xs_m = jnp.where(keep, xs_pair, 0.0).astype(bf16)
            ht_m = jnp.where(keep, ht_pair, 0.0).astype(bf16)
            acs_col = jnp.broadcast_to(acs[:, h:h + 1], (CHUNK, CHUNK))
            acs_row = jnp.broadcast_to(acs_t[h:h + 1, :], (CHUNK, CHUNK))
            lmat = jnp.exp2(jnp.where(tril, acs_col - acs_row, -jnp.inf))
            m_h = cb_mats[g] * lmat * dt_t[h:h + 1, :]
            ce_h = c_mats[g] * jnp.exp2(acs_col)
            lhs = jnp.concatenate([m_h, ce_h], axis=1).astype(bf16)
            rhs = jnp.concatenate([xs_m, ht_m], axis=0)
            y_pair = y_pair + jnp.dot(lhs, rhs, preferred_element_type=f32)
            bw = (bt_mats[g] * w_t[h:h + 1, :]).astype(bf16)
            upd = upd + jnp.dot(bw, xs_m, preferred_element_type=f32)
        y_scr[:, sl] = y_pair
        ht_scr[:, sl] = ht_pair * cdx[:, sl] + upd

    xs = xc[:, :D_SSM]
    y = y_scr[...] + xs * dx_ref[...]
    y_ref[...] = _gate_norm(y, z(), nw_ref[...]).astype(bf16)

    @pl.when(c == pl.num_programs(1) - 1)
    def _():
        if transpose_out:
            hout_ref[...] = ht_scr[...].T
        else:
            hout_ref[...] = ht_scr[...]


N_SSD_PARAMS = 10


def _ssd_chunk_kernel(z_ref, xs_ref, bc_ref, dt_ref, *rest, n_valid, transpose_out, n_side):
    params, rest = rest[:N_SSD_PARAMS], rest[N_SSD_PARAMS:]
    side_in, outs = rest[:n_side], rest[n_side:n_side + 3]
    side_out, scratch = rest[n_side + 3:2 * n_side + 3], rest[2 * n_side + 3:]
    _run_side_cast(zip(side_in, side_out))
    xbc = jnp.concatenate([xs_ref[...], bc_ref[...]], axis=1)
    _ssd_chunk_math(lambda: z_ref[...], xbc, dt_ref[...], *params, *outs, *scratch,
                    n_valid=n_valid, transpose_out=transpose_out)


def _ssd_chunks(pzx, pdt, cinit, hinit, ssm_w, consts, *, n_seq, n_chunks, row_block0, n_valid, transpose_out,
                side=None):
    cw, cb, dtb, alog, dx, nw = ssm_w
    tri, e_mat = consts
    bc_dim = CONV_DIM - D_SSM
    side_specs = [_side_cast(*side, (n_seq, n_chunks))] if side else []
    rows_map = lambda b, c: (row_block0 + b * n_chunks + c, 0)
    fixed = lambda b, c: (0, 0)
    in_specs = [
        pl.BlockSpec((CHUNK, D_SSM), rows_map),
        pl.BlockSpec((CHUNK, D_SSM), lambda b, c: (row_block0 + b * n_chunks + c, 1)),
        pl.BlockSpec((CHUNK, bc_dim), lambda b, c: (row_block0 + b * n_chunks + c, 2 * D_SSM // bc_dim)),
        pl.BlockSpec((CHUNK, LANES), rows_map),
        pl.BlockSpec((SUBLANES, CONV_DIM), fixed),
        pl.BlockSpec((D_STATE, D_SSM), fixed),
        pl.BlockSpec((SSM_CONV, CONV_DIM), fixed),
        pl.BlockSpec((1, CONV_DIM), fixed),
        pl.BlockSpec((1, LANES), fixed),
        pl.BlockSpec((1, LANES), fixed),
        pl.BlockSpec((1, D_SSM), fixed),
        pl.BlockSpec((1, D_SSM), fixed),
        pl.BlockSpec((CHUNK, CHUNK), fixed),
        pl.BlockSpec((LANES, D_SSM), fixed),
    ]
    args = [pzx, pzx, pzx, pdt, cinit, hinit, cw, cb, dtb, alog, dx, nw, tri, e_mat] + ([side[0]] if side else [])
    hshape = (n_seq, D_SSM, D_STATE) if transpose_out else (n_seq, D_STATE, D_SSM)
    return pl.pallas_call(
        functools.partial(_ssd_chunk_kernel, n_valid=n_valid, transpose_out=transpose_out, n_side=len(side_specs)),
        grid=(n_seq, n_chunks),
        in_specs=in_specs + [s[0] for s in side_specs],
        out_specs=(
            pl.BlockSpec((CHUNK, D_SSM), lambda b, c: (b * n_chunks + c, 0)),
            pl.BlockSpec((None,) + hshape[1:], lambda b, c: (b, 0, 0)),
            pl.BlockSpec((None, SUBLANES, CONV_DIM), lambda b, c: (b, 0, 0)),
        ) + tuple(s[1] for s in side_specs),
        out_shape=(
            jax.ShapeDtypeStruct((n_seq * n_chunks * CHUNK, D_SSM), bf16),
            jax.ShapeDtypeStruct(hshape, f32),
            jax.ShapeDtypeStruct((n_seq, SUBLANES, CONV_DIM), f32),
        ) + tuple(s[2] for s in side_specs),
        scratch_shapes=[
            pltpu.VMEM((SUBLANES + CHUNK, CONV_DIM), f32),
            pltpu.VMEM((D_STATE, D_SSM), f32),
            pltpu.VMEM((CHUNK, D_SSM), f32),
        ],
        compiler_params=_cparams(("arbitrary", "arbitrary")),
        name="ssd_chunks",
    )(*args)


SEQ_TILE = SUBLANES
TOK0 = SEQ_TILE - 4
SSD_BS = 8


def _ssd_sample_kernel(z_ref, xbc_ref, dt_ref, h0_ref, cw_ref, cb_ref, dtb_ref, alog_ref, dx_ref, nw_ref,
                       e_ref, sel_ref, y_ref, hnew_ref, yoff_scr, lhs_scr, rhs_scr):
    n_rows = SSD_BS * SEQ_TILE
    half = D_SSM // N_GROUPS
    x8 = xbc_ref[...]
    acc = cb_ref[...] + cw_ref[SSM_CONV - 1:SSM_CONV, :] * x8
    for j in range(1, SSM_CONV):
        acc = acc + cw_ref[SSM_CONV - 1 - j:SSM_CONV - j, :] * pltpu.roll(x8, j, axis=0)
    xc = _silu(acc)
    xs = xc[:, :D_SSM]
    bm = xc[:, D_SSM:D_SSM + N_GROUPS * D_STATE]
    cm = xc[:, D_SSM + N_GROUPS * D_STATE:]

    pos = lax.broadcasted_iota(jnp.int32, (n_rows, LANES), 0) % SEQ_TILE
    lanes = lax.broadcasted_iota(jnp.int32, (n_rows, LANES), 1)
    tok = pos >= TOK0
    dt = jnp.where(tok, _softplus(dt_ref[...] + dtb_ref[...]), 0.0)
    a = dt * (-jnp.exp(alog_ref[...]))
    acs = a
    for d in range(1, 4):
        acs = acs + pltpu.roll(a, d, axis=0)
    tot = jnp.sum(a.reshape(SSD_BS, SEQ_TILE, LANES), axis=1, keepdims=True)
    tot = jnp.broadcast_to(tot, (SSD_BS, SEQ_TILE, LANES)).reshape(n_rows, LANES)
    dend = jnp.where(tok, jnp.exp(tot - acs), 0.0)
    eacs = jnp.where(tok, jnp.exp(acs), 0.0)
    cdec = jnp.exp(tot)

    def expand(v):
        hi, mid, lo = _split3(v)
        stack = jnp.concatenate([hi, mid, lo], axis=0).astype(bf16)
        r = jnp.dot(stack, e_ref[...], preferred_element_type=f32)
        return r[0:n_rows] + r[n_rows:2 * n_rows] + r[2 * n_rows:3 * n_rows]

    heads_per_group = N_HEADS // N_GROUPS
    y = expand(eacs)
    for s in range(SSD_BS):
        for g in range(N_GROUPS):
            c8 = cm[s * SEQ_TILE:(s + 1) * SEQ_TILE, g * D_STATE:(g + 1) * D_STATE].astype(bf16)
            hg = h0_ref[s, g * half:(g + 1) * half, :].astype(bf16)
            yoff_scr[s * SEQ_TILE:(s + 1) * SEQ_TILE, g * half:(g + 1) * half] = lax.dot_general(
                c8, hg, (((1,), (1,)), ((), ())), preferred_element_type=f32)
    y = y * yoff_scr[...] + xs * dx_ref[...]
    for d in range(4):
        b_sh = pltpu.roll(bm, d, axis=0) if d else bm
        prod = cm * b_sh
        cbs = [jnp.sum(prod[:, g * D_STATE:(g + 1) * D_STATE], axis=-1, keepdims=True) for g in range(N_GROUPS)]
        cbh = jnp.where(lanes < heads_per_group, cbs[0], cbs[1])
        if d:
            coef = cbh * jnp.exp(acs - pltpu.roll(acs, d, axis=0)) * pltpu.roll(dt, d, axis=0)
        else:
            coef = cbh * dt
        coef = jnp.where(pos >= TOK0 + d, coef, 0.0)
        y = y + expand(coef) * (pltpu.roll(xs, d, axis=0) if d else xs)

    yn = _gate_norm(y, z_ref[...], nw_ref[...]).astype(bf16)
    y_ref[...] = jnp.dot(sel_ref[...], yn, preferred_element_type=f32).astype(bf16)

    xw = xs * expand(dt * dend)
    hi, mid, lo = _split3(expand(cdec))
    pos_w = lax.broadcasted_iota(jnp.int32, (n_rows, D_SSM), 0) % SEQ_TILE
    lhs_scr[...] = jnp.where(pos_w == 0, hi, jnp.where(pos_w == 1, mid, jnp.where(pos_w == 2, lo, xw)))
    ones = jnp.where(pos < 3, 1.0, 0.0)
    for g in range(N_GROUPS):
        rhs_scr[:, 2 * g * D_STATE:(2 * g + 1) * D_STATE] = jnp.where(tok, bm[:, g * D_STATE:(g + 1) * D_STATE], 0.0)
        rhs_scr[:, (2 * g + 1) * D_STATE:(2 * g + 2) * D_STATE] = ones
    for s in range(SSD_BS):
        for g in range(N_GROUPS):
            l8 = lhs_scr[s * SEQ_TILE:(s + 1) * SEQ_TILE, g * half:(g + 1) * half].astype(bf16)
            r8 = rhs_scr[s * SEQ_TILE:(s + 1) * SEQ_TILE, 2 * g * D_STATE:(2 * g + 2) * D_STATE].astype(bf16)
            res = lax.dot_general(l8, r8, (((0,), (0,)), ((), ())), preferred_element_type=f32)
            hnew_ref[s, g * half:(g + 1) * half, :] = (
                h0_ref[s, g * half:(g + 1) * half, :] * res[:, D_STATE:] + res[:, :D_STATE])


def _ssd_sample(pz8, pxbc8, pdt8, h0, ssm_w, e_mat, sel):
    cw, cb, dtb, alog, dx, nw = ssm_w
    n_seq = h0.shape[0]
    n_rows = SSD_BS * SEQ_TILE
    n_out = SSD_BS * 4
    rows_map = lambda i: (i, 0)
    fixed = lambda i: (0, 0)
    return pl.pallas_call(
        _ssd_sample_kernel,
        grid=(n_seq // SSD_BS,),
        in_specs=[
            pl.BlockSpec((n_rows, D_SSM), rows_map),
            pl.BlockSpec((n_rows, CONV_DIM), rows_map),
            pl.BlockSpec((n_rows, LANES), rows_map),
            pl.BlockSpec((SSD_BS, D_SSM, D_STATE), lambda i: (i, 0, 0)),
            pl.BlockSpec((SSM_CONV, CONV_DIM), fixed),
            pl.BlockSpec((1, CONV_DIM), fixed),
            pl.BlockSpec((1, LANES), fixed),
            pl.BlockSpec((1, LANES), fixed),
            pl.BlockSpec((1, D_SSM), fixed),
            pl.BlockSpec((1, D_SSM), fixed),
            pl.BlockSpec((LANES, D_SSM), fixed),
            pl.BlockSpec((n_out, n_rows), fixed),
        ],
        out_specs=(
            pl.BlockSpec((n_out, D_SSM), rows_map),
            pl.BlockSpec((SSD_BS, D_SSM, D_STATE), lambda i: (i, 0, 0)),
        ),
        out_shape=(
            jax.ShapeDtypeStruct((n_seq * 4, D_SSM), bf16),
            jax.ShapeDtypeStruct((n_seq, D_SSM, D_STATE), f32),
        ),
        scratch_shapes=[
            pltpu.VMEM((n_rows, D_SSM), f32),
            pltpu.VMEM((n_rows, D_SSM), f32),
            pltpu.VMEM((n_rows, 2 * N_GROUPS * D_STATE), f32),
        ],
        compiler_params=_cparams(("arbitrary",)),
        name="ssd_sample",
    )(pz8, pxbc8, pdt8, h0, cw, cb, dtb, alog, dx, nw, e_mat, sel)


N_LT = D_CONF // LANES
HIST = 32
CONF_RC = 64


def _conf_norm_act(acc, g, beta):
    mu = jnp.mean(acc, axis=-1, keepdims=True)
    xc = acc - mu
    var = jnp.mean(xc * xc, axis=-1, keepdims=True)
    return _silu(xc * lax.rsqrt(var + EPS) * g + beta)


def _conf_conv_tile(s_tile, w_ref, p_ref, lt, y_scr, tt):
    off = HIST - (CONF_KERNEL - 1)
    for r0 in range(0, tt, CONF_RC):
        acc = jnp.broadcast_to(p_ref[lt, 0:1, :], (CONF_RC, LANES))
        for k in range(CONF_KERNEL):
            acc = acc + w_ref[lt, k:k + 1, :] * s_tile[r0 + off + k:r0 + off + k + CONF_RC, :]
        y_scr[lt, r0:r0 + CONF_RC, :] = _conf_norm_act(acc, p_ref[lt, 1:2, :], p_ref[lt, 2:3, :])


def _conf_tile_kernel(ca_ref, cg_ref, hist_ref, w_ref, p_ref, y_ref, tail_ref, s_scr, y_scr, *, tt, n_valid):
    t = pl.program_id(1)

    @pl.when(t == 0)
    def _():
        for lt in range(N_LT):
            s_scr[lt, 0:HIST, :] = hist_ref[:, lt * LANES:(lt + 1) * LANES]

    @pl.when(t > 0)
    def _():
        s_scr[:, 0:HIST, :] = s_scr[:, tt:tt + HIST, :]

    glu = ca_ref[...] * _sigmoid(cg_ref[...])
    for lt in range(N_LT):
        s_scr[lt, HIST:HIST + tt, :] = glu[:, lt * LANES:(lt + 1) * LANES]

    def lane_tile(lt, carry):
        _conf_conv_tile(s_scr.at[lt], w_ref, p_ref, lt, y_scr, tt)
        return carry

    lax.fori_loop(0, N_LT, lane_tile, 0)
    for lt in range(N_LT):
        y_ref[:, lt * LANES:(lt + 1) * LANES] = y_scr[lt].astype(bf16)
        tail_ref[:, lt * LANES:(lt + 1) * LANES] = s_scr[lt, n_valid:n_valid + HIST, :]


def _conf_tiles(pc, hist, conf_w, *, n_seq, n_tiles, tt, row_block0, n_valid):
    w3, p3 = conf_w
    in_specs = [
        pl.BlockSpec((tt, D_CONF), lambda b, t: (row_block0 + b * n_tiles + t, 0)),
        pl.BlockSpec((tt, D_CONF), lambda b, t: (row_block0 + b * n_tiles + t, 1)),
        pl.BlockSpec((HIST, D_CONF), lambda b, t: (0, 0)),
        pl.BlockSpec((N_LT, HIST, LANES), lambda b, t: (0, 0, 0)),
        pl.BlockSpec((N_LT, SUBLANES, LANES), lambda b, t: (0, 0, 0)),
    ]
    return pl.pallas_call(
        functools.partial(_conf_tile_kernel, tt=tt, n_valid=n_valid),
        grid=(n_seq, n_tiles),
        in_specs=in_specs,
        out_specs=(
            pl.BlockSpec((tt, D_CONF), lambda b, t: (b * n_tiles + t, 0)),
            pl.BlockSpec((None, HIST, D_CONF), lambda b, t: (b, 0, 0)),
        ),
        out_shape=(
            jax.ShapeDtypeStruct((n_seq * n_tiles * tt, D_CONF), bf16),
            jax.ShapeDtypeStruct((n_seq, HIST, D_CONF), f32),
        ),
        scratch_shapes=[
            pltpu.VMEM((N_LT, HIST + tt, LANES), f32),
            pltpu.VMEM((N_LT, tt, LANES), f32),
        ],
        compiler_params=_cparams(("arbitrary", "arbitrary")),
        name="conf_tiles",
    )(pc, pc, hist, w3, p3)


CONF_BS = 16


def _conf_sample_kernel(ca_ref, cg_ref, buf_ref, w_ref, p_ref, y_ref, newbuf_ref):
    n_hist = CONF_KERNEL - 1
    n_tok = ca_ref.shape[0]
    glu = [ca_ref[t] * _sigmoid(cg_ref[t]) for t in range(n_tok)]
    win = lambda j: buf_ref[j] if j < n_hist else glu[j - n_hist]
    for j in range(n_hist):
        newbuf_ref[j] = win(j + n_tok)
    for t in range(n_tok):
        acc = jnp.broadcast_to(p_ref[0:1, :], (CONF_BS, D_CONF))
        for k in range(CONF_KERNEL):
            acc = acc + w_ref[k:k + 1, :] * win(t + k)
        for lt in range(N_LT):
            sl = slice(lt * LANES, (lt + 1) * LANES)
            y_ref[t, :, sl] = _conf_norm_act(acc[:, sl], p_ref[1:2, sl], p_ref[2:3, sl]).astype(bf16)


def _conf_sample(pc_t, buf_t, w2, p2, *, n_tok, tok_block0):
    n_seq = pc_t.shape[1]
    n_hist = CONF_KERNEL - 1
    seq_blocks = lambda col: (lambda i: (0, i, col))
    return pl.pallas_call(
        _conf_sample_kernel,
        grid=(n_seq // CONF_BS,),
        in_specs=[
            pl.BlockSpec((n_tok, CONF_BS, D_CONF), lambda i: (tok_block0, i, 0)),
            pl.BlockSpec((n_tok, CONF_BS, D_CONF), lambda i: (tok_block0, i, 1)),
            pl.BlockSpec((n_hist, CONF_BS, D_CONF), seq_blocks(0)),
            pl.BlockSpec((CONF_KERNEL, D_CONF), lambda i: (0, 0)),
            pl.BlockSpec((3, D_CONF), lambda i: (0, 0)),
        ],
        out_specs=(
            pl.BlockSpec((n_tok, CONF_BS, D_CONF), seq_blocks(0)),
            pl.BlockSpec((n_hist, CONF_BS, D_CONF), seq_blocks(0)),
        ),
        out_shape=(
            jax.ShapeDtypeStruct((n_tok, n_seq, D_CONF), bf16),
            jax.ShapeDtypeStruct((n_hist, n_seq, D_CONF), f32),
        ),
        compiler_params=_cparams(("arbitrary",)),
        name="conf_sample",
    )(pc_t, pc_t, buf_t, w2, p2)


def _seq_tiles(rows, n_seq, n_tok):
    w = rows.shape[-1]
    return jnp.pad(rows.reshape(n_seq, n_tok, w), ((0, 0), (SEQ_TILE - n_tok, 0), (0, 0))).reshape(n_seq * SEQ_TILE, w)


def kernel(x_prompt, x_sample, state_ssm, state_ssm_conv, state_conf_conv, meta_tokens, ffn1_norm, ffn1_w_gate, ffn1_w_up, ffn1_w_down, mix_norm, w_in, ssm_conv_w, ssm_conv_b, ssm_dt_bias, ssm_A_log, ssm_D, ssm_norm, conf_conv_w, conf_conv_b, conf_ln_g, conf_ln_b, w_out, ffn2_norm, ffn2_w_gate, ffn2_w_up, ffn2_w_down, final_norm):
    n_pb, n_pt = x_prompt.shape[0], x_prompt.shape[1]
    n_sb, n_st = x_sample.shape[0], x_sample.shape[1]
    assert (n_pb * n_pt, n_sb * n_st, n_st) == (N_PROMPT_ROWS, N_SAMPLE_ROWS, 4) and n_pt % CHUNK == 0

    row = lambda v: v.reshape(1, -1)
    swap01 = lambda v: jnp.transpose(v, (1, 0, 2))
    x_rows = (x_prompt.reshape(N_PROMPT_ROWS, D_MODEL), swap01(x_sample).reshape(N_SAMPLE_ROWS, D_MODEL), meta_tokens)
    h, u, w_t = _ffn(x_rows, row(ffn1_norm[0]), _to_bf16(ffn1_w_gate, 256), _to_bf16(ffn1_w_up, 256),
                     _to_bf16(ffn1_w_down, 704), row(mix_norm[0]), side=(jnp.transpose(w_in[0]), 96))

    c_in0 = D_SSM + CONV_DIM
    pzx, wg2 = _matmul_nt(u, w_t, row0=0, n=c_in0, tn=1536, tm=TM_PROJ, side=(ffn2_w_gate, 128))
    pdt = _matmul_nt(u, w_t, row0=c_in0, n=LANES, tn=LANES)
    pc, wu2 = _matmul_nt(u, w_t, row0=c_in0 + N_HEADS, n=2 * D_CONF, tn=1024, tm=TM_PROJ, side=(ffn2_w_up, 64))

    pad_heads = lambda v: jnp.pad(v.reshape(1, N_HEADS), ((0, 0), (0, LANES - N_HEADS)))
    ssm_w = (ssm_conv_w[0], row(ssm_conv_b[0]), pad_heads(ssm_dt_bias[0]), pad_heads(ssm_A_log[0]),
             row(jnp.repeat(ssm_D[0], HEAD_DIM)), row(ssm_norm[0]))
    tri = jnp.tril(jnp.ones((CHUNK, CHUNK), f32))
    e_mat = (jnp.arange(LANES)[:, None] == (jnp.arange(D_SSM)[None, :] // HEAD_DIM)).astype(bf16)
    lane_major = lambda v: v.reshape(v.shape[0], N_LT, LANES).transpose(1, 0, 2)
    conf_w = (lane_major(jnp.pad(conf_conv_w[0], ((0, HIST - CONF_KERNEL), (0, 0)))),
              lane_major(jnp.pad(jnp.stack([conf_conv_b[0], conf_ln_g[0], conf_ln_b[0]]), ((0, SUBLANES - 3), (0, 0)))))
    out_rows = jnp.arange(SSD_BS * 4)
    sel = ((out_rows // 4) * SEQ_TILE + TOK0 + out_rows % 4)[:, None] == jnp.arange(SSD_BS * SEQ_TILE)[None, :]
    sel = sel.astype(bf16)

    meta_block = ROW_META // CHUNK
    ys_m, h_meta, tail_meta = _ssd_chunks(
        pzx, pdt, jnp.zeros((SUBLANES, CONV_DIM), f32), jnp.zeros((D_STATE, D_SSM), f32), ssm_w, (tri, e_mat),
        n_seq=1, n_chunks=1, row_block0=meta_block, n_valid=N_META, transpose_out=False)
    yc_m, hist_meta = _conf_tiles(pc, jnp.zeros((HIST, D_CONF), f32), conf_w,
                                  n_seq=1, n_tiles=1, tt=CHUNK, row_block0=meta_block, n_valid=N_META)

    ys_p, p_ssm, p_tail, wo = _ssd_chunks(
        pzx, pdt, tail_meta[0], h_meta[0], ssm_w, (tri, e_mat),
        n_seq=n_pb, n_chunks=n_pt // CHUNK, row_block0=0, n_valid=CHUNK, transpose_out=True, side=(w_out, 64))
    conf_tt = 256
    yc_p, p_hist = _conf_tiles(pc, hist_meta[0], conf_w,
                               n_seq=n_pb, n_tiles=n_pt // conf_tt, tt=conf_tt, row_block0=0, n_valid=conf_tt)

    smp = slice(ROW_SAMPLE, ROW_SAMPLE + N_SAMPLE_ROWS)
    seq_major = lambda v: swap01(v.reshape(n_st, n_sb, v.shape[-1]))
    pzx_s = seq_major(pzx[smp])
    pxbc_s = pzx_s[:, :, D_SSM:]
    pxbc8 = jnp.concatenate([jnp.zeros((n_sb, 1, CONV_DIM), f32), state_ssm_conv[0], pxbc_s], axis=1)
    ys_s, s_ssm = _ssd_sample(_seq_tiles(pzx_s[:, :, :D_SSM], n_sb, n_st), pxbc8.reshape(n_sb * SEQ_TILE, CONV_DIM),
                              _seq_tiles(seq_major(pdt[smp]), n_sb, n_st), state_ssm[0].reshape(n_sb, D_SSM, D_STATE),
                              ssm_w, e_mat, sel)
    ys_s = swap01(ys_s.reshape(n_sb, n_st, D_SSM)).reshape(N_SAMPLE_ROWS, D_SSM)
    yc_t, cconv_t = _conf_sample(pc.reshape(N_ROWS // n_sb, n_sb, 2 * D_CONF), swap01(state_conf_conv[0]),
                                 conf_conv_w[0], jnp.stack([conf_conv_b[0], conf_ln_g[0], conf_ln_b[0]]),
                                 n_tok=n_st, tok_block0=ROW_SAMPLE // n_sb // n_st)
    yc_s = yc_t.reshape(N_SAMPLE_ROWS, D_CONF)
    s_cconv = swap01(cconv_t)

    h2, wd2 = _outproj(h, (ys_p, ys_s, ys_m), (yc_p, yc_s, yc_m), wo, tm=TM_PROJ, tn=512, side=(ffn2_w_down, 256))
    y_prompt, y_sample = _ffn((h2,), row(ffn2_norm[0]), wg2, wu2, wd2, row(final_norm),
                              dst_rows=(N_PROMPT_ROWS, N_SAMPLE_ROWS))
    n_hist = CONF_KERNEL - 1
    return (
        y_prompt.reshape(n_pb, n_pt, D_MODEL), swap01(y_sample.reshape(n_st, n_sb, D_MODEL)),
        p_ssm.reshape(1, n_pb, N_HEADS, HEAD_DIM, D_STATE),
        p_tail[:, SUBLANES - (SSM_CONV - 1):, :][None],
        p_hist[:, HIST - n_hist:, :][None],
        s_ssm.reshape(1, n_sb, N_HEADS, HEAD_DIM, D_STATE),
        pxbc_s[:, n_st - (SSM_CONV - 1):, :][None],
        s_cconv[None],
    )
```

```python
import functools

import jax
import jax.numpy as jnp
from jax import lax
from jax.experimental import pallas as pl
from jax.experimental.pallas import tpu as pltpu

f32 = jnp.float32
bf16 = jnp.bfloat16

D_MODEL = 2048
N_META = 16
D_SSM = 2048
D_CONF = 2048
HEAD_DIM = 64
N_HEADS = 32
N_GROUPS = 2
D_STATE = 128
SSM_CONV = 4
CONV_DIM = D_SSM + 2 * N_GROUPS * D_STATE
CONF_KERNEL = 31
CONF_GROUP = 128
D_FF = 5632
EPS = 1e-5
LOG2E = 1.4426950408889634

LANES = 128
SUBLANES = 8
CHUNK = 128
VMEM_LIMIT = 60 * 1024 * 1024

N_PROMPT_ROWS = 4 * 2048
N_SAMPLE_ROWS = 128 * 4
ROW_SAMPLE = N_PROMPT_ROWS
ROW_META = N_PROMPT_ROWS + N_SAMPLE_ROWS
N_ROWS = ROW_META + CHUNK
TM = 736
TM_PROJ = 1104
TF = 512


def _cparams(sem):
    return pltpu.CompilerParams(dimension_semantics=sem, vmem_limit_bytes=VMEM_LIMIT)


def _sigmoid(x):
    return 1.0 / (1.0 + jnp.exp(-x))


def _silu(x):
    return x * _sigmoid(x)


def _rms(x, g):
    return x * lax.rsqrt(jnp.mean(x * x, axis=-1, keepdims=True) + EPS) * g


def _softplus(x):
    return jnp.maximum(x, 0.0) + jnp.log1p(jnp.exp(-jnp.abs(x)))


def _split3(v):
    hi = v.astype(bf16).astype(f32)
    r1 = v - hi
    mid = r1.astype(bf16).astype(f32)
    return hi, mid, r1 - mid


def _row_segments(src_rows, n_total, tm):
    bounds, start = [], 0
    for k, n in enumerate(src_rows):
        bounds.append((k, start, n))
        start += n
    if start < n_total:
        bounds.append((None, start, n_total - start))
    table = []
    for t in range(n_total // tm):
        lo, hi = t * tm, (t + 1) * tm
        segs = []
        for k, s, n in bounds:
            a, b = max(lo, s), min(hi, s + n)
            if a < b:
                segs.append((k, a - s, a - lo, b - a))
        table.append(segs)
    return table


def _load_tile(i, seg_table, groups, sem):
    tm = groups[0][2].shape[0]
    special = [t for t, segs in enumerate(seg_table) if segs != [(0, t * tm, 0, tm)]]
    first = special[0] if special else len(seg_table)
    assert special == list(range(first, len(seg_table)))

    @pl.when(i < first)
    def _():
        for main_ref, _, dst_ref in groups:
            dst_ref[...] = main_ref[...]

    for t in special:
        @pl.when(i == t)
        def _(segs=seg_table[t]):
            copies = []
            for main_ref, tail_srcs, dst_ref in groups:
                for k, r0, d0, n in segs:
                    if k is None:
                        dst_ref[d0:d0 + n, :] = jnp.zeros((n, dst_ref.shape[1]), dst_ref.dtype)
                    elif k == 0:
                        dst_ref[d0:d0 + n, :] = main_ref[d0:d0 + n, :]
                    else:
                        cp = pltpu.make_async_copy(tail_srcs[k - 1].at[pl.ds(r0, n)], dst_ref.at[pl.ds(d0, n)], sem)
                        cp.start()
                        copies.append(cp)
            for cp in copies:
                cp.wait()


def _side_cast(w, block_rows, grid):
    r, c = w.shape[-2:]
    n_blocks = r // block_rows
    assert r % block_rows == 0 and n_blocks <= grid[0] * grid[1]
    blk = lambda i, j: jnp.minimum(i * grid[1] + j, n_blocks - 1)
    if w.ndim == 3:
        in_spec = pl.BlockSpec((None, block_rows, c), lambda i, j: (0, blk(i, j), 0))
    else:
        in_spec = pl.BlockSpec((block_rows, c), lambda i, j: (blk(i, j), 0))
    return in_spec, pl.BlockSpec((block_rows, c), lambda i, j: (blk(i, j), 0)), jax.ShapeDtypeStruct((r, c), bf16)


def _run_side_cast(side_refs):
    for src_ref, dst_ref in side_refs:
        dst_ref[...] = src_ref[...].astype(bf16)


def _drain_rows(i, out_table, dsts, acc_scr, sems):
    def copies(t):
        return [pltpu.make_async_copy(acc_scr.at[t % 2, pl.ds(s0, n)], dsts[k].at[pl.ds(r0, n)], sems.at[t % 2])
                for k, r0, s0, n in out_table[t] if k is not None]

    n_tiles = len(out_table)
    for t in range(n_tiles):
        @pl.when(i == t)
        def _(t=t):
            if t > 0:
                for cp in copies(t - 1):
                    cp.wait()
            for cp in copies(t):
                cp.start()
            if t == n_tiles - 1:
                for cp in copies(t):
                    cp.wait()


def _ffn_kernel(*refs, seg_table, out_table, n_src, n_side):
    x_ref, tail_srcs = refs[0], refs[1:n_src]
    nw_ref, wg_ref, wu_ref, wd_ref, pw_ref = refs[n_src:n_src + 5]
    pos = n_src + 5
    side_in, pos = refs[pos:pos + n_side], pos + n_side
    i = pl.program_id(0)
    f = pl.program_id(1)
    if out_table is None:
        (acc, yn_ref), pos = refs[pos:pos + 2], pos + 2
        side_out, pos = refs[pos:pos + n_side], pos + n_side
        xn_ref, sem = refs[pos:]
    else:
        n_dst = 1 + max(k for segs in out_table for k, _, _, _ in segs if k is not None)
        dsts, pos = refs[pos:pos + n_dst], pos + n_dst
        side_out, pos = refs[pos:pos + n_side], pos + n_side
        acc_scr, xn_ref, sem, out_sems = refs[pos:]
        acc = acc_scr.at[i % 2]
    _run_side_cast(zip(side_in, side_out))

    @pl.when(f == 0)
    def _():
        _load_tile(i, seg_table, [(x_ref, tail_srcs, acc)], sem)
        xn_ref[...] = _rms(acc[...], nw_ref[...]).astype(bf16)

    xn = xn_ref[...]
    g = jnp.dot(xn, wg_ref[...], preferred_element_type=f32)
    u = jnp.dot(xn, wu_ref[...], preferred_element_type=f32)
    a = (_silu(g) * u).astype(bf16)
    acc[...] += 0.5 * jnp.dot(a, wd_ref[...], preferred_element_type=f32)

    @pl.when(f == pl.num_programs(1) - 1)
    def _():
        if out_table is None:
            yn_ref[...] = _rms(acc[...], pw_ref[...]).astype(yn_ref.dtype)
        else:
            acc[...] = _rms(acc[...], pw_ref[...])
            _drain_rows(i, out_table, dsts, acc_scr, out_sems)


def _ffn(srcs, norm_w, wg, wu, wd, post_w, *, dst_rows=None, side=None, tf=TF):
    seg_table = _row_segments([s.shape[0] for s in srcs], N_ROWS, TM)
    grid = (N_ROWS // TM, D_FF // tf)
    side_specs = [_side_cast(*side, grid)] if side else []
    row_spec = pl.BlockSpec((TM, D_MODEL), lambda i, f: (i, 0))
    vec_spec = pl.BlockSpec((1, D_MODEL), lambda i, f: (0, 0))
    scratch = [pltpu.VMEM((TM, D_MODEL), bf16), pltpu.SemaphoreType.DMA(())]
    if dst_rows is None:
        out_table = None
        out_shape = (jax.ShapeDtypeStruct((N_ROWS, D_MODEL), f32), jax.ShapeDtypeStruct((N_ROWS, D_MODEL), bf16))
        out_specs = (row_spec, row_spec)
    else:
        out_table = _row_segments(dst_rows, N_ROWS, TM)
        out_shape = tuple(jax.ShapeDtypeStruct((n, D_MODEL), f32) for n in dst_rows)
        out_specs = tuple(pl.BlockSpec(memory_space=pl.ANY) for _ in dst_rows)
        scratch = [pltpu.VMEM((2, TM, D_MODEL), f32)] + scratch + [pltpu.SemaphoreType.DMA((2,))]
    return pl.pallas_call(
        functools.partial(_ffn_kernel, seg_table=seg_table, out_table=out_table, n_src=len(srcs),
                          n_side=len(side_specs)),
        grid=grid,
        in_specs=[row_spec] + [pl.BlockSpec(memory_space=pl.ANY)] * (len(srcs) - 1) + [
            vec_spec,
            pl.BlockSpec((D_MODEL, tf), lambda i, f: (0, f)),
            pl.BlockSpec((D_MODEL, tf), lambda i, f: (0, f)),
            pl.BlockSpec((tf, D_MODEL), lambda i, f: (f, 0)),
            vec_spec,
        ] + [s[0] for s in side_specs],
        out_specs=tuple(out_specs) + tuple(s[1] for s in side_specs),
        out_shape=tuple(out_shape) + tuple(s[2] for s in side_specs),
        scratch_shapes=scratch,
        compiler_params=_cparams(("arbitrary", "arbitrary")),
        name="ffn",
    )(*srcs, norm_w, wg, wu, wd, post_w, *([side[0]] if side else []))


def _cast_kernel(x_ref, o_ref):
    o_ref[...] = x_ref[...].astype(bf16)


def _to_bf16(w, block_rows):
    r, c = w.shape[-2:]
    if w.ndim == 3:
        in_spec = pl.BlockSpec((None, block_rows, c), lambda i: (0, i, 0))
    else:
        in_spec = pl.BlockSpec((block_rows, c), lambda i: (i, 0))
    return pl.pallas_call(
        _cast_kernel,
        grid=(r // block_rows,),
        in_specs=[in_spec],
        out_specs=pl.BlockSpec((block_rows, c), lambda i: (i, 0)),
        out_shape=jax.ShapeDtypeStruct((r, c), bf16),
        compiler_params=_cparams(("arbitrary",)),
        name="cast_bf16",
    )(w)


def _matmul_nt_kernel(a_ref, w_ref, *rest):
    n_side = len(rest) // 2
    o_ref = rest[n_side]
    _run_side_cast(zip(rest[:n_side], rest[n_side + 1:]))
    o_ref[...] = lax.dot_general(a_ref[...], w_ref[...], (((1,), (1,)), ((), ())), preferred_element_type=f32)


def _matmul_nt(a, w_t, *, row0, n, tn, tm=TM, a_rows=None, side=None):
    k = a.shape[1]
    a_row0, n_rows = a_rows if a_rows else (0, a.shape[0])
    align = 16
    assert row0 % align == 0 and tn % align == 0 and n % tn == 0 and n_rows % tm == 0 and a_row0 % align == 0
    grid = (n_rows // tm, n // tn)
    side_specs = [_side_cast(*side, grid)] if side else []
    out = pl.pallas_call(
        _matmul_nt_kernel,
        grid=grid,
        in_specs=[
            pl.BlockSpec((pl.Element(tm), pl.Element(k)), lambda i, j: (pl.multiple_of(a_row0 + tm * i, align), 0)),
            pl.BlockSpec((pl.Element(tn), pl.Element(k)), lambda i, j: (pl.multiple_of(row0 + tn * j, align), 0)),
        ] + [s[0] for s in side_specs],
        out_specs=(pl.BlockSpec((tm, tn), lambda i, j: (i, j)),) + tuple(s[1] for s in side_specs),
        out_shape=(jax.ShapeDtypeStruct((n_rows, n), f32),) + tuple(s[2] for s in side_specs),
        compiler_params=_cparams(("arbitrary", "arbitrary")),
        name="in_proj",
    )(a, w_t, *([side[0]] if side else []))
    return out if side else out[0]


def _outproj_kernel(*refs, seg_table, n_src, n_side):
    ys_ref, ys_tails = refs[0], refs[1:n_src]
    yc_ref, yc_tails = refs[n_src], refs[n_src + 1:2 * n_src]
    h_ref, w0_ref, w1_ref = refs[2 * n_src:2 * n_src + 3]
    pos = 2 * n_src + 3
    side_in, o_ref, side_out = refs[pos:pos + n_side], refs[pos + n_side], refs[pos + n_side + 1:pos + 2 * n_side + 1]
    ys_scr, yc_scr, sem = refs[pos + 2 * n_side + 1:]
    _run_side_cast(zip(side_in, side_out))

    @pl.when(pl.program_id(1) == 0)
    def _():
        _load_tile(pl.program_id(0), seg_table, [(ys_ref, ys_tails, ys_scr), (yc_ref, yc_tails, yc_scr)], sem)

    acc = jnp.dot(ys_scr[...], w0_ref[...], preferred_element_type=f32)
    acc = acc + jnp.dot(yc_scr[...], w1_ref[...], preferred_element_type=f32)
    o_ref[...] = h_ref[...] + acc


def _outproj(h, ys_srcs, yc_srcs, w_out, *, tm, tn, side=None):
    n_rows = h.shape[0]
    assert n_rows % tm == 0
    src_rows = [min(a.shape[0], b.shape[0]) for a, b in zip(ys_srcs, yc_srcs)]
    seg_table = _row_segments(src_rows, n_rows, tm)
    n_src = len(ys_srcs)
    grid = (n_rows // tm, D_MODEL // tn)
    side_specs = [_side_cast(*side, grid)] if side else []
    y_specs = [pl.BlockSpec((tm, D_SSM), lambda i, j: (i, 0))] + [pl.BlockSpec(memory_space=pl.ANY)] * (n_src - 1)
    out = pl.pallas_call(
        functools.partial(_outproj_kernel, seg_table=seg_table, n_src=n_src, n_side=len(side_specs)),
        grid=grid,
        in_specs=y_specs + y_specs + [
            pl.BlockSpec((tm, tn), lambda i, j: (i, j)),
            pl.BlockSpec((D_SSM, tn), lambda i, j: (0, j)),
            pl.BlockSpec((D_CONF, tn), lambda i, j: (1, j)),
        ] + [s[0] for s in side_specs],
        out_specs=(pl.BlockSpec((tm, tn), lambda i, j: (i, j)),) + tuple(s[1] for s in side_specs),
        out_shape=(jax.ShapeDtypeStruct((n_rows, D_MODEL), f32),) + tuple(s[2] for s in side_specs),
        scratch_shapes=[pltpu.VMEM((tm, D_SSM), bf16), pltpu.VMEM((tm, D_CONF), bf16), pltpu.SemaphoreType.DMA(())],
        compiler_params=_cparams(("arbitrary", "arbitrary")),
        name="out_proj",
    )(*ys_srcs, *yc_srcs, h, w_out, w_out, *([side[0]] if side else []))
    return out if side else out[0]


def _gate_norm(y, z, nw):
    y = y * _silu(z)
    half = D_SSM // N_GROUPS
    parts = []
    for g in range(N_GROUPS):
        yg = y[:, g * half:(g + 1) * half]
        ms = jnp.mean(yg * yg, axis=-1, keepdims=True)
        parts.append(yg * lax.rsqrt(ms + EPS) * nw[:, g * half:(g + 1) * half])
    return jnp.concatenate(parts, axis=1)


def _ssd_chunk_math(z, xbc, dt_raw, cinit_ref, hinit_ref, cw_ref, cb_ref, dtb_ref, alog_ref,
                    dx_ref, nw_ref, tri_ref, e_ref, y_ref, hout_ref, tail_ref, win_scr, ht_scr, y_scr,
                    *, n_valid, transpose_out):
    c = pl.program_id(1)

    @pl.when(c == 0)
    def _():
        ht_scr[...] = hinit_ref[...]
        win_scr[0:SUBLANES, :] = cinit_ref[...]

    win_scr[SUBLANES:SUBLANES + CHUNK, :] = xbc
    acc = cb_ref[...] + cw_ref[SSM_CONV - 1:SSM_CONV, :] * xbc
    for j in range(1, SSM_CONV):
        acc = acc + cw_ref[SSM_CONV - 1 - j:SSM_CONV - j, :] * win_scr[SUBLANES - j:SUBLANES - j + CHUNK, :]
    tail_ref[...] = win_scr[n_valid:n_valid + SUBLANES, :]
    win_scr[0:SUBLANES, :] = win_scr[CHUNK:CHUNK + SUBLANES, :]
    xc = _silu(acc)
    b_mats = [xc[:, D_SSM + g * D_STATE:D_SSM + (g + 1) * D_STATE] for g in range(N_GROUPS)]
    c_mats = [xc[:, D_SSM + (N_GROUPS + g) * D_STATE:D_SSM + (N_GROUPS + g + 1) * D_STATE] for g in range(N_GROUPS)]

    rows = lax.broadcasted_iota(jnp.int32, (CHUNK, LANES), 0)
    lanes = lax.broadcasted_iota(jnp.int32, (CHUNK, LANES), 1)
    dt = _softplus(dt_raw + dtb_ref[...])
    if n_valid < CHUNK:
        dt = jnp.where(rows < n_valid, dt, 0.0)
    a = dt * (-jnp.exp(alog_ref[...]))
    acs = jnp.dot(tri_ref[...], a, precision=lax.Precision.HIGHEST, preferred_element_type=f32) * LOG2E
    total = acs[CHUNK - 1:CHUNK, :]
    dend = jnp.exp2(total - acs)
    acs_t = acs.T
    dt_t = dt.T
    w_t = (dt * dend).T
    hi, mid, lo = _split3(jnp.exp2(total))
    rid = lax.broadcasted_iota(jnp.int32, (SUBLANES, LANES), 0)
    cd3 = jnp.where(rid == 0, hi, jnp.where(rid == 1, mid, jnp.where(rid == 2, lo, 0.0)))
    cdx = jnp.sum(jnp.dot(cd3.astype(bf16), e_ref[...], preferred_element_type=f32), axis=0, keepdims=True)

    cb_mats, bt_mats = [], []
    for g in range(N_GROUPS):
        cb_mats.append(lax.dot_general(c_mats[g].astype(bf16), b_mats[g].astype(bf16), (((1,), (1,)), ((), ())),
                                       preferred_element_type=f32))
        bt_mats.append(b_mats[g].T)

    lane_lo = lanes < HEAD_DIM
    tril = rows >= lanes
    heads_per_group = N_HEADS // N_GROUPS
    for j in range(N_HEADS // 2):
        sl = slice(j * LANES, (j + 1) * LANES)
        xs_pair = xc[:, sl]
        ht_pair = ht_scr[:, sl]
        y_pair = jnp.zeros((CHUNK, LANES), f32)
        upd = jnp.zeros((D_STATE, LANES), f32)
        for half in range(2):
            h = 2 * j + half
            g = h // heads_per_group
            keep = lane_lo if half == 0 else jnp.logical_not(lane_lo)
            xs_m = jnp.where(keep, xs_pair, 0.0).astype(bf16)
            ht_m = jnp.where(keep, ht_pair, 0.0).astype(bf16)
            acs_col = jnp.broadcast_to(acs[:, h:h + 1], (CHUNK, CHUNK))
            acs_row = jnp.broadcast_to(acs_t[h:h + 1, :], (CHUNK, CHUNK))
            lmat = jnp.exp2(jnp.where(tril, acs_col - acs_row, -jnp.inf))
            m_h = cb_mats[g] * lmat * dt_t[h:h + 1, :]
            ce_h = c_mats[g] * jnp.exp2(acs_col)
            lhs = jnp.concatenate([m_h, ce_h], axis=1).astype(bf16)
            rhs = jnp.concatenate([xs_m, ht_m], axis=0)
            y_pair = y_pair + jnp.dot(lhs, rhs, preferred_element_type=f32)
            bw = (bt_mats[g] * w_t[h:h + 1, :]).astype(bf16)
            upd = upd + jnp.dot(bw, xs_m, preferred_element_type=f32)
        y_scr[:, sl] = y_pair
        ht_scr[:, sl] = ht_pair * cdx[:, sl] + upd

    xs = xc[:, :D_SSM]
    y = y_scr[...] + xs * dx_ref[...]
    y_ref[...] = _gate_norm(y, z(), nw_ref[...]).astype(bf16)

    @pl.when(c == pl.num_programs(1) - 1)
    def _():
        if transpose_out:
            hout_ref[...] = ht_scr[...].T
        else:
            hout_ref[...] = ht_scr[...]


N_SSD_PARAMS = 10


def _ssd_chunk_kernel(z_ref, xs_ref, bc_ref, dt_ref, *rest, n_valid, transpose_out, n_side):
    params, rest = rest[:N_SSD_PARAMS], rest[N_SSD_PARAMS:]
    side_in, outs = rest[:n_side], rest[n_side:n_side + 3]
    side_out, scratch = rest[n_side + 3:2 * n_side + 3], rest[2 * n_side + 3:]
    _run_side_cast(zip(side_in, side_out))
    xbc = jnp.concatenate([xs_ref[...], bc_ref[...]], axis=1)
    _ssd_chunk_math(lambda: z_ref[...], xbc, dt_ref[...], *params, *outs, *scratch,
                    n_valid=n_valid, transpose_out=transpose_out)


def _ssd_chunks(pzx, pdt, cinit, hinit, ssm_w, consts, *, n_seq, n_chunks, row_block0, n_valid, transpose_out,
                side=None):
    cw, cb, dtb, alog, dx, nw = ssm_w
    tri, e_mat = consts
    bc_dim = CONV_DIM - D_SSM
    side_specs = [_side_cast(*side, (n_seq, n_chunks))] if side else []
    rows_map = lambda b, c: (row_block0 + b * n_chunks + c, 0)
    fixed = lambda b, c: (0, 0)
    in_specs = [
        pl.BlockSpec((CHUNK, D_SSM), rows_map),
        pl.BlockSpec((CHUNK, D_SSM), lambda b, c: (row_block0 + b * n_chunks + c, 1)),
        pl.BlockSpec((CHUNK, bc_dim), lambda b, c: (row_block0 + b * n_chunks + c, 2 * D_SSM // bc_dim)),
        pl.BlockSpec((CHUNK, LANES), rows_map),
        pl.BlockSpec((SUBLANES, CONV_DIM), fixed),
        pl.BlockSpec((D_STATE, D_SSM), fixed),
        pl.BlockSpec((SSM_CONV, CONV_DIM), fixed),
        pl.BlockSpec((1, CONV_DIM), fixed),
        pl.BlockSpec((1, LANES), fixed),
        pl.BlockSpec((1, LANES), fixed),
        pl.BlockSpec((1, D_SSM), fixed),
        pl.BlockSpec((1, D_SSM), fixed),
        pl.BlockSpec((CHUNK, CHUNK), fixed),
        pl.BlockSpec((LANES, D_SSM), fixed),
    ]
    args = [pzx, pzx, pzx, pdt, cinit, hinit, cw, cb, dtb, alog, dx, nw, tri, e_mat] + ([side[0]] if side else [])
    hshape = (n_seq, D_SSM, D_STATE) if transpose_out else (n_seq, D_STATE, D_SSM)
    return pl.pallas_call(
        functools.partial(_ssd_chunk_kernel, n_valid=n_valid, transpose_out=transpose_out, n_side=len(side_specs)),
        grid=(n_seq, n_chunks),
        in_specs=in_specs + [s[0] for s in side_specs],
        out_specs=(
            pl.BlockSpec((CHUNK, D_SSM), lambda b, c: (b * n_chunks + c, 0)),
            pl.BlockSpec((None,) + hshape[1:], lambda b, c: (b, 0, 0)),
            pl.BlockSpec((None, SUBLANES, CONV_DIM), lambda b, c: (b, 0, 0)),
        ) + tuple(s[1] for s in side_specs),
        out_shape=(
            jax.ShapeDtypeStruct((n_seq * n_chunks * CHUNK, D_SSM), bf16),
            jax.ShapeDtypeStruct(hshape, f32),
            jax.ShapeDtypeStruct((n_seq, SUBLANES, CONV_DIM), f32),
        ) + tuple(s[2] for s in side_specs),
        scratch_shapes=[
            pltpu.VMEM((SUBLANES + CHUNK, CONV_DIM), f32),
            pltpu.VMEM((D_STATE, D_SSM), f32),
            pltpu.VMEM((CHUNK, D_SSM), f32),
        ],
        compiler_params=_cparams(("arbitrary", "arbitrary")),
        name="ssd_chunks",
    )(*args)


SEQ_TILE = SUBLANES
TOK0 = SEQ_TILE - 4
SSD_BS = 8


def _ssd_sample_kernel(z_ref, xbc_ref, dt_ref, h0_ref, cw_ref, cb_ref, dtb_ref, alog_ref, dx_ref, nw_ref,
                       e_ref, sel_ref, y_ref, hnew_ref, yoff_scr, lhs_scr, rhs_scr):
    n_rows = SSD_BS * SEQ_TILE
    half = D_SSM // N_GROUPS
    x8 = xbc_ref[...]
    acc = cb_ref[...] + cw_ref[SSM_CONV - 1:SSM_CONV, :] * x8
    for j in range(1, SSM_CONV):
        acc = acc + cw_ref[SSM_CONV - 1 - j:SSM_CONV - j, :] * pltpu.roll(x8, j, axis=0)
    xc = _silu(acc)
    xs = xc[:, :D_SSM]
    bm = xc[:, D_SSM:D_SSM + N_GROUPS * D_STATE]
    cm = xc[:, D_SSM + N_GROUPS * D_STATE:]

    pos = lax.broadcasted_iota(jnp.int32, (n_rows, LANES), 0) % SEQ_TILE
    lanes = lax.broadcasted_iota(jnp.int32, (n_rows, LANES), 1)
    tok = pos >= TOK0
    dt = jnp.where(tok, _softplus(dt_ref[...] + dtb_ref[...]), 0.0)
    a = dt * (-jnp.exp(alog_ref[...]))
    acs = a
    for d in range(1, 4):
        acs = acs + pltpu.roll(a, d, axis=0)
    tot = jnp.sum(a.reshape(SSD_BS, SEQ_TILE, LANES), axis=1, keepdims=True)
    tot = jnp.broadcast_to(tot, (SSD_BS, SEQ_TILE, LANES)).reshape(n_rows, LANES)
    dend = jnp.where(tok, jnp.exp(tot - acs), 0.0)
    eacs = jnp.where(tok, jnp.exp(acs), 0.0)
    cdec = jnp.exp(tot)

    def expand(v):
        hi, mid, lo = _split3(v)
        stack = jnp.concatenate([hi, mid, lo], axis=0).astype(bf16)
        r = jnp.dot(stack, e_ref[...], preferred_element_type=f32)
        return r[0:n_rows] + r[n_rows:2 * n_rows] + r[2 * n_rows:3 * n_rows]

    heads_per_group = N_HEADS // N_GROUPS
    y = expand(eacs)
    for s in range(SSD_BS):
        for g in range(N_GROUPS):
            c8 = cm[s * SEQ_TILE:(s + 1) * SEQ_TILE, g * D_STATE:(g + 1) * D_STATE].astype(bf16)
            hg = h0_ref[s, g * half:(g + 1) * half, :].astype(bf16)
            yoff_scr[s * SEQ_TILE:(s + 1) * SEQ_TILE, g * half:(g + 1) * half] = lax.dot_general(
                c8, hg, (((1,), (1,)), ((), ())), preferred_element_type=f32)
    y = y * yoff_scr[...] + xs * dx_ref[...]
    for d in range(4):
        b_sh = pltpu.roll(bm, d, axis=0) if d else bm
        prod = cm * b_sh
        cbs = [jnp.sum(prod[:, g * D_STATE:(g + 1) * D_STATE], axis=-1, keepdims=True) for g in range(N_GROUPS)]
        cbh = jnp.where(lanes < heads_per_group, cbs[0], cbs[1])
        if d:
            coef = cbh * jnp.exp(acs - pltpu.roll(acs, d, axis=0)) * pltpu.roll(dt, d, axis=0)
        else:
            coef = cbh * dt
        coef = jnp.where(pos >= TOK0 + d, coef, 0.0)
        y = y + expand(coef) * (pltpu.roll(xs, d, axis=0) if d else xs)

    yn = _gate_norm(y, z_ref[...], nw_ref[...]).astype(bf16)
    y_ref[...] = jnp.dot(sel_ref[...], yn, preferred_element_type=f32).astype(bf16)

    xw = xs * expand(dt * dend)
    hi, mid, lo = _split3(expand(cdec))
    pos_w = lax.broadcasted_iota(jnp.int32, (n_rows, D_SSM), 0) % SEQ_TILE
    lhs_scr[...] = jnp.where(pos_w == 0, hi, jnp.where(pos_w == 1, mid, jnp.where(pos_w == 2, lo, xw)))
    ones = jnp.where(pos < 3, 1.0, 0.0)
    for g in range(N_GROUPS):
        rhs_scr[:, 2 * g * D_STATE:(2 * g + 1) * D_STATE] = jnp.where(tok, bm[:, g * D_STATE:(g + 1) * D_STATE], 0.0)
        rhs_scr[:, (2 * g + 1) * D_STATE:(2 * g + 2) * D_STATE] = ones
    for s in range(SSD_BS):
        for g in range(N_GROUPS):
            l8 = lhs_scr[s * SEQ_TILE:(s + 1) * SEQ_TILE, g * half:(g + 1) * half].astype(bf16)
            r8 = rhs_scr[s * SEQ_TILE:(s + 1) * SEQ_TILE, 2 * g * D_STATE:(2 * g + 2) * D_STATE].astype(bf16)
            res = lax.dot_general(l8, r8, (((0,), (0,)), ((), ())), preferred_element_type=f32)
            hnew_ref[s, g * half:(g + 1) * half, :] = (
                h0_ref[s, g * half:(g + 1) * half, :] * res[:, D_STATE:] + res[:, :D_STATE])


def _ssd_sample(pz8, pxbc8, pdt8, h0, ssm_w, e_mat, sel):
    cw, cb, dtb, alog, dx, nw = ssm_w
    n_seq = h0.shape[0]
    n_rows = SSD_BS * SEQ_TILE
    n_out = SSD_BS * 4
    rows_map = lambda i: (i, 0)
    fixed = lambda i: (0, 0)
    return pl.pallas_call(
        _ssd_sample_kernel,
        grid=(n_seq // SSD_BS,),
        in_specs=[
            pl.BlockSpec((n_rows, D_SSM), rows_map),
            pl.BlockSpec((n_rows, CONV_DIM), rows_map),
            pl.BlockSpec((n_rows, LANES), rows_map),
            pl.BlockSpec((SSD_BS, D_SSM, D_STATE), lambda i: (i, 0, 0)),
            pl.BlockSpec((SSM_CONV, CONV_DIM), fixed),
            pl.BlockSpec((1, CONV_DIM), fixed),
            pl.BlockSpec((1, LANES), fixed),
            pl.BlockSpec((1, LANES), fixed),
            pl.BlockSpec((1, D_SSM), fixed),
            pl.BlockSpec((1, D_SSM), fixed),
            pl.BlockSpec((LANES, D_SSM), fixed),
            pl.BlockSpec((n_out, n_rows), fixed),
        ],
        out_specs=(
            pl.BlockSpec((n_out, D_SSM), rows_map),
            pl.BlockSpec((SSD_BS, D_SSM, D_STATE), lambda i: (i, 0, 0)),
        ),
        out_shape=(
            jax.ShapeDtypeStruct((n_seq * 4, D_SSM), bf16),
            jax.ShapeDtypeStruct((n_seq, D_SSM, D_STATE), f32),
        ),
        scratch_shapes=[
            pltpu.VMEM((n_rows, D_SSM), f32),
            pltpu.VMEM((n_rows, D_SSM), f32),
            pltpu.VMEM((n_rows, 2 * N_GROUPS * D_STATE), f32),
        ],
        compiler_params=_cparams(("arbitrary",)),
        name="ssd_sample",
    )(pz8, pxbc8, pdt8, h0, cw, cb, dtb, alog, dx, nw, e_mat, sel)


N_LT = D_CONF // LANES
HIST = 32
CONF_RC = 64


def _conf_norm_act(acc, g, beta):
    mu = jnp.mean(acc, axis=-1, keepdims=True)
    xc = acc - mu
    var = jnp.mean(xc * xc, axis=-1, keepdims=True)
    return _silu(xc * lax.rsqrt(var + EPS) * g + beta)


def _conf_conv_tile(s_tile, w_ref, p_ref, lt, y_tile, tt):
    off = HIST - (CONF_KERNEL - 1)
    for r0 in range(0, tt, CONF_RC):
        acc = jnp.broadcast_to(p_ref[lt, 0:1, :], (CONF_RC, LANES))
        for k in range(CONF_KERNEL):
            acc = acc + w_ref[lt, k:k + 1, :] * s_tile[r0 + off + k:r0 + off + k + CONF_RC, :]
        y_tile[r0:r0 + CONF_RC, :] = _conf_norm_act(acc, p_ref[lt, 1:2, :], p_ref[lt, 2:3, :])


def _conf_tile_kernel(ca_ref, cg_ref, hist_ref, w_ref, p_ref, y_ref, tail_ref, s_scr, y_scr, *, tt, n_valid):
    t = pl.program_id(1)

    @pl.when(t == 0)
    def _():
        for lt in range(N_LT):
            s_scr[lt, 0:HIST, :] = hist_ref[:, lt * LANES:(lt + 1) * LANES]

    @pl.when(t > 0)
    def _():
        s_scr[:, 0:HIST, :] = s_scr[:, tt:tt + HIST, :]

    glu = ca_ref[...] * _sigmoid(cg_ref[...])
    for lt in range(N_LT):
        s_scr[lt, HIST:HIST + tt, :] = glu[:, lt * LANES:(lt + 1) * LANES]

    def lane_tile(lt, carry):
        _conf_conv_tile(s_scr.at[lt], w_ref, p_ref, lt, y_scr.at[lt], tt)
        return carry

    lax.fori_loop(0, N_LT, lane_tile, 0)
    for lt in range(N_LT):
        y_ref[:, lt * LANES:(lt + 1) * LANES] = y_scr[lt].astype(bf16)
        tail_ref[:, lt * LANES:(lt + 1) * LANES] = s_scr[lt, n_valid:n_valid + HIST, :]


def _conf_tiles(pc, hist, conf_w, *, n_seq, n_tiles, tt, row_block0, n_valid):
    w3, p3 = conf_w
    in_specs = [
        pl.BlockSpec((tt, D_CONF), lambda b, t: (row_block0 + b * n_tiles + t, 0)),
        pl.BlockSpec((tt, D_CONF), lambda b, t: (row_block0 + b * n_tiles + t, 1)),
        pl.BlockSpec((HIST, D_CONF), lambda b, t: (0, 0)),
        pl.BlockSpec((N_LT, HIST, LANES), lambda b, t: (0, 0, 0)),
        pl.BlockSpec((N_LT, SUBLANES, LANES), lambda b, t: (0, 0, 0)),
    ]
    return pl.pallas_call(
        functools.partial(_conf_tile_kernel, tt=tt, n_valid=n_valid),
        grid=(n_seq, n_tiles),
        in_specs=in_specs,
        out_specs=(
            pl.BlockSpec((tt, D_CONF), lambda b, t: (b * n_tiles + t, 0)),
            pl.BlockSpec((None, HIST, D_CONF), lambda b, t: (b, 0, 0)),
        ),
        out_shape=(
            jax.ShapeDtypeStruct((n_seq * n_tiles * tt, D_CONF), bf16),
            jax.ShapeDtypeStruct((n_seq, HIST, D_CONF), f32),
        ),
        scratch_shapes=[
            pltpu.VMEM((N_LT, HIST + tt, LANES), f32),
            pltpu.VMEM((N_LT, tt, LANES), f32),
        ],
        compiler_params=_cparams(("arbitrary", "arbitrary")),
        name="conf_tiles",
    )(pc, pc, hist, w3, p3)


CP_ROWS = 1024
CP_COLS = 512


def _conf_proj_kernel(u_ref, wca_ref, wcg_ref, hist_ref, w_ref, p_ref, side_in, y_ref, tail_ref, side_out,
                      s_scr, g_scr, y_scr, *, tiles_per_seq):
    i = pl.program_id(0)
    j = pl.program_id(1)
    n_lt = CP_COLS // LANES
    side_out[...] = side_in[...].astype(bf16)

    @pl.when((i == 0) & (j == 0))
    def _():
        g_scr[...] = jnp.zeros(g_scr.shape, f32)
        s_scr[...] = jnp.zeros(s_scr.shape, f32)

    @pl.when(j == 0)
    def _():
        @pl.when((i - 1) % tiles_per_seq == 0)
        def _():
            for lt in range(N_LT):
                s_scr[lt, 0:HIST, :] = hist_ref[:, lt * LANES:(lt + 1) * LANES]

        @pl.when((i - 1) % tiles_per_seq != 0)
        def _():
            s_scr[:, 0:HIST, :] = s_scr[:, CP_ROWS:CP_ROWS + HIST, :]

        s_scr[:, HIST:HIST + CP_ROWS, :] = g_scr[...]

    u = u_ref[...]
    nt = (((1,), (1,)), ((), ()))
    glu = (lax.dot_general(u, wca_ref[...], nt, preferred_element_type=f32)
           * _sigmoid(lax.dot_general(u, wcg_ref[...], nt, preferred_element_type=f32)))
    for q in range(n_lt):
        lt = j * n_lt + q
        _conf_conv_tile(s_scr.at[lt], w_ref, p_ref, lt, y_scr.at[q], CP_ROWS)
        g_scr[lt] = glu[:, q * LANES:(q + 1) * LANES]
        y_ref[:, q * LANES:(q + 1) * LANES] = y_scr[q].astype(bf16)
        tail_ref[:, q * LANES:(q + 1) * LANES] = s_scr[lt, CP_ROWS:CP_ROWS + HIST, :]


def _conf_proj(u, w_t, ca_row0, hist, conf_w, side, *, n_seq, seq_rows):
    w3, p3 = conf_w
    n_tiles = n_seq * seq_rows // CP_ROWS
    tiles_per_seq = seq_rows // CP_ROWS
    grid = (n_tiles + 1, D_CONF // CP_COLS)
    side_in, side_out, side_shape = _side_cast(*side, grid)
    align = 2 * 16
    w_block = lambda row0: pl.BlockSpec(
        (pl.Element(CP_COLS), pl.Element(D_MODEL)), lambda i, j: (pl.multiple_of(row0 + CP_COLS * j, align), 0))
    slot = lambda i: (i + n_tiles) % (n_tiles + 1)
    return pl.pallas_call(
        functools.partial(_conf_proj_kernel, tiles_per_seq=tiles_per_seq),
        grid=grid,
        in_specs=[
            pl.BlockSpec((CP_ROWS, D_MODEL), lambda i, j: (jnp.minimum(i, n_tiles - 1), 0)),
            w_block(ca_row0),
            w_block(ca_row0 + D_CONF),
            pl.BlockSpec((HIST, D_CONF), lambda i, j: (0, 0)),
            pl.BlockSpec((N_LT, HIST, LANES), lambda i, j: (0, 0, 0)),
            pl.BlockSpec((N_LT, SUBLANES, LANES), lambda i, j: (0, 0, 0)),
            side_in,
        ],
        out_specs=(
            pl.BlockSpec((CP_ROWS, CP_COLS), lambda i, j: (slot(i), j)),
            pl.BlockSpec((None, HIST, CP_COLS), lambda i, j: (slot(i), 0, j)),
            side_out,
        ),
        out_shape=(
            jax.ShapeDtypeStruct(((n_tiles + 1) * CP_ROWS, D_CONF), bf16),
            jax.ShapeDtypeStruct((n_tiles + 1, HIST, D_CONF), f32),
            side_shape,
        ),
        scratch_shapes=[
            pltpu.VMEM((N_LT, HIST + CP_ROWS, LANES), f32),
            pltpu.VMEM((N_LT, CP_ROWS, LANES), f32),
            pltpu.VMEM((CP_COLS // LANES, CP_ROWS, LANES), f32),
        ],
        compiler_params=_cparams(("arbitrary", "arbitrary")),
        name="conf_proj",
    )(u, w_t, w_t, hist, w3, p3, side[0])


CONF_BS = 16


def _conf_sample_kernel(ca_ref, cg_ref, buf_ref, w_ref, p_ref, y_ref, newbuf_ref):
    n_hist = CONF_KERNEL - 1
    n_tok = ca_ref.shape[0]
    glu = [ca_ref[t] * _sigmoid(cg_ref[t]) for t in range(n_tok)]
    win = lambda j: buf_ref[j] if j < n_hist else glu[j - n_hist]
    for j in range(n_hist):
        newbuf_ref[j] = win(j + n_tok)
    for t in range(n_tok):
        acc = jnp.broadcast_to(p_ref[0:1, :], (CONF_BS, D_CONF))
        for k in range(CONF_KERNEL):
            acc = acc + w_ref[k:k + 1, :] * win(t + k)
        for lt in range(N_LT):
            sl = slice(lt * LANES, (lt + 1) * LANES)
            y_ref[t, :, sl] = _conf_norm_act(acc[:, sl], p_ref[1:2, sl], p_ref[2:3, sl]).astype(bf16)


def _conf_sample(pc_t, buf_t, w2, p2, *, n_tok, tok_block0):
    n_seq = pc_t.shape[1]
    n_hist = CONF_KERNEL - 1
    seq_blocks = lambda col: (lambda i: (0, i, col))
    return pl.pallas_call(
        _conf_sample_kernel,
        grid=(n_seq // CONF_BS,),
        in_specs=[
            pl.BlockSpec((n_tok, CONF_BS, D_CONF), lambda i: (tok_block0, i, 0)),
            pl.BlockSpec((n_tok, CONF_BS, D_CONF), lambda i: (tok_block0, i, 1)),
            pl.BlockSpec((n_hist, CONF_BS, D_CONF), seq_blocks(0)),
            pl.BlockSpec((CONF_KERNEL, D_CONF), lambda i: (0, 0)),
            pl.BlockSpec((3, D_CONF), lambda i: (0, 0)),
        ],
        out_specs=(
            pl.BlockSpec((n_tok, CONF_BS, D_CONF), seq_blocks(0)),
            pl.BlockSpec((n_hist, CONF_BS, D_CONF), seq_blocks(0)),
        ),
        out_shape=(
            jax.ShapeDtypeStruct((n_tok, n_seq, D_CONF), bf16),
            jax.ShapeDtypeStruct((n_hist, n_seq, D_CONF), f32),
        ),
        compiler_params=_cparams(("arbitrary",)),
        name="conf_sample",
    )(pc_t, pc_t, buf_t, w2, p2)


def _seq_tiles(rows, n_seq, n_tok):
    w = rows.shape[-1]
    return jnp.pad(rows.reshape(n_seq, n_tok, w), ((0, 0), (SEQ_TILE - n_tok, 0), (0, 0))).reshape(n_seq * SEQ_TILE, w)


def kernel(x_prompt, x_sample, state_ssm, state_ssm_conv, state_conf_conv, meta_tokens, ffn1_norm, ffn1_w_gate, ffn1_w_up, ffn1_w_down, mix_norm, w_in, ssm_conv_w, ssm_conv_b, ssm_dt_bias, ssm_A_log, ssm_D, ssm_norm, conf_conv_w, conf_conv_b, conf_ln_g, conf_ln_b, w_out, ffn2_norm, ffn2_w_gate, ffn2_w_up, ffn2_w_down, final_norm):
    n_pb, n_pt = x_prompt.shape[0], x_prompt.shape[1]
    n_sb, n_st = x_sample.shape[0], x_sample.shape[1]
    assert (n_pb * n_pt, n_sb * n_st, n_st) == (N_PROMPT_ROWS, N_SAMPLE_ROWS, 4) and n_pt % CHUNK == 0

    row = lambda v: v.reshape(1, -1)
    swap01 = lambda v: jnp.transpose(v, (1, 0, 2))
    x_rows = (x_prompt.reshape(N_PROMPT_ROWS, D_MODEL), swap01(x_sample).reshape(N_SAMPLE_ROWS, D_MODEL), meta_tokens)
    h, u, w_t = _ffn(x_rows, row(ffn1_norm[0]), _to_bf16(ffn1_w_gate, 256), _to_bf16(ffn1_w_up, 256),
                     _to_bf16(ffn1_w_down, 704), row(mix_norm[0]), side=(jnp.transpose(w_in[0]), 96))

    c_in0 = D_SSM + CONV_DIM
    pzx, wg2 = _matmul_nt(u, w_t, row0=0, n=c_in0, tn=1536, tm=TM_PROJ, side=(ffn2_w_gate, 128))
    pdt = _matmul_nt(u, w_t, row0=c_in0, n=LANES, tn=LANES)
    n_small = N_ROWS - ROW_SAMPLE
    pc = _matmul_nt(u, w_t, row0=c_in0 + N_HEADS, n=2 * D_CONF, tn=1024, tm=n_small, a_rows=(ROW_SAMPLE, n_small))

    pad_heads = lambda v: jnp.pad(v.reshape(1, N_HEADS), ((0, 0), (0, LANES - N_HEADS)))
    ssm_w = (ssm_conv_w[0], row(ssm_conv_b[0]), pad_heads(ssm_dt_bias[0]), pad_heads(ssm_A_log[0]),
             row(jnp.repeat(ssm_D[0], HEAD_DIM)), row(ssm_norm[0]))
    tri = jnp.tril(jnp.ones((CHUNK, CHUNK), f32))
    e_mat = (jnp.arange(LANES)[:, None] == (jnp.arange(D_SSM)[None, :] // HEAD_DIM)).astype(bf16)
    lane_major = lambda v: v.reshape(v.shape[0], N_LT, LANES).transpose(1, 0, 2)
    conf_w = (lane_major(jnp.pad(conf_conv_w[0], ((0, HIST - CONF_KERNEL), (0, 0)))),
              lane_major(jnp.pad(jnp.stack([conf_conv_b[0], conf_ln_g[0], conf_ln_b[0]]), ((0, SUBLANES - 3), (0, 0)))))
    out_rows = jnp.arange(SSD_BS * 4)
    sel = ((out_rows // 4) * SEQ_TILE + TOK0 + out_rows % 4)[:, None] == jnp.arange(SSD_BS * SEQ_TILE)[None, :]
    sel = sel.astype(bf16)

    meta_block = ROW_META // CHUNK
    ys_m, h_meta, tail_meta = _ssd_chunks(
        pzx, pdt, jnp.zeros((SUBLANES, CONV_DIM), f32), jnp.zeros((D_STATE, D_SSM), f32), ssm_w, (tri, e_mat),
        n_seq=1, n_chunks=1, row_block0=meta_block, n_valid=N_META, transpose_out=False)
    yc_m, hist_meta = _conf_tiles(pc, jnp.zeros((HIST, D_CONF), f32), conf_w, n_seq=1, n_tiles=1, tt=CHUNK,
                                  row_block0=(ROW_META - ROW_SAMPLE) // CHUNK, n_valid=N_META)

    ys_p, p_ssm, p_tail, wo = _ssd_chunks(
        pzx, pdt, tail_meta[0], h_meta[0], ssm_w, (tri, e_mat),
        n_seq=n_pb, n_chunks=n_pt // CHUNK, row_block0=0, n_valid=CHUNK, transpose_out=True, side=(w_out, 64))
    yc_p, tile_hist, wu2 = _conf_proj(u, w_t, c_in0 + N_HEADS, hist_meta[0], conf_w, (ffn2_w_up, 64),
                                      n_seq=n_pb, seq_rows=n_pt)
    tiles_per_seq = n_pt // CP_ROWS
    p_hist = tile_hist[tiles_per_seq - 1:n_pb * tiles_per_seq:tiles_per_seq]

    smp = slice(ROW_SAMPLE, ROW_SAMPLE + N_SAMPLE_ROWS)
    seq_major = lambda v: swap01(v.reshape(n_st, n_sb, v.shape[-1]))
    pzx_s = seq_major(pzx[smp])
    pxbc_s = pzx_s[:, :, D_SSM:]
    pxbc8 = jnp.concatenate([jnp.zeros((n_sb, 1, CONV_DIM), f32), state_ssm_conv[0], pxbc_s], axis=1)
    ys_s, s_ssm = _ssd_sample(_seq_tiles(pzx_s[:, :, :D_SSM], n_sb, n_st), pxbc8.reshape(n_sb * SEQ_TILE, CONV_DIM),
                              _seq_tiles(seq_major(pdt[smp]), n_sb, n_st), state_ssm[0].reshape(n_sb, D_SSM, D_STATE),
                              ssm_w, e_mat, sel)
    ys_s = swap01(ys_s.reshape(n_sb, n_st, D_SSM)).reshape(N_SAMPLE_ROWS, D_SSM)
    yc_t, cconv_t = _conf_sample(pc.reshape(n_small // n_sb, n_sb, 2 * D_CONF), swap01(state_conf_conv[0]),
                                 conf_conv_w[0], jnp.stack([conf_conv_b[0], conf_ln_g[0], conf_ln_b[0]]),
                                 n_tok=n_st, tok_block0=0)
    yc_s = yc_t.reshape(N_SAMPLE_ROWS, D_CONF)
    s_cconv = swap01(cconv_t)

    h2, wd2 = _outproj(h, (ys_p, ys_s, ys_m), (yc_p, yc_s, yc_m), wo, tm=TM_PROJ, tn=512, side=(ffn2_w_down, 256))
    y_prompt, y_sample = _ffn((h2,), row(ffn2_norm[0]), wg2, wu2, wd2, row(final_norm),
                              dst_rows=(N_PROMPT_ROWS, N_SAMPLE_ROWS))
    n_hist = CONF_KERNEL - 1
    return (
        y_prompt.reshape(n_pb, n_pt, D_MODEL), swap01(y_sample.reshape(n_st, n_sb, D_MODEL)),
        p_ssm.reshape(1, n_pb, N_HEADS, HEAD_DIM, D_STATE),
        p_tail[:, SUBLANES - (SSM_CONV - 1):, :][None],
        p_hist[:, HIST - n_hist:, :][None],
        s_ssm.reshape(1, n_sb, N_HEADS, HEAD_DIM, D_STATE),
        pxbc_s[:, n_st - (SSM_CONV - 1):, :][None],
        s_cconv[None],
    )
```

```python
import functools

import jax
import jax.numpy as jnp
from jax import lax
from jax.experimental import pallas as pl
from jax.experimental.pallas import tpu as pltpu

f32 = jnp.float32
bf16 = jnp.bfloat16

D_MODEL = 2048
N_META = 16
D_SSM = 2048
D_CONF = 2048
HEAD_DIM = 64
N_HEADS = 32
N_GROUPS = 2
D_STATE = 128
SSM_CONV = 4
CONV_DIM = D_SSM + 2 * N_GROUPS * D_STATE
CONF_KERNEL = 31
CONF_GROUP = 128
D_FF = 5632
EPS = 1e-5
LOG2E = 1.4426950408889634

LANES = 128
SUBLANES = 8
CHUNK = 128
VMEM_LIMIT = 60 * 1024 * 1024

N_PROMPT_ROWS = 4 * 2048
N_SAMPLE_ROWS = 128 * 4
ROW_SAMPLE = N_PROMPT_ROWS
ROW_META = N_PROMPT_ROWS + N_SAMPLE_ROWS
N_ROWS = ROW_META + CHUNK
TM = 736
TM_PROJ = 1104
TF = 512


def _cparams(sem):
    return pltpu.CompilerParams(dimension_semantics=sem, vmem_limit_bytes=VMEM_LIMIT)


def _sigmoid(x):
    return 1.0 / (1.0 + jnp.exp(-x))


def _silu(x):
    return x * _sigmoid(x)


def _rms(x, g):
    return x * lax.rsqrt(jnp.mean(x * x, axis=-1, keepdims=True) + EPS) * g


def _softplus(x):
    return jnp.maximum(x, 0.0) + jnp.log1p(jnp.exp(-jnp.abs(x)))


def _split3(v):
    hi = v.astype(bf16).astype(f32)
    r1 = v - hi
    mid = r1.astype(bf16).astype(f32)
    return hi, mid, r1 - mid


def _row_segments(src_rows, n_total, tm):
    bounds, start = [], 0
    for k, n in enumerate(src_rows):
        bounds.append((k, start, n))
        start += n
    if start < n_total:
        bounds.append((None, start, n_total - start))
    table = []
    for t in range(n_total // tm):
        lo, hi = t * tm, (t + 1) * tm
        segs = []
        for k, s, n in bounds:
            a, b = max(lo, s), min(hi, s + n)
            if a < b:
                segs.append((k, a - s, a - lo, b - a))
        table.append(segs)
    return table


def _load_tile(i, seg_table, groups, sem):
    tm = groups[0][2].shape[0]
    special = [t for t, segs in enumerate(seg_table) if segs != [(0, t * tm, 0, tm)]]
    first = special[0] if special else len(seg_table)
    assert special == list(range(first, len(seg_table)))

    @pl.when(i < first)
    def _():
        for main_ref, _, dst_ref in groups:
            dst_ref[...] = main_ref[...]

    for t in special:
        @pl.when(i == t)
        def _(segs=seg_table[t]):
            copies = []
            for main_ref, tail_srcs, dst_ref in groups:
                for k, r0, d0, n in segs:
                    if k is None:
                        dst_ref[d0:d0 + n, :] = jnp.zeros((n, dst_ref.shape[1]), dst_ref.dtype)
                    elif k == 0:
                        dst_ref[d0:d0 + n, :] = main_ref[d0:d0 + n, :]
                    else:
                        cp = pltpu.make_async_copy(tail_srcs[k - 1].at[pl.ds(r0, n)], dst_ref.at[pl.ds(d0, n)], sem)
                        cp.start()
                        copies.append(cp)
            for cp in copies:
                cp.wait()


def _side_cast(w, block_rows, grid):
    r, c = w.shape[-2:]
    n_blocks = r // block_rows
    assert r % block_rows == 0 and n_blocks <= grid[0] * grid[1]
    blk = lambda i, j: jnp.minimum(i * grid[1] + j, n_blocks - 1)
    if w.ndim == 3:
        in_spec = pl.BlockSpec((None, block_rows, c), lambda i, j: (0, blk(i, j), 0))
    else:
        in_spec = pl.BlockSpec((block_rows, c), lambda i, j: (blk(i, j), 0))
    return in_spec, pl.BlockSpec((block_rows, c), lambda i, j: (blk(i, j), 0)), jax.ShapeDtypeStruct((r, c), bf16)


def _run_side_cast(side_refs):
    for src_ref, dst_ref in side_refs:
        dst_ref[...] = src_ref[...].astype(bf16)


def _drain_rows(i, out_table, dsts, acc_scr, sems):
    def copies(t):
        return [pltpu.make_async_copy(acc_scr.at[t % 2, pl.ds(s0, n)], dsts[k].at[pl.ds(r0, n)], sems.at[t % 2])
                for k, r0, s0, n in out_table[t] if k is not None]

    n_tiles = len(out_table)
    for t in range(n_tiles):
        @pl.when(i == t)
        def _(t=t):
            if t > 0:
                for cp in copies(t - 1):
                    cp.wait()
            for cp in copies(t):
                cp.start()
            if t == n_tiles - 1:
                for cp in copies(t):
                    cp.wait()


def _ffn_kernel(*refs, seg_table, out_table, n_src, n_side):
    x_ref, tail_srcs = refs[0], refs[1:n_src]
    nw_ref, wg_ref, wu_ref, wd_ref, pw_ref = refs[n_src:n_src + 5]
    pos = n_src + 5
    side_in, pos = refs[pos:pos + n_side], pos + n_side
    i = pl.program_id(0)
    f = pl.program_id(1)
    if out_table is None:
        (acc, yn_ref), pos = refs[pos:pos + 2], pos + 2
        side_out, pos = refs[pos:pos + n_side], pos + n_side
        xn_ref, sem = refs[pos:]
    else:
        n_dst = 1 + max(k for segs in out_table for k, _, _, _ in segs if k is not None)
        dsts, pos = refs[pos:pos + n_dst], pos + n_dst
        side_out, pos = refs[pos:pos + n_side], pos + n_side
        acc_scr, xn_ref, sem, out_sems = refs[pos:]
        acc = acc_scr.at[i % 2]
    _run_side_cast(zip(side_in, side_out))

    @pl.when(f == 0)
    def _():
        _load_tile(i, seg_table, [(x_ref, tail_srcs, acc)], sem)
        xn_ref[...] = _rms(acc[...], nw_ref[...]).astype(bf16)

    xn = xn_ref[...]
    g = jnp.dot(xn, wg_ref[...], preferred_element_type=f32)
    u = jnp.dot(xn, wu_ref[...], preferred_element_type=f32)
    a = (_silu(g) * u).astype(bf16)
    acc[...] += 0.5 * jnp.dot(a, wd_ref[...], preferred_element_type=f32)

    @pl.when(f == pl.num_programs(1) - 1)
    def _():
        if out_table is None:
            yn_ref[...] = _rms(acc[...], pw_ref[...]).astype(yn_ref.dtype)
        else:
            acc[...] = _rms(acc[...], pw_ref[...])
            _drain_rows(i, out_table, dsts, acc_scr, out_sems)


def _ffn(srcs, norm_w, wg, wu, wd, post_w, *, dst_rows=None, sides=(), tf=TF):
    seg_table = _row_segments([s.shape[0] for s in srcs], N_ROWS, TM)
    grid = (N_ROWS // TM, D_FF // tf)
    side_specs = [_side_cast(*s, grid) for s in sides]
    row_spec = pl.BlockSpec((TM, D_MODEL), lambda i, f: (i, 0))
    vec_spec = pl.BlockSpec((1, D_MODEL), lambda i, f: (0, 0))
    scratch = [pltpu.VMEM((TM, D_MODEL), bf16), pltpu.SemaphoreType.DMA(())]
    if dst_rows is None:
        out_table = None
        out_shape = (jax.ShapeDtypeStruct((N_ROWS, D_MODEL), f32), jax.ShapeDtypeStruct((N_ROWS, D_MODEL), bf16))
        out_specs = (row_spec, row_spec)
    else:
        out_table = _row_segments(dst_rows, N_ROWS, TM)
        out_shape = tuple(jax.ShapeDtypeStruct((n, D_MODEL), f32) for n in dst_rows)
        out_specs = tuple(pl.BlockSpec(memory_space=pl.ANY) for _ in dst_rows)
        scratch = [pltpu.VMEM((2, TM, D_MODEL), f32)] + scratch + [pltpu.SemaphoreType.DMA((2,))]
    return pl.pallas_call(
        functools.partial(_ffn_kernel, seg_table=seg_table, out_table=out_table, n_src=len(srcs),
                          n_side=len(side_specs)),
        grid=grid,
        in_specs=[row_spec] + [pl.BlockSpec(memory_space=pl.ANY)] * (len(srcs) - 1) + [
            vec_spec,
            pl.BlockSpec((D_MODEL, tf), lambda i, f: (0, f)),
            pl.BlockSpec((D_MODEL, tf), lambda i, f: (0, f)),
            pl.BlockSpec((tf, D_MODEL), lambda i, f: (f, 0)),
            vec_spec,
        ] + [s[0] for s in side_specs],
        out_specs=tuple(out_specs) + tuple(s[1] for s in side_specs),
        out_shape=tuple(out_shape) + tuple(s[2] for s in side_specs),
        scratch_shapes=scratch,
        compiler_params=_cparams(("arbitrary", "arbitrary")),
        name="ffn",
    )(*srcs, norm_w, wg, wu, wd, post_w, *[s[0] for s in sides])


def _cast_kernel(x_ref, o_ref):
    o_ref[...] = x_ref[...].astype(bf16)


def _to_bf16(w, block_rows):
    r, c = w.shape[-2:]
    if w.ndim == 3:
        in_spec = pl.BlockSpec((None, block_rows, c), lambda i: (0, i, 0))
    else:
        in_spec = pl.BlockSpec((block_rows, c), lambda i: (i, 0))
    return pl.pallas_call(
        _cast_kernel,
        grid=(r // block_rows,),
        in_specs=[in_spec],
        out_specs=pl.BlockSpec((block_rows, c), lambda i: (i, 0)),
        out_shape=jax.ShapeDtypeStruct((r, c), bf16),
        compiler_params=_cparams(("arbitrary",)),
        name="cast_bf16",
    )(w)


def _matmul_nt_kernel(a_ref, w_ref, *rest):
    n_side = len(rest) // 2
    o_ref = rest[n_side]
    _run_side_cast(zip(rest[:n_side], rest[n_side + 1:]))
    o_ref[...] = lax.dot_general(a_ref[...], w_ref[...], (((1,), (1,)), ((), ())), preferred_element_type=f32)


def _matmul_nt(a, w_t, *, row0, n, tn, tm=TM, a_rows=None, side=None):
    k = a.shape[1]
    a_row0, n_rows = a_rows if a_rows else (0, a.shape[0])
    align = 16
    assert row0 % align == 0 and tn % align == 0 and n % tn == 0 and n_rows % tm == 0 and a_row0 % align == 0
    grid = (n_rows // tm, n // tn)
    side_specs = [_side_cast(*side, grid)] if side else []
    out = pl.pallas_call(
        _matmul_nt_kernel,
        grid=grid,
        in_specs=[
            pl.BlockSpec((pl.Element(tm), pl.Element(k)), lambda i, j: (pl.multiple_of(a_row0 + tm * i, align), 0)),
            pl.BlockSpec((pl.Element(tn), pl.Element(k)), lambda i, j: (pl.multiple_of(row0 + tn * j, align), 0)),
        ] + [s[0] for s in side_specs],
        out_specs=(pl.BlockSpec((tm, tn), lambda i, j: (i, j)),) + tuple(s[1] for s in side_specs),
        out_shape=(jax.ShapeDtypeStruct((n_rows, n), f32),) + tuple(s[2] for s in side_specs),
        compiler_params=_cparams(("arbitrary", "arbitrary")),
        name="in_proj",
    )(a, w_t, *([side[0]] if side else []))
    return out if side else out[0]


def _outproj_kernel(*refs, seg_table, n_src, n_side):
    ys_ref, ys_tails = refs[0], refs[1:n_src]
    yc_ref, yc_tails = refs[n_src], refs[n_src + 1:2 * n_src]
    h_ref, w0_ref, w1_ref = refs[2 * n_src:2 * n_src + 3]
    pos = 2 * n_src + 3
    side_in, o_ref, side_out = refs[pos:pos + n_side], refs[pos + n_side], refs[pos + n_side + 1:pos + 2 * n_side + 1]
    ys_scr, yc_scr, sem = refs[pos + 2 * n_side + 1:]
    _run_side_cast(zip(side_in, side_out))

    @pl.when(pl.program_id(1) == 0)
    def _():
        _load_tile(pl.program_id(0), seg_table, [(ys_ref, ys_tails, ys_scr), (yc_ref, yc_tails, yc_scr)], sem)

    acc = jnp.dot(ys_scr[...], w0_ref[...], preferred_element_type=f32)
    acc = acc + jnp.dot(yc_scr[...], w1_ref[...], preferred_element_type=f32)
    o_ref[...] = h_ref[...] + acc


def _outproj(h, ys_srcs, yc_srcs, w_out, *, main_rows, tm, tn, side=None):
    n_rows = h.shape[0]
    assert n_rows % tm == 0
    seg_table = _row_segments([main_rows] + [s.shape[0] for s in ys_srcs[1:]], n_rows, tm)
    n_src = len(ys_srcs)
    grid = (n_rows // tm, D_MODEL // tn)
    side_specs = [_side_cast(*side, grid)] if side else []
    y_specs = [pl.BlockSpec((tm, D_SSM), lambda i, j: (i, 0))] + [pl.BlockSpec(memory_space=pl.ANY)] * (n_src - 1)
    out = pl.pallas_call(
        functools.partial(_outproj_kernel, seg_table=seg_table, n_src=n_src, n_side=len(side_specs)),
        grid=grid,
        in_specs=y_specs + y_specs + [
            pl.BlockSpec((tm, tn), lambda i, j: (i, j)),
            pl.BlockSpec((D_SSM, tn), lambda i, j: (0, j)),
            pl.BlockSpec((D_CONF, tn), lambda i, j: (1, j)),
        ] + [s[0] for s in side_specs],
        out_specs=(pl.BlockSpec((tm, tn), lambda i, j: (i, j)),) + tuple(s[1] for s in side_specs),
        out_shape=(jax.ShapeDtypeStruct((n_rows, D_MODEL), f32),) + tuple(s[2] for s in side_specs),
        scratch_shapes=[pltpu.VMEM((tm, D_SSM), bf16), pltpu.VMEM((tm, D_CONF), bf16), pltpu.SemaphoreType.DMA(())],
        compiler_params=_cparams(("arbitrary", "arbitrary")),
        name="out_proj",
    )(*ys_srcs, *yc_srcs, h, w_out, w_out, *([side[0]] if side else []))
    return out if side else out[0]


def _gate_norm(y, z, nw):
    y = y * _silu(z)
    half = D_SSM // N_GROUPS
    parts = []
    for g in range(N_GROUPS):
        yg = y[:, g * half:(g + 1) * half]
        ms = jnp.mean(yg * yg, axis=-1, keepdims=True)
        parts.append(yg * lax.rsqrt(ms + EPS) * nw[:, g * half:(g + 1) * half])
    return jnp.concatenate(parts, axis=1)


def _ssd_chunk_math(z, xbc, dt_raw, cinit_ref, hinit_ref, cw_ref, cb_ref, dtb_ref, alog_ref,
                    dx_ref, nw_ref, tri_ref, e_ref, y_ref, hout_ref, tail_ref, win_scr, ht_scr, y_scr,
                    *, n_valid, transpose_out, first=None, last=None, side_work=None):
    if first is None:
        first = pl.program_id(1) == 0
        last = pl.program_id(1) == pl.num_programs(1) - 1

    @pl.when(first)
    def _():
        ht_scr[...] = hinit_ref[...]
        win_scr[0:SUBLANES, :] = cinit_ref[...]

    win_scr[SUBLANES:SUBLANES + CHUNK, :] = xbc
    acc = cb_ref[...] + cw_ref[SSM_CONV - 1:SSM_CONV, :] * xbc
    for j in range(1, SSM_CONV):
        acc = acc + cw_ref[SSM_CONV - 1 - j:SSM_CONV - j, :] * win_scr[SUBLANES - j:SUBLANES - j + CHUNK, :]
    tail_ref[...] = win_scr[n_valid:n_valid + SUBLANES, :]
    win_scr[0:SUBLANES, :] = win_scr[CHUNK:CHUNK + SUBLANES, :]
    xc = _silu(acc)
    b_mats = [xc[:, D_SSM + g * D_STATE:D_SSM + (g + 1) * D_STATE] for g in range(N_GROUPS)]
    c_mats = [xc[:, D_SSM + (N_GROUPS + g) * D_STATE:D_SSM + (N_GROUPS + g + 1) * D_STATE] for g in range(N_GROUPS)]

    rows = lax.broadcasted_iota(jnp.int32, (CHUNK, LANES), 0)
    lanes = lax.broadcasted_iota(jnp.int32, (CHUNK, LANES), 1)
    dt = _softplus(dt_raw + dtb_ref[...])
    if n_valid < CHUNK:
        dt = jnp.where(rows < n_valid, dt, 0.0)
    a = dt * (-jnp.exp(alog_ref[...]))
    acs = jnp.dot(tri_ref[...], a, precision=lax.Precision.HIGHEST, preferred_element_type=f32) * LOG2E
    total = acs[CHUNK - 1:CHUNK, :]
    dend = jnp.exp2(total - acs)
    acs_t = acs.T
    dt_t = dt.T
    w_t = (dt * dend).T
    hi, mid, lo = _split3(jnp.exp2(total))
    rid = lax.broadcasted_iota(jnp.int32, (SUBLANES, LANES), 0)
    cd3 = jnp.where(rid == 0, hi, jnp.where(rid == 1, mid, jnp.where(rid == 2, lo, 0.0)))
    cdx = jnp.sum(jnp.dot(cd3.astype(bf16), e_ref[...], preferred_element_type=f32), axis=0, keepdims=True)

    cb_mats, bt_mats = [], []
    for g in range(N_GROUPS):
        cb_mats.append(lax.dot_general(c_mats[g].astype(bf16), b_mats[g].astype(bf16), (((1,), (1,)), ((), ())),
                                       preferred_element_type=f32))
        bt_mats.append(b_mats[g].T)

    lane_lo = lanes < HEAD_DIM
    tril = rows >= lanes
    heads_per_group = N_HEADS // N_GROUPS
    for j in range(N_HEADS // 2):
        if side_work is not None:
            side_work(j)
        sl = slice(j * LANES, (j + 1) * LANES)
        xs_pair = xc[:, sl]
        ht_pair = ht_scr[:, sl]
        y_pair = jnp.zeros((CHUNK, LANES), f32)
        upd = jnp.zeros((D_STATE, LANES), f32)
        for half in range(2):
            h = 2 * j + half
            g = h // heads_per_group
            keep = lane_lo if half == 0 else jnp.logical_not(lane_lo)
            xs_m = jnp.where(keep, xs_pair, 0.0).astype(bf16)
            ht_m = jnp.where(keep, ht_pair, 0.0).astype(bf16)
            acs_col = jnp.broadcast_to(acs[:, h:h + 1], (CHUNK, CHUNK))
            acs_row = jnp.broadcast_to(acs_t[h:h + 1, :], (CHUNK, CHUNK))
            lmat = jnp.exp2(jnp.where(tril, acs_col - acs_row, -jnp.inf))
            m_h = cb_mats[g] * lmat * dt_t[h:h + 1, :]
            ce_h = c_mats[g] * jnp.exp2(acs_col)
            lhs = jnp.concatenate([m_h, ce_h], axis=1).astype(bf16)
            rhs = jnp.concatenate([xs_m, ht_m], axis=0)
            y_pair = y_pair + jnp.dot(lhs, rhs, preferred_element_type=f32)
            bw = (bt_mats[g] * w_t[h:h + 1, :]).astype(bf16)
            upd = upd + jnp.dot(bw, xs_m, preferred_element_type=f32)
        y_scr[:, sl] = y_pair
        ht_scr[:, sl] = ht_pair * cdx[:, sl] + upd

    xs = xc[:, :D_SSM]
    y = y_scr[...] + xs * dx_ref[...]
    y_ref[...] = _gate_norm(y, z(), nw_ref[...]).astype(bf16)

    @pl.when(last)
    def _():
        if transpose_out:
            hout_ref[...] = ht_scr[...].T
        else:
            hout_ref[...] = ht_scr[...]


N_SSD_PARAMS = 10


def _ssd_chunk_kernel(z_ref, xs_ref, bc_ref, dt_ref, *rest, n_valid, transpose_out, n_side):
    params, rest = rest[:N_SSD_PARAMS], rest[N_SSD_PARAMS:]
    side_in, outs = rest[:n_side], rest[n_side:n_side + 3]
    side_out, scratch = rest[n_side + 3:2 * n_side + 3], rest[2 * n_side + 3:]
    _run_side_cast(zip(side_in, side_out))
    xbc = jnp.concatenate([xs_ref[...], bc_ref[...]], axis=1)
    _ssd_chunk_math(lambda: z_ref[...], xbc, dt_ref[...], *params, *outs, *scratch,
                    n_valid=n_valid, transpose_out=transpose_out)


def _ssd_chunks(pzx, pdt, cinit, hinit, ssm_w, consts, *, n_seq, n_chunks, row_block0, n_valid, transpose_out,
                side=None):
    cw, cb, dtb, alog, dx, nw = ssm_w
    tri, e_mat = consts
    bc_dim = CONV_DIM - D_SSM
    side_specs = [_side_cast(*side, (n_seq, n_chunks))] if side else []
    rows_map = lambda b, c: (row_block0 + b * n_chunks + c, 0)
    fixed = lambda b, c: (0, 0)
    in_specs = [
        pl.BlockSpec((CHUNK, D_SSM), rows_map),
        pl.BlockSpec((CHUNK, D_SSM), lambda b, c: (row_block0 + b * n_chunks + c, 1)),
        pl.BlockSpec((CHUNK, bc_dim), lambda b, c: (row_block0 + b * n_chunks + c, 2 * D_SSM // bc_dim)),
        pl.BlockSpec((CHUNK, LANES), rows_map),
        pl.BlockSpec((SUBLANES, CONV_DIM), fixed),
        pl.BlockSpec((D_STATE, D_SSM), fixed),
        pl.BlockSpec((SSM_CONV, CONV_DIM), fixed),
        pl.BlockSpec((1, CONV_DIM), fixed),
        pl.BlockSpec((1, LANES), fixed),
        pl.BlockSpec((1, LANES), fixed),
        pl.BlockSpec((1, D_SSM), fixed),
        pl.BlockSpec((1, D_SSM), fixed),
        pl.BlockSpec((CHUNK, CHUNK), fixed),
        pl.BlockSpec((LANES, D_SSM), fixed),
    ]
    args = [pzx, pzx, pzx, pdt, cinit, hinit, cw, cb, dtb, alog, dx, nw, tri, e_mat] + ([side[0]] if side else [])
    hshape = (n_seq, D_SSM, D_STATE) if transpose_out else (n_seq, D_STATE, D_SSM)
    return pl.pallas_call(
        functools.partial(_ssd_chunk_kernel, n_valid=n_valid, transpose_out=transpose_out, n_side=len(side_specs)),
        grid=(n_seq, n_chunks),
        in_specs=in_specs + [s[0] for s in side_specs],
        out_specs=(
            pl.BlockSpec((CHUNK, D_SSM), lambda b, c: (b * n_chunks + c, 0)),
            pl.BlockSpec((None,) + hshape[1:], lambda b, c: (b, 0, 0)),
            pl.BlockSpec((None, SUBLANES, CONV_DIM), lambda b, c: (b, 0, 0)),
        ) + tuple(s[1] for s in side_specs),
        out_shape=(
            jax.ShapeDtypeStruct((n_seq * n_chunks * CHUNK, D_SSM), bf16),
            jax.ShapeDtypeStruct(hshape, f32),
            jax.ShapeDtypeStruct((n_seq, SUBLANES, CONV_DIM), f32),
        ) + tuple(s[2] for s in side_specs),
        scratch_shapes=[
            pltpu.VMEM((SUBLANES + CHUNK, CONV_DIM), f32),
            pltpu.VMEM((D_STATE, D_SSM), f32),
            pltpu.VMEM((CHUNK, D_SSM), f32),
        ],
        compiler_params=_cparams(("arbitrary", "arbitrary")),
        name="ssd_chunks",
    )(*args)


SP_CHUNKS = 4
SP_ROWS = SP_CHUNKS * CHUNK
SP_COLS = 1280


def _ssd_proj_kernel(u_ref, w_ref, *rest, chunks_per_seq):
    params, rest = rest[:N_SSD_PARAMS], rest[N_SSD_PARAMS:]
    side_in, y_ref, hout_ref, tail_ref, side_out, win_scr, ht_scr, y_scr, p_cur, p_next = rest
    i = pl.program_id(0)
    j = pl.program_id(1)
    side_out[...] = side_in[...].astype(bf16)

    @pl.when((i == 0) & (j == 0))
    def _():
        p_next[...] = jnp.zeros(p_next.shape, f32)
        ht_scr[...] = jnp.zeros(ht_scr.shape, f32)
        win_scr[...] = jnp.zeros(win_scr.shape, f32)

    @pl.when(j == 0)
    def _():
        p_cur[...] = p_next[...]

    piece = 2 * LANES
    n_pieces = SP_COLS // piece
    n_pairs = N_HEADS // 2

    def side_work(pair):
        if pair % (n_pairs // n_pieces) == 0 and pair // (n_pairs // n_pieces) < n_pieces:
            q = pair // (n_pairs // n_pieces)
            p_next[j, :, q * piece:(q + 1) * piece] = lax.dot_general(
                u_ref[...], w_ref[q * piece:(q + 1) * piece, :], (((1,), (1,)), ((), ())), preferred_element_type=f32)

    rows = pl.ds(pl.multiple_of(j * CHUNK, CHUNK), CHUNK)

    def cols(lo, hi):
        parts = []
        for k in range(SP_CHUNKS):
            a, b = max(lo, k * SP_COLS), min(hi, (k + 1) * SP_COLS)
            if a < b:
                parts.append(p_cur[k, rows, a - k * SP_COLS:b - k * SP_COLS])
        return parts[0] if len(parts) == 1 else jnp.concatenate(parts, axis=1)

    c = ((i - 1) % (chunks_per_seq // SP_CHUNKS)) * SP_CHUNKS + j
    _ssd_chunk_math(lambda: cols(0, D_SSM), cols(D_SSM, D_SSM + CONV_DIM),
                    cols(D_SSM + CONV_DIM, D_SSM + CONV_DIM + LANES), *params, y_ref, hout_ref, tail_ref,
                    win_scr, ht_scr, y_scr, n_valid=CHUNK, transpose_out=True,
                    first=(c == 0) & (i > 0), last=c == chunks_per_seq - 1, side_work=side_work)


def _ssd_proj(u, w_t, cinit, hinit, ssm_w, consts, side, *, n_seq, chunks_per_seq):
    cw, cb, dtb, alog, dx, nw = ssm_w
    tri, e_mat = consts
    tiles_per_seq = chunks_per_seq // SP_CHUNKS
    n_tiles = n_seq * tiles_per_seq
    grid = (n_tiles + 1, SP_CHUNKS)
    side_in, side_out, side_shape = _side_cast(*side, grid)
    fixed = lambda i, j: (0, 0)
    slot = lambda i: (i + n_tiles) % (n_tiles + 1)
    seq_slot = lambda i: jnp.where(i == 0, n_seq, (i - 1) // tiles_per_seq)
    return pl.pallas_call(
        functools.partial(_ssd_proj_kernel, chunks_per_seq=chunks_per_seq),
        grid=grid,
        in_specs=[
            pl.BlockSpec((SP_ROWS, D_MODEL), lambda i, j: (jnp.minimum(i, n_tiles - 1), 0)),
            pl.BlockSpec((SP_COLS, D_MODEL), lambda i, j: (j, 0)),
            pl.BlockSpec((SUBLANES, CONV_DIM), fixed),
            pl.BlockSpec((D_STATE, D_SSM), fixed),
            pl.BlockSpec((SSM_CONV, CONV_DIM), fixed),
            pl.BlockSpec((1, CONV_DIM), fixed),
            pl.BlockSpec((1, LANES), fixed),
            pl.BlockSpec((1, LANES), fixed),
            pl.BlockSpec((1, D_SSM), fixed),
            pl.BlockSpec((1, D_SSM), fixed),
            pl.BlockSpec((CHUNK, CHUNK), fixed),
            pl.BlockSpec((LANES, D_SSM), fixed),
            side_in,
        ],
        out_specs=(
            pl.BlockSpec((CHUNK, D_SSM), lambda i, j: (slot(i) * SP_CHUNKS + j, 0)),
            pl.BlockSpec((None, D_SSM, D_STATE), lambda i, j: (seq_slot(i), 0, 0)),
            pl.BlockSpec((None, SUBLANES, CONV_DIM), lambda i, j: (seq_slot(i), 0, 0)),
            side_out,
        ),
        out_shape=(
            jax.ShapeDtypeStruct(((n_tiles + 1) * SP_ROWS, D_SSM), bf16),
            jax.ShapeDtypeStruct((n_seq + 1, D_SSM, D_STATE), f32),
            jax.ShapeDtypeStruct((n_seq + 1, SUBLANES, CONV_DIM), f32),
            side_shape,
        ),
        scratch_shapes=[
            pltpu.VMEM((SUBLANES + CHUNK, CONV_DIM), f32),
            pltpu.VMEM((D_STATE, D_SSM), f32),
            pltpu.VMEM((CHUNK, D_SSM), f32),
            pltpu.VMEM((SP_CHUNKS, SP_ROWS, SP_COLS), f32),
            pltpu.VMEM((SP_CHUNKS, SP_ROWS, SP_COLS), f32),
        ],
        compiler_params=_cparams(("arbitrary", "arbitrary")),
        name="ssd_proj",
    )(u, w_t, cinit, hinit, cw, cb, dtb, alog, dx, nw, tri, e_mat, side[0])


SEQ_TILE = SUBLANES
TOK0 = SEQ_TILE - 4
SSD_BS = 8


def _ssd_sample_kernel(z_ref, xbc_ref, dt_ref, h0_ref, cw_ref, cb_ref, dtb_ref, alog_ref, dx_ref, nw_ref,
                       e_ref, sel_ref, y_ref, hnew_ref, yoff_scr, lhs_scr, rhs_scr):
    n_rows = SSD_BS * SEQ_TILE
    half = D_SSM // N_GROUPS
    x8 = xbc_ref[...]
    acc = cb_ref[...] + cw_ref[SSM_CONV - 1:SSM_CONV, :] * x8
    for j in range(1, SSM_CONV):
        acc = acc + cw_ref[SSM_CONV - 1 - j:SSM_CONV - j, :] * pltpu.roll(x8, j, axis=0)
    xc = _silu(acc)
    xs = xc[:, :D_SSM]
    bm = xc[:, D_SSM:D_SSM + N_GROUPS * D_STATE]
    cm = xc[:, D_SSM + N_GROUPS * D_STATE:]

    pos = lax.broadcasted_iota(jnp.int32, (n_rows, LANES), 0) % SEQ_TILE
    lanes = lax.broadcasted_iota(jnp.int32, (n_rows, LANES), 1)
    tok = pos >= TOK0
    dt = jnp.where(tok, _softplus(dt_ref[...] + dtb_ref[...]), 0.0)
    a = dt * (-jnp.exp(alog_ref[...]))
    acs = a
    for d in range(1, 4):
        acs = acs + pltpu.roll(a, d, axis=0)
    tot = jnp.sum(a.reshape(SSD_BS, SEQ_TILE, LANES), axis=1, keepdims=True)
    tot = jnp.broadcast_to(tot, (SSD_BS, SEQ_TILE, LANES)).reshape(n_rows, LANES)
    dend = jnp.where(tok, jnp.exp(tot - acs), 0.0)
    eacs = jnp.where(tok, jnp.exp(acs), 0.0)
    cdec = jnp.exp(tot)

    def expand(v):
        hi, mid, lo = _split3(v)
        stack = jnp.concatenate([hi, mid, lo], axis=0).astype(bf16)
        r = jnp.dot(stack, e_ref[...], preferred_element_type=f32)
        return r[0:n_rows] + r[n_rows:2 * n_rows] + r[2 * n_rows:3 * n_rows]

    heads_per_group = N_HEADS // N_GROUPS
    y = expand(eacs)
    for s in range(SSD_BS):
        for g in range(N_GROUPS):
            c8 = cm[s * SEQ_TILE:(s + 1) * SEQ_TILE, g * D_STATE:(g + 1) * D_STATE].astype(bf16)
            hg = h0_ref[s, g * half:(g + 1) * half, :].astype(bf16)
            yoff_scr[s * SEQ_TILE:(s + 1) * SEQ_TILE, g * half:(g + 1) * half] = lax.dot_general(
                c8, hg, (((1,), (1,)), ((), ())), preferred_element_type=f32)
    y = y * yoff_scr[...] + xs * dx_ref[...]
    for d in range(4):
        b_sh = pltpu.roll(bm, d, axis=0) if d else bm
        prod = cm * b_sh
        cbs = [jnp.sum(prod[:, g * D_STATE:(g + 1) * D_STATE], axis=-1, keepdims=True) for g in range(N_GROUPS)]
        cbh = jnp.where(lanes < heads_per_group, cbs[0], cbs[1])
        if d:
            coef = cbh * jnp.exp(acs - pltpu.roll(acs, d, axis=0)) * pltpu.roll(dt, d, axis=0)
        else:
            coef = cbh * dt
        coef = jnp.where(pos >= TOK0 + d, coef, 0.0)
        y = y + expand(coef) * (pltpu.roll(xs, d, axis=0) if d else xs)

    yn = _gate_norm(y, z_ref[...], nw_ref[...]).astype(bf16)
    y_ref[...] = jnp.dot(sel_ref[...], yn, preferred_element_type=f32).astype(bf16)

    xw = xs * expand(dt * dend)
    hi, mid, lo = _split3(expand(cdec))
    pos_w = lax.broadcasted_iota(jnp.int32, (n_rows, D_SSM), 0) % SEQ_TILE
    lhs_scr[...] = jnp.where(pos_w == 0, hi, jnp.where(pos_w == 1, mid, jnp.where(pos_w == 2, lo, xw)))
    ones = jnp.where(pos < 3, 1.0, 0.0)
    for g in range(N_GROUPS):
        rhs_scr[:, 2 * g * D_STATE:(2 * g + 1) * D_STATE] = jnp.where(tok, bm[:, g * D_STATE:(g + 1) * D_STATE], 0.0)
        rhs_scr[:, (2 * g + 1) * D_STATE:(2 * g + 2) * D_STATE] = ones
    for s in range(SSD_BS):
        for g in range(N_GROUPS):
            l8 = lhs_scr[s * SEQ_TILE:(s + 1) * SEQ_TILE, g * half:(g + 1) * half].astype(bf16)
            r8 = rhs_scr[s * SEQ_TILE:(s + 1) * SEQ_TILE, 2 * g * D_STATE:(2 * g + 2) * D_STATE].astype(bf16)
            res = lax.dot_general(l8, r8, (((0,), (0,)), ((), ())), preferred_element_type=f32)
            hnew_ref[s, g * half:(g + 1) * half, :] = (
                h0_ref[s, g * half:(g + 1) * half, :] * res[:, D_STATE:] + res[:, :D_STATE])


def _ssd_sample(pz8, pxbc8, pdt8, h0, ssm_w, e_mat, sel):
    cw, cb, dtb, alog, dx, nw = ssm_w
    n_seq = h0.shape[0]
    n_rows = SSD_BS * SEQ_TILE
    n_out = SSD_BS * 4
    rows_map = lambda i: (i, 0)
    fixed = lambda i: (0, 0)
    return pl.pallas_call(
        _ssd_sample_kernel,
        grid=(n_seq // SSD_BS,),
        in_specs=[
            pl.BlockSpec((n_rows, D_SSM), rows_map),
            pl.BlockSpec((n_rows, CONV_DIM), rows_map),
            pl.BlockSpec((n_rows, LANES), rows_map),
            pl.BlockSpec((SSD_BS, D_SSM, D_STATE), lambda i: (i, 0, 0)),
            pl.BlockSpec((SSM_CONV, CONV_DIM), fixed),
            pl.BlockSpec((1, CONV_DIM), fixed),
            pl.BlockSpec((1, LANES), fixed),
            pl.BlockSpec((1, LANES), fixed),
            pl.BlockSpec((1, D_SSM), fixed),
            pl.BlockSpec((1, D_SSM), fixed),
            pl.BlockSpec((LANES, D_SSM), fixed),
            pl.BlockSpec((n_out, n_rows), fixed),
        ],
        out_specs=(
            pl.BlockSpec((n_out, D_SSM), rows_map),
            pl.BlockSpec((SSD_BS, D_SSM, D_STATE), lambda i: (i, 0, 0)),
        ),
        out_shape=(
            jax.ShapeDtypeStruct((n_seq * 4, D_SSM), bf16),
            jax.ShapeDtypeStruct((n_seq, D_SSM, D_STATE), f32),
        ),
        scratch_shapes=[
            pltpu.VMEM((n_rows, D_SSM), f32),
            pltpu.VMEM((n_rows, D_SSM), f32),
            pltpu.VMEM((n_rows, 2 * N_GROUPS * D_STATE), f32),
        ],
        compiler_params=_cparams(("arbitrary",)),
        name="ssd_sample",
    )(pz8, pxbc8, pdt8, h0, cw, cb, dtb, alog, dx, nw, e_mat, sel)


N_LT = D_CONF // LANES
HIST = 32
CONF_RC = 64


def _conf_norm_act(acc, g, beta):
    mu = jnp.mean(acc, axis=-1, keepdims=True)
    xc = acc - mu
    var = jnp.mean(xc * xc, axis=-1, keepdims=True)
    return _silu(xc * lax.rsqrt(var + EPS) * g + beta)


def _conf_conv_tile(s_tile, w_ref, p_ref, lt, y_tile, tt):
    off = HIST - (CONF_KERNEL - 1)
    for r0 in range(0, tt, CONF_RC):
        acc = jnp.broadcast_to(p_ref[lt, 0:1, :], (CONF_RC, LANES))
        for k in range(CONF_KERNEL):
            acc = acc + w_ref[lt, k:k + 1, :] * s_tile[r0 + off + k:r0 + off + k + CONF_RC, :]
        y_tile[r0:r0 + CONF_RC, :] = _conf_norm_act(acc, p_ref[lt, 1:2, :], p_ref[lt, 2:3, :])


def _conf_tile_kernel(ca_ref, cg_ref, hist_ref, w_ref, p_ref, y_ref, tail_ref, s_scr, y_scr, *, tt, n_valid):
    t = pl.program_id(1)

    @pl.when(t == 0)
    def _():
        for lt in range(N_LT):
            s_scr[lt, 0:HIST, :] = hist_ref[:, lt * LANES:(lt + 1) * LANES]

    @pl.when(t > 0)
    def _():
        s_scr[:, 0:HIST, :] = s_scr[:, tt:tt + HIST, :]

    glu = ca_ref[...] * _sigmoid(cg_ref[...])
    for lt in range(N_LT):
        s_scr[lt, HIST:HIST + tt, :] = glu[:, lt * LANES:(lt + 1) * LANES]

    def lane_tile(lt, carry):
        _conf_conv_tile(s_scr.at[lt], w_ref, p_ref, lt, y_scr.at[lt], tt)
        return carry

    lax.fori_loop(0, N_LT, lane_tile, 0)
    for lt in range(N_LT):
        y_ref[:, lt * LANES:(lt + 1) * LANES] = y_scr[lt].astype(bf16)
        tail_ref[:, lt * LANES:(lt + 1) * LANES] = s_scr[lt, n_valid:n_valid + HIST, :]


def _conf_tiles(pc, hist, conf_w, *, n_seq, n_tiles, tt, row_block0, n_valid):
    w3, p3 = conf_w
    in_specs = [
        pl.BlockSpec((tt, D_CONF), lambda b, t: (row_block0 + b * n_tiles + t, 0)),
        pl.BlockSpec((tt, D_CONF), lambda b, t: (row_block0 + b * n_tiles + t, 1)),
        pl.BlockSpec((HIST, D_CONF), lambda b, t: (0, 0)),
        pl.BlockSpec((N_LT, HIST, LANES), lambda b, t: (0, 0, 0)),
        pl.BlockSpec((N_LT, SUBLANES, LANES), lambda b, t: (0, 0, 0)),
    ]
    return pl.pallas_call(
        functools.partial(_conf_tile_kernel, tt=tt, n_valid=n_valid),
        grid=(n_seq, n_tiles),
        in_specs=in_specs,
        out_specs=(
            pl.BlockSpec((tt, D_CONF), lambda b, t: (b * n_tiles + t, 0)),
            pl.BlockSpec((None, HIST, D_CONF), lambda b, t: (b, 0, 0)),
        ),
        out_shape=(
            jax.ShapeDtypeStruct((n_seq * n_tiles * tt, D_CONF), bf16),
            jax.ShapeDtypeStruct((n_seq, HIST, D_CONF), f32),
        ),
        scratch_shapes=[
            pltpu.VMEM((N_LT, HIST + tt, LANES), f32),
            pltpu.VMEM((N_LT, tt, LANES), f32),
        ],
        compiler_params=_cparams(("arbitrary", "arbitrary")),
        name="conf_tiles",
    )(pc, pc, hist, w3, p3)


CP_ROWS = 1024
CP_COLS = 512


def _conf_proj_kernel(u_ref, wca_ref, wcg_ref, hist_ref, w_ref, p_ref, side_in, y_ref, tail_ref, side_out,
                      s_scr, g_scr, y_scr, *, tiles_per_seq):
    i = pl.program_id(0)
    j = pl.program_id(1)
    n_lt = CP_COLS // LANES
    side_out[...] = side_in[...].astype(bf16)

    @pl.when((i == 0) & (j == 0))
    def _():
        g_scr[...] = jnp.zeros(g_scr.shape, f32)
        s_scr[...] = jnp.zeros(s_scr.shape, f32)

    @pl.when(j == 0)
    def _():
        @pl.when((i - 1) % tiles_per_seq == 0)
        def _():
            for lt in range(N_LT):
                s_scr[lt, 0:HIST, :] = hist_ref[:, lt * LANES:(lt + 1) * LANES]

        @pl.when((i - 1) % tiles_per_seq != 0)
        def _():
            s_scr[:, 0:HIST, :] = s_scr[:, CP_ROWS:CP_ROWS + HIST, :]

        s_scr[:, HIST:HIST + CP_ROWS, :] = g_scr[...]

    u = u_ref[...]
    nt = (((1,), (1,)), ((), ()))
    glu = (lax.dot_general(u, wca_ref[...], nt, preferred_element_type=f32)
           * _sigmoid(lax.dot_general(u, wcg_ref[...], nt, preferred_element_type=f32)))
    for q in range(n_lt):
        lt = j * n_lt + q
        _conf_conv_tile(s_scr.at[lt], w_ref, p_ref, lt, y_scr.at[q], CP_ROWS)
        g_scr[lt] = glu[:, q * LANES:(q + 1) * LANES]
        y_ref[:, q * LANES:(q + 1) * LANES] = y_scr[q].astype(bf16)
        tail_ref[:, q * LANES:(q + 1) * LANES] = s_scr[lt, CP_ROWS:CP_ROWS + HIST, :]


def _conf_proj(u, w_t, ca_row0, hist, conf_w, side, *, n_seq, seq_rows):
    w3, p3 = conf_w
    n_tiles = n_seq * seq_rows // CP_ROWS
    tiles_per_seq = seq_rows // CP_ROWS
    grid = (n_tiles + 1, D_CONF // CP_COLS)
    side_in, side_out, side_shape = _side_cast(*side, grid)
    align = 2 * 16
    w_block = lambda row0: pl.BlockSpec(
        (pl.Element(CP_COLS), pl.Element(D_MODEL)), lambda i, j: (pl.multiple_of(row0 + CP_COLS * j, align), 0))
    slot = lambda i: (i + n_tiles) % (n_tiles + 1)
    return pl.pallas_call(
        functools.partial(_conf_proj_kernel, tiles_per_seq=tiles_per_seq),
        grid=grid,
        in_specs=[
            pl.BlockSpec((CP_ROWS, D_MODEL), lambda i, j: (jnp.minimum(i, n_tiles - 1), 0)),
            w_block(ca_row0),
            w_block(ca_row0 + D_CONF),
            pl.BlockSpec((HIST, D_CONF), lambda i, j: (0, 0)),
            pl.BlockSpec((N_LT, HIST, LANES), lambda i, j: (0, 0, 0)),
            pl.BlockSpec((N_LT, SUBLANES, LANES), lambda i, j: (0, 0, 0)),
            side_in,
        ],
        out_specs=(
            pl.BlockSpec((CP_ROWS, CP_COLS), lambda i, j: (slot(i), j)),
            pl.BlockSpec((None, HIST, CP_COLS), lambda i, j: (slot(i), 0, j)),
            side_out,
        ),
        out_shape=(
            jax.ShapeDtypeStruct(((n_tiles + 1) * CP_ROWS, D_CONF), bf16),
            jax.ShapeDtypeStruct((n_tiles + 1, HIST, D_CONF), f32),
            side_shape,
        ),
        scratch_shapes=[
            pltpu.VMEM((N_LT, HIST + CP_ROWS, LANES), f32),
            pltpu.VMEM((N_LT, CP_ROWS, LANES), f32),
            pltpu.VMEM((CP_COLS // LANES, CP_ROWS, LANES), f32),
        ],
        compiler_params=_cparams(("arbitrary", "arbitrary")),
        name="conf_proj",
    )(u, w_t, w_t, hist, w3, p3, side[0])


CONF_BS = 16


def _conf_sample_kernel(ca_ref, cg_ref, buf_ref, w_ref, p_ref, y_ref, newbuf_ref):
    n_hist = CONF_KERNEL - 1
    n_tok = ca_ref.shape[0]
    glu = [ca_ref[t] * _sigmoid(cg_ref[t]) for t in range(n_tok)]
    win = lambda j: buf_ref[j] if j < n_hist else glu[j - n_hist]
    for j in range(n_hist):
        newbuf_ref[j] = win(j + n_tok)
    for t in range(n_tok):
        acc = jnp.broadcast_to(p_ref[0:1, :], (CONF_BS, D_CONF))
        for k in range(CONF_KERNEL):
            acc = acc + w_ref[k:k + 1, :] * win(t + k)
        for lt in range(N_LT):
            sl = slice(lt * LANES, (lt + 1) * LANES)
            y_ref[t, :, sl] = _conf_norm_act(acc[:, sl], p_ref[1:2, sl], p_ref[2:3, sl]).astype(bf16)


def _conf_sample(pc_t, buf_t, w2, p2, *, n_tok, tok_block0):
    n_seq = pc_t.shape[1]
    n_hist = CONF_KERNEL - 1
    seq_blocks = lambda col: (lambda i: (0, i, col))
    return pl.pallas_call(
        _conf_sample_kernel,
        grid=(n_seq // CONF_BS,),
        in_specs=[
            pl.BlockSpec((n_tok, CONF_BS, D_CONF), lambda i: (tok_block0, i, 0)),
            pl.BlockSpec((n_tok, CONF_BS, D_CONF), lambda i: (tok_block0, i, 1)),
            pl.BlockSpec((n_hist, CONF_BS, D_CONF), seq_blocks(0)),
            pl.BlockSpec((CONF_KERNEL, D_CONF), lambda i: (0, 0)),
            pl.BlockSpec((3, D_CONF), lambda i: (0, 0)),
        ],
        out_specs=(
            pl.BlockSpec((n_tok, CONF_BS, D_CONF), seq_blocks(0)),
            pl.BlockSpec((n_hist, CONF_BS, D_CONF), seq_blocks(0)),
        ),
        out_shape=(
            jax.ShapeDtypeStruct((n_tok, n_seq, D_CONF), bf16),
            jax.ShapeDtypeStruct((n_hist, n_seq, D_CONF), f32),
        ),
        compiler_params=_cparams(("arbitrary",)),
        name="conf_sample",
    )(pc_t, pc_t, buf_t, w2, p2)


def _seq_tiles(rows, n_seq, n_tok):
    w = rows.shape[-1]
    return jnp.pad(rows.reshape(n_seq, n_tok, w), ((0, 0), (SEQ_TILE - n_tok, 0), (0, 0))).reshape(n_seq * SEQ_TILE, w)


def kernel(x_prompt, x_sample, state_ssm, state_ssm_conv, state_conf_conv, meta_tokens, ffn1_norm, ffn1_w_gate, ffn1_w_up, ffn1_w_down, mix_norm, w_in, ssm_conv_w, ssm_conv_b, ssm_dt_bias, ssm_A_log, ssm_D, ssm_norm, conf_conv_w, conf_conv_b, conf_ln_g, conf_ln_b, w_out, ffn2_norm, ffn2_w_gate, ffn2_w_up, ffn2_w_down, final_norm):
    n_pb, n_pt = x_prompt.shape[0], x_prompt.shape[1]
    n_sb, n_st = x_sample.shape[0], x_sample.shape[1]
    assert (n_pb * n_pt, n_sb * n_st, n_st) == (N_PROMPT_ROWS, N_SAMPLE_ROWS, 4) and n_pt % CHUNK == 0

    row = lambda v: v.reshape(1, -1)
    swap01 = lambda v: jnp.transpose(v, (1, 0, 2))
    x_rows = (x_prompt.reshape(N_PROMPT_ROWS, D_MODEL), swap01(x_sample).reshape(N_SAMPLE_ROWS, D_MODEL), meta_tokens)
    h, u, w_t, wg2 = _ffn(x_rows, row(ffn1_norm[0]), _to_bf16(ffn1_w_gate, 256), _to_bf16(ffn1_w_up, 256),
                          _to_bf16(ffn1_w_down, 704), row(mix_norm[0]),
                          sides=((jnp.transpose(w_in[0]), 96), (ffn2_w_gate, 16)))

    c_in0 = D_SSM + CONV_DIM
    n_small = N_ROWS - ROW_SAMPLE
    small = dict(tm=n_small, a_rows=(ROW_SAMPLE, n_small))
    pzx = _matmul_nt(u, w_t, row0=0, n=c_in0, tn=1536, **small)
    pdt = _matmul_nt(u, w_t, row0=c_in0, n=LANES, tn=LANES, **small)
    pc = _matmul_nt(u, w_t, row0=c_in0 + N_HEADS, n=2 * D_CONF, tn=1024, **small)

    pad_heads = lambda v: jnp.pad(v.reshape(1, N_HEADS), ((0, 0), (0, LANES - N_HEADS)))
    ssm_w = (ssm_conv_w[0], row(ssm_conv_b[0]), pad_heads(ssm_dt_bias[0]), pad_heads(ssm_A_log[0]),
             row(jnp.repeat(ssm_D[0], HEAD_DIM)), row(ssm_norm[0]))
    tri = jnp.tril(jnp.ones((CHUNK, CHUNK), f32))
    e_mat = (jnp.arange(LANES)[:, None] == (jnp.arange(D_SSM)[None, :] // HEAD_DIM)).astype(bf16)
    lane_major = lambda v: v.reshape(v.shape[0], N_LT, LANES).transpose(1, 0, 2)
    conf_w = (lane_major(jnp.pad(conf_conv_w[0], ((0, HIST - CONF_KERNEL), (0, 0)))),
              lane_major(jnp.pad(jnp.stack([conf_conv_b[0], conf_ln_g[0], conf_ln_b[0]]), ((0, SUBLANES - 3), (0, 0)))))
    out_rows = jnp.arange(SSD_BS * 4)
    sel = ((out_rows // 4) * SEQ_TILE + TOK0 + out_rows % 4)[:, None] == jnp.arange(SSD_BS * SEQ_TILE)[None, :]
    sel = sel.astype(bf16)

    meta_block = (ROW_META - ROW_SAMPLE) // CHUNK
    ys_m, h_meta, tail_meta = _ssd_chunks(
        pzx, pdt, jnp.zeros((SUBLANES, CONV_DIM), f32), jnp.zeros((D_STATE, D_SSM), f32), ssm_w, (tri, e_mat),
        n_seq=1, n_chunks=1, row_block0=meta_block, n_valid=N_META, transpose_out=False)
    yc_m, hist_meta = _conf_tiles(pc, jnp.zeros((HIST, D_CONF), f32), conf_w, n_seq=1, n_tiles=1, tt=CHUNK,
                                  row_block0=meta_block, n_valid=N_META)

    ys_p, p_ssm, p_tail, wo = _ssd_proj(u, w_t, tail_meta[0], h_meta[0], ssm_w, (tri, e_mat), (w_out, 64),
                                        n_seq=n_pb, chunks_per_seq=n_pt // CHUNK)
    p_ssm, p_tail = p_ssm[:n_pb], p_tail[:n_pb]
    yc_p, tile_hist, wu2 = _conf_proj(u, w_t, c_in0 + N_HEADS, hist_meta[0], conf_w, (ffn2_w_up, 64),
                                      n_seq=n_pb, seq_rows=n_pt)
    tiles_per_seq = n_pt // CP_ROWS
    p_hist = tile_hist[tiles_per_seq - 1:n_pb * tiles_per_seq:tiles_per_seq]

    smp = slice(0, N_SAMPLE_ROWS)
    seq_major = lambda v: swap01(v.reshape(n_st, n_sb, v.shape[-1]))
    pzx_s = seq_major(pzx[smp])
    pxbc_s = pzx_s[:, :, D_SSM:]
    pxbc8 = jnp.concatenate([jnp.zeros((n_sb, 1, CONV_DIM), f32), state_ssm_conv[0], pxbc_s], axis=1)
    ys_s, s_ssm = _ssd_sample(_seq_tiles(pzx_s[:, :, :D_SSM], n_sb, n_st), pxbc8.reshape(n_sb * SEQ_TILE, CONV_DIM),
                              _seq_tiles(seq_major(pdt[smp]), n_sb, n_st), state_ssm[0].reshape(n_sb, D_SSM, D_STATE),
                              ssm_w, e_mat, sel)
    ys_s = swap01(ys_s.reshape(n_sb, n_st, D_SSM)).reshape(N_SAMPLE_ROWS, D_SSM)
    yc_t, cconv_t = _conf_sample(pc.reshape(n_small // n_sb, n_sb, 2 * D_CONF), swap01(state_conf_conv[0]),
                                 conf_conv_w[0], jnp.stack([conf_conv_b[0], conf_ln_g[0], conf_ln_b[0]]),
                                 n_tok=n_st, tok_block0=0)
    yc_s = yc_t.reshape(N_SAMPLE_ROWS, D_CONF)
    s_cconv = swap01(cconv_t)

    h2, wd2 = _outproj(h, (ys_p, ys_s, ys_m), (yc_p, yc_s, yc_m), wo, main_rows=N_PROMPT_ROWS, tm=TM_PROJ, tn=512,
                       side=(ffn2_w_down, 256))
    y_prompt, y_sample = _ffn((h2,), row(ffn2_norm[0]), wg2, wu2, wd2, row(final_norm),
                              dst_rows=(N_PROMPT_ROWS, N_SAMPLE_ROWS))
    n_hist = CONF_KERNEL - 1
    return (
        y_prompt.reshape(n_pb, n_pt, D_MODEL), swap01(y_sample.reshape(n_st, n_sb, D_MODEL)),
        p_ssm.reshape(1, n_pb, N_HEADS, HEAD_DIM, D_STATE),
        p_tail[:, SUBLANES - (SSM_CONV - 1):, :][None],
        p_hist[:, HIST - n_hist:, :][None],
        s_ssm.reshape(1, n_sb, N_HEADS, HEAD_DIM, D_STATE),
        pxbc_s[:, n_st - (SSM_CONV - 1):, :][None],
        s_cconv[None],
    )
```

```python
import functools

import jax
import jax.numpy as jnp
from jax import lax
from jax.experimental import pallas as pl
from jax.experimental.pallas import tpu as pltpu

f32 = jnp.float32
bf16 = jnp.bfloat16

D_MODEL = 2048
N_META = 16
D_SSM = 2048
D_CONF = 2048
HEAD_DIM = 64
N_HEADS = 32
N_GROUPS = 2
D_STATE = 128
SSM_CONV = 4
CONV_DIM = D_SSM + 2 * N_GROUPS * D_STATE
CONF_KERNEL = 31
CONF_GROUP = 128
D_FF = 5632
EPS = 1e-5
LOG2E = 1.4426950408889634

LANES = 128
SUBLANES = 8
CHUNK = 128
VMEM_LIMIT = 60 * 1024 * 1024

N_PROMPT_ROWS = 4 * 2048
N_SAMPLE_ROWS = 128 * 4
ROW_SAMPLE = N_PROMPT_ROWS
ROW_META = N_PROMPT_ROWS + N_SAMPLE_ROWS
N_ROWS = ROW_META + CHUNK
TM = 736
TM_PROJ = 1104
TF = 512


def _cparams(sem):
    return pltpu.CompilerParams(dimension_semantics=sem, vmem_limit_bytes=VMEM_LIMIT)


def _sigmoid(x):
    return 1.0 / (1.0 + jnp.exp(-x))


def _silu(x):
    return x * _sigmoid(x)


def _rms(x, g):
    return x * lax.rsqrt(jnp.mean(x * x, axis=-1, keepdims=True) + EPS) * g


def _softplus(x):
    return jnp.maximum(x, 0.0) + jnp.log1p(jnp.exp(-jnp.abs(x)))


def _split3(v):
    hi = v.astype(bf16).astype(f32)
    r1 = v - hi
    mid = r1.astype(bf16).astype(f32)
    return hi, mid, r1 - mid


def _row_segments(src_rows, n_total, tm):
    bounds, start = [], 0
    for k, n in enumerate(src_rows):
        bounds.append((k, start, n))
        start += n
    if start < n_total:
        bounds.append((None, start, n_total - start))
    table = []
    for t in range(n_total // tm):
        lo, hi = t * tm, (t + 1) * tm
        segs = []
        for k, s, n in bounds:
            a, b = max(lo, s), min(hi, s + n)
            if a < b:
                segs.append((k, a - s, a - lo, b - a))
        table.append(segs)
    return table


def _load_tile(i, seg_table, groups, sem):
    tm = groups[0][2].shape[0]
    special = [t for t, segs in enumerate(seg_table) if segs != [(0, t * tm, 0, tm)]]
    first = special[0] if special else len(seg_table)
    assert special == list(range(first, len(seg_table)))

    @pl.when(i < first)
    def _():
        for main_ref, _, dst_ref in groups:
            dst_ref[...] = main_ref[...]

    for t in special:
        @pl.when(i == t)
        def _(segs=seg_table[t]):
            copies = []
            for main_ref, tail_srcs, dst_ref in groups:
                for k, r0, d0, n in segs:
                    if k is None:
                        dst_ref[d0:d0 + n, :] = jnp.zeros((n, dst_ref.shape[1]), dst_ref.dtype)
                    elif k == 0:
                        dst_ref[d0:d0 + n, :] = main_ref[d0:d0 + n, :]
                    else:
                        cp = pltpu.make_async_copy(tail_srcs[k - 1].at[pl.ds(r0, n)], dst_ref.at[pl.ds(d0, n)], sem)
                        cp.start()
                        copies.append(cp)
            for cp in copies:
                cp.wait()


def _side_cast(w, block_rows, grid):
    r, c = w.shape[-2:]
    n_blocks = r // block_rows
    assert r % block_rows == 0 and n_blocks <= grid[0] * grid[1]
    blk = lambda i, j: jnp.minimum(i * grid[1] + j, n_blocks - 1)
    if w.ndim == 3:
        in_spec = pl.BlockSpec((None, block_rows, c), lambda i, j: (0, blk(i, j), 0))
    else:
        in_spec = pl.BlockSpec((block_rows, c), lambda i, j: (blk(i, j), 0))
    return in_spec, pl.BlockSpec((block_rows, c), lambda i, j: (blk(i, j), 0)), jax.ShapeDtypeStruct((r, c), bf16)


def _run_side_cast(side_refs):
    for src_ref, dst_ref in side_refs:
        dst_ref[...] = src_ref[...].astype(bf16)


def _drain_rows(i, out_table, dsts, acc_scr, sems):
    def copies(t):
        return [pltpu.make_async_copy(acc_scr.at[t % 2, pl.ds(s0, n)], dsts[k].at[pl.ds(r0, n)], sems.at[t % 2])
                for k, r0, s0, n in out_table[t] if k is not None]

    n_tiles = len(out_table)
    for t in range(n_tiles):
        @pl.when(i == t)
        def _(t=t):
            if t > 0:
                for cp in copies(t - 1):
                    cp.wait()
            for cp in copies(t):
                cp.start()
            if t == n_tiles - 1:
                for cp in copies(t):
                    cp.wait()


def _ffn_kernel(*refs, seg_table, out_table, n_src, n_side):
    x_ref, tail_srcs = refs[0], refs[1:n_src]
    nw_ref, wg_ref, wu_ref, wd_ref, pw_ref = refs[n_src:n_src + 5]
    pos = n_src + 5
    side_in, pos = refs[pos:pos + n_side], pos + n_side
    i = pl.program_id(0)
    f = pl.program_id(1)
    if out_table is None:
        (acc, yn_ref), pos = refs[pos:pos + 2], pos + 2
        side_out, pos = refs[pos:pos + n_side], pos + n_side
        xn_ref, sem = refs[pos:]
    else:
        n_dst = 1 + max(k for segs in out_table for k, _, _, _ in segs if k is not None)
        dsts, pos = refs[pos:pos + n_dst], pos + n_dst
        side_out, pos = refs[pos:pos + n_side], pos + n_side
        acc_scr, xn_ref, sem, out_sems = refs[pos:]
        acc = acc_scr.at[i % 2]
    _run_side_cast(zip(side_in, side_out))

    @pl.when(f == 0)
    def _():
        _load_tile(i, seg_table, [(x_ref, tail_srcs, acc)], sem)
        xn_ref[...] = _rms(acc[...], nw_ref[...]).astype(bf16)

    xn = xn_ref[...]
    g = jnp.dot(xn, wg_ref[...], preferred_element_type=f32)
    u = jnp.dot(xn, wu_ref[...], preferred_element_type=f32)
    a = (_silu(g) * u).astype(bf16)
    acc[...] += 0.5 * jnp.dot(a, wd_ref[...], preferred_element_type=f32)

    @pl.when(f == pl.num_programs(1) - 1)
    def _():
        if out_table is None:
            yn_ref[...] = _rms(acc[...], pw_ref[...]).astype(yn_ref.dtype)
        else:
            acc[...] = _rms(acc[...], pw_ref[...])
            _drain_rows(i, out_table, dsts, acc_scr, out_sems)


def _ffn(srcs, norm_w, wg, wu, wd, post_w, *, dst_rows=None, sides=(), tf=TF):
    seg_table = _row_segments([s.shape[0] for s in srcs], N_ROWS, TM)
    grid = (N_ROWS // TM, D_FF // tf)
    side_specs = [_side_cast(*s, grid) for s in sides]
    row_spec = pl.BlockSpec((TM, D_MODEL), lambda i, f: (i, 0))
    vec_spec = pl.BlockSpec((1, D_MODEL), lambda i, f: (0, 0))
    scratch = [pltpu.VMEM((TM, D_MODEL), bf16), pltpu.SemaphoreType.DMA(())]
    if dst_rows is None:
        out_table = None
        out_shape = (jax.ShapeDtypeStruct((N_ROWS, D_MODEL), f32), jax.ShapeDtypeStruct((N_ROWS, D_MODEL), bf16))
        out_specs = (row_spec, row_spec)
    else:
        out_table = _row_segments(dst_rows, N_ROWS, TM)
        out_shape = tuple(jax.ShapeDtypeStruct((n, D_MODEL), f32) for n in dst_rows)
        out_specs = tuple(pl.BlockSpec(memory_space=pl.ANY) for _ in dst_rows)
        scratch = [pltpu.VMEM((2, TM, D_MODEL), f32)] + scratch + [pltpu.SemaphoreType.DMA((2,))]
    return pl.pallas_call(
        functools.partial(_ffn_kernel, seg_table=seg_table, out_table=out_table, n_src=len(srcs),
                          n_side=len(side_specs)),
        grid=grid,
        in_specs=[row_spec] + [pl.BlockSpec(memory_space=pl.ANY)] * (len(srcs) - 1) + [
            vec_spec,
            pl.BlockSpec((D_MODEL, tf), lambda i, f: (0, f)),
            pl.BlockSpec((D_MODEL, tf), lambda i, f: (0, f)),
            pl.BlockSpec((tf, D_MODEL), lambda i, f: (f, 0)),
            vec_spec,
        ] + [s[0] for s in side_specs],
        out_specs=tuple(out_specs) + tuple(s[1] for s in side_specs),
        out_shape=tuple(out_shape) + tuple(s[2] for s in side_specs),
        scratch_shapes=scratch,
        compiler_params=_cparams(("arbitrary", "arbitrary")),
        name="ffn",
    )(*srcs, norm_w, wg, wu, wd, post_w, *[s[0] for s in sides])


def _cast_kernel(x_ref, o_ref):
    o_ref[...] = x_ref[...].astype(bf16)


def _to_bf16(w, block_rows):
    r, c = w.shape[-2:]
    if w.ndim == 3:
        in_spec = pl.BlockSpec((None, block_rows, c), lambda i: (0, i, 0))
    else:
        in_spec = pl.BlockSpec((block_rows, c), lambda i: (i, 0))
    return pl.pallas_call(
        _cast_kernel,
        grid=(r // block_rows,),
        in_specs=[in_spec],
        out_specs=pl.BlockSpec((block_rows, c), lambda i: (i, 0)),
        out_shape=jax.ShapeDtypeStruct((r, c), bf16),
        compiler_params=_cparams(("arbitrary",)),
        name="cast_bf16",
    )(w)


def _matmul_nt_kernel(a_ref, w_ref, *rest):
    n_side = len(rest) // 2
    o_ref = rest[n_side]
    _run_side_cast(zip(rest[:n_side], rest[n_side + 1:]))
    o_ref[...] = lax.dot_general(a_ref[...], w_ref[...], (((1,), (1,)), ((), ())), preferred_element_type=f32)


def _matmul_nt(a, w_t, *, row0, n, tn, tm=TM, a_rows=None, side=None):
    k = a.shape[1]
    a_row0, n_rows = a_rows if a_rows else (0, a.shape[0])
    align = 16
    assert row0 % align == 0 and tn % align == 0 and n % tn == 0 and n_rows % tm == 0 and a_row0 % align == 0
    grid = (n_rows // tm, n // tn)
    side_specs = [_side_cast(*side, grid)] if side else []
    out = pl.pallas_call(
        _matmul_nt_kernel,
        grid=grid,
        in_specs=[
            pl.BlockSpec((pl.Element(tm), pl.Element(k)), lambda i, j: (pl.multiple_of(a_row0 + tm * i, align), 0)),
            pl.BlockSpec((pl.Element(tn), pl.Element(k)), lambda i, j: (pl.multiple_of(row0 + tn * j, align), 0)),
        ] + [s[0] for s in side_specs],
        out_specs=(pl.BlockSpec((tm, tn), lambda i, j: (i, j)),) + tuple(s[1] for s in side_specs),
        out_shape=(jax.ShapeDtypeStruct((n_rows, n), f32),) + tuple(s[2] for s in side_specs),
        compiler_params=_cparams(("arbitrary", "arbitrary")),
        name="in_proj",
    )(a, w_t, *([side[0]] if side else []))
    return out if side else out[0]


def _outproj_kernel(*refs, seg_table, n_src, n_side):
    ys_ref, ys_tails = refs[0], refs[1:n_src]
    yc_ref, yc_tails = refs[n_src], refs[n_src + 1:2 * n_src]
    h_ref, w0_ref, w1_ref = refs[2 * n_src:2 * n_src + 3]
    pos = 2 * n_src + 3
    side_in, o_ref, side_out = refs[pos:pos + n_side], refs[pos + n_side], refs[pos + n_side + 1:pos + 2 * n_side + 1]
    ys_scr, yc_scr, sem = refs[pos + 2 * n_side + 1:]
    _run_side_cast(zip(side_in, side_out))

    @pl.when(pl.program_id(1) == 0)
    def _():
        _load_tile(pl.program_id(0), seg_table, [(ys_ref, ys_tails, ys_scr), (yc_ref, yc_tails, yc_scr)], sem)

    acc = jnp.dot(ys_scr[...], w0_ref[...], preferred_element_type=f32)
    acc = acc + jnp.dot(yc_scr[...], w1_ref[...], preferred_element_type=f32)
    o_ref[...] = h_ref[...] + acc


def _outproj(h, ys_srcs, yc_srcs, w_out, *, main_rows, tm, tn, side=None):
    n_rows = h.shape[0]
    assert n_rows % tm == 0
    seg_table = _row_segments([main_rows] + [s.shape[0] for s in ys_srcs[1:]], n_rows, tm)
    n_src = len(ys_srcs)
    grid = (n_rows // tm, D_MODEL // tn)
    side_specs = [_side_cast(*side, grid)] if side else []
    y_specs = [pl.BlockSpec((tm, D_SSM), lambda i, j: (i, 0))] + [pl.BlockSpec(memory_space=pl.ANY)] * (n_src - 1)
    out = pl.pallas_call(
        functools.partial(_outproj_kernel, seg_table=seg_table, n_src=n_src, n_side=len(side_specs)),
        grid=grid,
        in_specs=y_specs + y_specs + [
            pl.BlockSpec((tm, tn), lambda i, j: (i, j)),
            pl.BlockSpec((D_SSM, tn), lambda i, j: (0, j)),
            pl.BlockSpec((D_CONF, tn), lambda i, j: (1, j)),
        ] + [s[0] for s in side_specs],
        out_specs=(pl.BlockSpec((tm, tn), lambda i, j: (i, j)),) + tuple(s[1] for s in side_specs),
        out_shape=(jax.ShapeDtypeStruct((n_rows, D_MODEL), f32),) + tuple(s[2] for s in side_specs),
        scratch_shapes=[pltpu.VMEM((tm, D_SSM), bf16), pltpu.VMEM((tm, D_CONF), bf16), pltpu.SemaphoreType.DMA(())],
        compiler_params=_cparams(("arbitrary", "arbitrary")),
        name="out_proj",
    )(*ys_srcs, *yc_srcs, h, w_out, w_out, *([side[0]] if side else []))
    return out if side else out[0]


def _gate_norm(y, z, nw):
    y = y * _silu(z)
    half = D_SSM // N_GROUPS
    parts = []
    for g in range(N_GROUPS):
        yg = y[:, g * half:(g + 1) * half]
        ms = jnp.mean(yg * yg, axis=-1, keepdims=True)
        parts.append(yg * lax.rsqrt(ms + EPS) * nw[:, g * half:(g + 1) * half])
    return jnp.concatenate(parts, axis=1)


def _ssd_chunk_math(z, xbc, dt_raw, cinit_ref, hinit_ref, cw_ref, cb_ref, dtb_ref, alog_ref,
                    dx_ref, nw_ref, tri_ref, e_ref, y_ref, hout_ref, tail_ref, win_scr, ht_scr, y_scr,
                    *, n_valid, transpose_out, first=None, last=None, side_work=None):
    if first is None:
        first = pl.program_id(1) == 0
        last = pl.program_id(1) == pl.num_programs(1) - 1

    @pl.when(first)
    def _():
        ht_scr[...] = hinit_ref[...]
        win_scr[0:SUBLANES, :] = cinit_ref[...]

    win_scr[SUBLANES:SUBLANES + CHUNK, :] = xbc
    acc = cb_ref[...] + cw_ref[SSM_CONV - 1:SSM_CONV, :] * xbc
    for j in range(1, SSM_CONV):
        acc = acc + cw_ref[SSM_CONV - 1 - j:SSM_CONV - j, :] * win_scr[SUBLANES - j:SUBLANES - j + CHUNK, :]
    tail_ref[...] = win_scr[n_valid:n_valid + SUBLANES, :]
    win_scr[0:SUBLANES, :] = win_scr[CHUNK:CHUNK + SUBLANES, :]
    xc = _silu(acc)
    b_mats = [xc[:, D_SSM + g * D_STATE:D_SSM + (g + 1) * D_STATE] for g in range(N_GROUPS)]
    c_mats = [xc[:, D_SSM + (N_GROUPS + g) * D_STATE:D_SSM + (N_GROUPS + g + 1) * D_STATE] for g in range(N_GROUPS)]

    rows = lax.broadcasted_iota(jnp.int32, (CHUNK, LANES), 0)
    lanes = lax.broadcasted_iota(jnp.int32, (CHUNK, LANES), 1)
    dt = _softplus(dt_raw + dtb_ref[...])
    if n_valid < CHUNK:
        dt = jnp.where(rows < n_valid, dt, 0.0)
    a = dt * (-jnp.exp(alog_ref[...]))
    acs = jnp.dot(tri_ref[...], a, precision=lax.Precision.HIGHEST, preferred_element_type=f32) * LOG2E
    total = acs[CHUNK - 1:CHUNK, :]
    dend = jnp.exp2(total - acs)
    acs_t = acs.T
    dt_t = dt.T
    w_t = (dt * dend).T
    hi, mid, lo = _split3(jnp.exp2(total))
    rid = lax.broadcasted_iota(jnp.int32, (SUBLANES, LANES), 0)
    cd3 = jnp.where(rid == 0, hi, jnp.where(rid == 1, mid, jnp.where(rid == 2, lo, 0.0)))
    cdx = jnp.sum(jnp.dot(cd3.astype(bf16), e_ref[...], preferred_element_type=f32), axis=0, keepdims=True)

    cb_mats, bt_mats = [], []
    for g in range(N_GROUPS):
        cb_mats.append(lax.dot_general(c_mats[g].astype(bf16), b_mats[g].astype(bf16), (((1,), (1,)), ((), ())),
                                       preferred_element_type=f32))
        bt_mats.append(b_mats[g].T)

    lane_lo = lanes < HEAD_DIM
    tril = rows >= lanes
    heads_per_group = N_HEADS // N_GROUPS
    for j in range(N_HEADS // 2):
        if side_work is not None:
            side_work(j)
        sl = slice(j * LANES, (j + 1) * LANES)
        xs_pair = xc[:, sl]
        ht_pair = ht_scr[:, sl]
        y_pair = jnp.zeros((CHUNK, LANES), f32)
        upd = jnp.zeros((D_STATE, LANES), f32)
        for half in range(2):
            h = 2 * j + half
            g = h // heads_per_group
            keep = lane_lo if half == 0 else jnp.logical_not(lane_lo)
            xs_m = jnp.where(keep, xs_pair, 0.0).astype(bf16)
            ht_m = jnp.where(keep, ht_pair, 0.0).astype(bf16)
            acs_col = jnp.broadcast_to(acs[:, h:h + 1], (CHUNK, CHUNK))
            acs_row = jnp.broadcast_to(acs_t[h:h + 1, :], (CHUNK, CHUNK))
            lmat = jnp.exp2(jnp.where(tril, acs_col - acs_row, -jnp.inf))
            m_h = cb_mats[g] * lmat * dt_t[h:h + 1, :]
            ce_h = c_mats[g] * jnp.exp2(acs_col)
            lhs = jnp.concatenate([m_h, ce_h], axis=1).astype(bf16)
            rhs = jnp.concatenate([xs_m, ht_m], axis=0)
            y_pair = y_pair + jnp.dot(lhs, rhs, preferred_element_type=f32)
            bw = (bt_mats[g] * w_t[h:h + 1, :]).astype(bf16)
            upd = upd + jnp.dot(bw, xs_m, preferred_element_type=f32)
        y_scr[:, sl] = y_pair
        ht_scr[:, sl] = ht_pair * cdx[:, sl] + upd

    xs = xc[:, :D_SSM]
    y = y_scr[...] + xs * dx_ref[...]
    y_ref[...] = _gate_norm(y, z(), nw_ref[...]).astype(bf16)

    @pl.when(last)
    def _():
        if transpose_out:
            hout_ref[...] = ht_scr[...].T
        else:
            hout_ref[...] = ht_scr[...]


N_SSD_PARAMS = 10


def _ssd_chunk_kernel(z_ref, xs_ref, bc_ref, dt_ref, *rest, n_valid, transpose_out, n_side):
    params, rest = rest[:N_SSD_PARAMS], rest[N_SSD_PARAMS:]
    side_in, outs = rest[:n_side], rest[n_side:n_side + 3]
    side_out, scratch = rest[n_side + 3:2 * n_side + 3], rest[2 * n_side + 3:]
    _run_side_cast(zip(side_in, side_out))
    xbc = jnp.concatenate([xs_ref[...], bc_ref[...]], axis=1)
    _ssd_chunk_math(lambda: z_ref[...], xbc, dt_ref[...], *params, *outs, *scratch,
                    n_valid=n_valid, transpose_out=transpose_out)


def _ssd_chunks(pzx, pdt, cinit, hinit, ssm_w, consts, *, n_seq, n_chunks, row_block0, n_valid, transpose_out,
                side=None):
    cw, cb, dtb, alog, dx, nw = ssm_w
    tri, e_mat = consts
    bc_dim = CONV_DIM - D_SSM
    side_specs = [_side_cast(*side, (n_seq, n_chunks))] if side else []
    rows_map = lambda b, c: (row_block0 + b * n_chunks + c, 0)
    fixed = lambda b, c: (0, 0)
    in_specs = [
        pl.BlockSpec((CHUNK, D_SSM), rows_map),
        pl.BlockSpec((CHUNK, D_SSM), lambda b, c: (row_block0 + b * n_chunks + c, 1)),
        pl.BlockSpec((CHUNK, bc_dim), lambda b, c: (row_block0 + b * n_chunks + c, 2 * D_SSM // bc_dim)),
        pl.BlockSpec((CHUNK, LANES), rows_map),
        pl.BlockSpec((SUBLANES, CONV_DIM), fixed),
        pl.BlockSpec((D_STATE, D_SSM), fixed),
        pl.BlockSpec((SSM_CONV, CONV_DIM), fixed),
        pl.BlockSpec((1, CONV_DIM), fixed),
        pl.BlockSpec((1, LANES), fixed),
        pl.BlockSpec((1, LANES), fixed),
        pl.BlockSpec((1, D_SSM), fixed),
        pl.BlockSpec((1, D_SSM), fixed),
        pl.BlockSpec((CHUNK, CHUNK), fixed),
        pl.BlockSpec((LANES, D_SSM), fixed),
    ]
    args = [pzx, pzx, pzx, pdt, cinit, hinit, cw, cb, dtb, alog, dx, nw, tri, e_mat] + ([side[0]] if side else [])
    hshape = (n_seq, D_SSM, D_STATE) if transpose_out else (n_seq, D_STATE, D_SSM)
    return pl.pallas_call(
        functools.partial(_ssd_chunk_kernel, n_valid=n_valid, transpose_out=transpose_out, n_side=len(side_specs)),
        grid=(n_seq, n_chunks),
        in_specs=in_specs + [s[0] for s in side_specs],
        out_specs=(
            pl.BlockSpec((CHUNK, D_SSM), lambda b, c: (b * n_chunks + c, 0)),
            pl.BlockSpec((None,) + hshape[1:], lambda b, c: (b, 0, 0)),
            pl.BlockSpec((None, SUBLANES, CONV_DIM), lambda b, c: (b, 0, 0)),
        ) + tuple(s[1] for s in side_specs),
        out_shape=(
            jax.ShapeDtypeStruct((n_seq * n_chunks * CHUNK, D_SSM), bf16),
            jax.ShapeDtypeStruct(hshape, f32),
            jax.ShapeDtypeStruct((n_seq, SUBLANES, CONV_DIM), f32),
        ) + tuple(s[2] for s in side_specs),
        scratch_shapes=[
            pltpu.VMEM((SUBLANES + CHUNK, CONV_DIM), f32),
            pltpu.VMEM((D_STATE, D_SSM), f32),
            pltpu.VMEM((CHUNK, D_SSM), f32),
        ],
        compiler_params=_cparams(("arbitrary", "arbitrary")),
        name="ssd_chunks",
    )(*args)


SP_CHUNKS = 4
SP_ROWS = SP_CHUNKS * CHUNK
SP_COLS = 1280


def _ssd_proj_kernel(u_ref, w_ref, *rest, chunks_per_seq):
    params, rest = rest[:N_SSD_PARAMS], rest[N_SSD_PARAMS:]
    side_in, y_ref, hout_ref, tail_ref, side_out, win_scr, ht_scr, y_scr, p_cur, p_next = rest
    i = pl.program_id(0)
    j = pl.program_id(1)
    side_out[...] = side_in[...].astype(bf16)

    @pl.when((i == 0) & (j == 0))
    def _():
        p_next[...] = jnp.zeros(p_next.shape, f32)
        ht_scr[...] = jnp.zeros(ht_scr.shape, f32)
        win_scr[...] = jnp.zeros(win_scr.shape, f32)

    @pl.when(j == 0)
    def _():
        p_cur[...] = p_next[...]

    piece = 2 * LANES
    n_pieces = SP_COLS // piece
    n_pairs = N_HEADS // 2

    def side_work(pair):
        if pair % (n_pairs // n_pieces) == 0 and pair // (n_pairs // n_pieces) < n_pieces:
            q = pair // (n_pairs // n_pieces)
            p_next[j, :, q * piece:(q + 1) * piece] = lax.dot_general(
                u_ref[...], w_ref[q * piece:(q + 1) * piece, :], (((1,), (1,)), ((), ())), preferred_element_type=f32)

    rows = pl.ds(pl.multiple_of(j * CHUNK, CHUNK), CHUNK)

    def cols(lo, hi):
        parts = []
        for k in range(SP_CHUNKS):
            a, b = max(lo, k * SP_COLS), min(hi, (k + 1) * SP_COLS)
            if a < b:
                parts.append(p_cur[k, rows, a - k * SP_COLS:b - k * SP_COLS])
        return parts[0] if len(parts) == 1 else jnp.concatenate(parts, axis=1)

    c = ((i - 1) % (chunks_per_seq // SP_CHUNKS)) * SP_CHUNKS + j
    _ssd_chunk_math(lambda: cols(0, D_SSM), cols(D_SSM, D_SSM + CONV_DIM),
                    cols(D_SSM + CONV_DIM, D_SSM + CONV_DIM + LANES), *params, y_ref, hout_ref, tail_ref,
                    win_scr, ht_scr, y_scr, n_valid=CHUNK, transpose_out=True,
                    first=(c == 0) & (i > 0), last=c == chunks_per_seq - 1, side_work=side_work)


def _ssd_proj(u, w_t, cinit, hinit, ssm_w, consts, side, *, n_seq, chunks_per_seq):
    cw, cb, dtb, alog, dx, nw = ssm_w
    tri, e_mat = consts
    tiles_per_seq = chunks_per_seq // SP_CHUNKS
    n_tiles = n_seq * tiles_per_seq
    grid = (n_tiles + 1, SP_CHUNKS)
    side_in, side_out, side_shape = _side_cast(*side, grid)
    fixed = lambda i, j: (0, 0)
    slot = lambda i: (i + n_tiles) % (n_tiles + 1)
    seq_slot = lambda i: jnp.where(i == 0, n_seq, (i - 1) // tiles_per_seq)
    return pl.pallas_call(
        functools.partial(_ssd_proj_kernel, chunks_per_seq=chunks_per_seq),
        grid=grid,
        in_specs=[
            pl.BlockSpec((SP_ROWS, D_MODEL), lambda i, j: (jnp.minimum(i, n_tiles - 1), 0)),
            pl.BlockSpec((SP_COLS, D_MODEL), lambda i, j: (j, 0)),
            pl.BlockSpec((SUBLANES, CONV_DIM), fixed),
            pl.BlockSpec((D_STATE, D_SSM), fixed),
            pl.BlockSpec((SSM_CONV, CONV_DIM), fixed),
            pl.BlockSpec((1, CONV_DIM), fixed),
            pl.BlockSpec((1, LANES), fixed),
            pl.BlockSpec((1, LANES), fixed),
            pl.BlockSpec((1, D_SSM), fixed),
            pl.BlockSpec((1, D_SSM), fixed),
            pl.BlockSpec((CHUNK, CHUNK), fixed),
            pl.BlockSpec((LANES, D_SSM), fixed),
            side_in,
        ],
        out_specs=(
            pl.BlockSpec((CHUNK, D_SSM), lambda i, j: (slot(i) * SP_CHUNKS + j, 0)),
            pl.BlockSpec((None, D_SSM, D_STATE), lambda i, j: (seq_slot(i), 0, 0)),
            pl.BlockSpec((None, SUBLANES, CONV_DIM), lambda i, j: (seq_slot(i), 0, 0)),
            side_out,
        ),
        out_shape=(
            jax.ShapeDtypeStruct(((n_tiles + 1) * SP_ROWS, D_SSM), bf16),
            jax.ShapeDtypeStruct((n_seq + 1, D_SSM, D_STATE), f32),
            jax.ShapeDtypeStruct((n_seq + 1, SUBLANES, CONV_DIM), f32),
            side_shape,
        ),
        scratch_shapes=[
            pltpu.VMEM((SUBLANES + CHUNK, CONV_DIM), f32),
            pltpu.VMEM((D_STATE, D_SSM), f32),
            pltpu.VMEM((CHUNK, D_SSM), f32),
            pltpu.VMEM((SP_CHUNKS, SP_ROWS, SP_COLS), f32),
            pltpu.VMEM((SP_CHUNKS, SP_ROWS, SP_COLS), f32),
        ],
        compiler_params=_cparams(("arbitrary", "arbitrary")),
        name="ssd_proj",
    )(u, w_t, cinit, hinit, cw, cb, dtb, alog, dx, nw, tri, e_mat, side[0])


SEQ_TILE = SUBLANES
TOK0 = SEQ_TILE - 4
SSD_BS = 8


def _ssd_sample_kernel(z_ref, xbc_ref, dt_ref, h0_ref, cw_ref, cb_ref, dtb_ref, alog_ref, dx_ref, nw_ref,
                       e_ref, sel_ref, y_ref, hnew_ref, yoff_scr, lhs_scr, rhs_scr):
    n_rows = SSD_BS * SEQ_TILE
    half = D_SSM // N_GROUPS
    x8 = xbc_ref[...]
    acc = cb_ref[...] + cw_ref[SSM_CONV - 1:SSM_CONV, :] * x8
    for j in range(1, SSM_CONV):
        acc = acc + cw_ref[SSM_CONV - 1 - j:SSM_CONV - j, :] * pltpu.roll(x8, j, axis=0)
    xc = _silu(acc)
    xs = xc[:, :D_SSM]
    bm = xc[:, D_SSM:D_SSM + N_GROUPS * D_STATE]
    cm = xc[:, D_SSM + N_GROUPS * D_STATE:]

    pos = lax.broadcasted_iota(jnp.int32, (n_rows, LANES), 0) % SEQ_TILE
    lanes = lax.broadcasted_iota(jnp.int32, (n_rows, LANES), 1)
    tok = pos >= TOK0
    dt = jnp.where(tok, _softplus(dt_ref[...] + dtb_ref[...]), 0.0)
    a = dt * (-jnp.exp(alog_ref[...]))
    acs = a
    for d in range(1, 4):
        acs = acs + pltpu.roll(a, d, axis=0)
    tot = jnp.sum(a.reshape(SSD_BS, SEQ_TILE, LANES), axis=1, keepdims=True)
    tot = jnp.broadcast_to(tot, (SSD_BS, SEQ_TILE, LANES)).reshape(n_rows, LANES)
    dend = jnp.where(tok, jnp.exp(tot - acs), 0.0)
    eacs = jnp.where(tok, jnp.exp(acs), 0.0)
    cdec = jnp.exp(tot)

    def expand(v):
        hi, mid, lo = _split3(v)
        stack = jnp.concatenate([hi, mid, lo], axis=0).astype(bf16)
        r = jnp.dot(stack, e_ref[...], preferred_element_type=f32)
        return r[0:n_rows] + r[n_rows:2 * n_rows] + r[2 * n_rows:3 * n_rows]

    heads_per_group = N_HEADS // N_GROUPS
    y = expand(eacs)
    for s in range(SSD_BS):
        for g in range(N_GROUPS):
            c8 = cm[s * SEQ_TILE:(s + 1) * SEQ_TILE, g * D_STATE:(g + 1) * D_STATE].astype(bf16)
            hg = h0_ref[s, g * half:(g + 1) * half, :].astype(bf16)
            yoff_scr[s * SEQ_TILE:(s + 1) * SEQ_TILE, g * half:(g + 1) * half] = lax.dot_general(
                c8, hg, (((1,), (1,)), ((), ())), preferred_element_type=f32)
    y = y * yoff_scr[...] + xs * dx_ref[...]
    for d in range(4):
        b_sh = pltpu.roll(bm, d, axis=0) if d else bm
        prod = cm * b_sh
        cbs = [jnp.sum(prod[:, g * D_STATE:(g + 1) * D_STATE], axis=-1, keepdims=True) for g in range(N_GROUPS)]
        cbh = jnp.where(lanes < heads_per_group, cbs[0], cbs[1])
        if d:
            coef = cbh * jnp.exp(acs - pltpu.roll(acs, d, axis=0)) * pltpu.roll(dt, d, axis=0)
        else:
            coef = cbh * dt
        coef = jnp.where(pos >= TOK0 + d, coef, 0.0)
        y = y + expand(coef) * (pltpu.roll(xs, d, axis=0) if d else xs)

    yn = _gate_norm(y, z_ref[...], nw_ref[...]).astype(bf16)
    y_ref[...] = jnp.dot(sel_ref[...], yn, preferred_element_type=f32).astype(bf16)

    xw = xs * expand(dt * dend)
    hi, mid, lo = _split3(expand(cdec))
    pos_w = lax.broadcasted_iota(jnp.int32, (n_rows, D_SSM), 0) % SEQ_TILE
    lhs_scr[...] = jnp.where(pos_w == 0, hi, jnp.where(pos_w == 1, mid, jnp.where(pos_w == 2, lo, xw)))
    ones = jnp.where(pos < 3, 1.0, 0.0)
    for g in range(N_GROUPS):
        rhs_scr[:, 2 * g * D_STATE:(2 * g + 1) * D_STATE] = jnp.where(tok, bm[:, g * D_STATE:(g + 1) * D_STATE], 0.0)
        rhs_scr[:, (2 * g + 1) * D_STATE:(2 * g + 2) * D_STATE] = ones
    for s in range(SSD_BS):
        for g in range(N_GROUPS):
            l8 = lhs_scr[s * SEQ_TILE:(s + 1) * SEQ_TILE, g * half:(g + 1) * half].astype(bf16)
            r8 = rhs_scr[s * SEQ_TILE:(s + 1) * SEQ_TILE, 2 * g * D_STATE:(2 * g + 2) * D_STATE].astype(bf16)
            res = lax.dot_general(l8, r8, (((0,), (0,)), ((), ())), preferred_element_type=f32)
            hnew_ref[s, g * half:(g + 1) * half, :] = (
                h0_ref[s, g * half:(g + 1) * half, :] * res[:, D_STATE:] + res[:, :D_STATE])


def _ssd_sample(pz8, pxbc8, pdt8, h0, ssm_w, e_mat, sel):
    cw, cb, dtb, alog, dx, nw = ssm_w
    n_seq = h0.shape[0]
    n_rows = SSD_BS * SEQ_TILE
    n_out = SSD_BS * 4
    rows_map = lambda i: (i, 0)
    fixed = lambda i: (0, 0)
    return pl.pallas_call(
        _ssd_sample_kernel,
        grid=(n_seq // SSD_BS,),
        in_specs=[
            pl.BlockSpec((n_rows, D_SSM), rows_map),
            pl.BlockSpec((n_rows, CONV_DIM), rows_map),
            pl.BlockSpec((n_rows, LANES), rows_map),
            pl.BlockSpec((SSD_BS, D_SSM, D_STATE), lambda i: (i, 0, 0)),
            pl.BlockSpec((SSM_CONV, CONV_DIM), fixed),
            pl.BlockSpec((1, CONV_DIM), fixed),
            pl.BlockSpec((1, LANES), fixed),
            pl.BlockSpec((1, LANES), fixed),
            pl.BlockSpec((1, D_SSM), fixed),
            pl.BlockSpec((1, D_SSM), fixed),
            pl.BlockSpec((LANES, D_SSM), fixed),
            pl.BlockSpec((n_out, n_rows), fixed),
        ],
        out_specs=(
            pl.BlockSpec((n_out, D_SSM), rows_map),
            pl.BlockSpec((SSD_BS, D_SSM, D_STATE), lambda i: (i, 0, 0)),
        ),
        out_shape=(
            jax.ShapeDtypeStruct((n_seq * 4, D_SSM), bf16),
            jax.ShapeDtypeStruct((n_seq, D_SSM, D_STATE), f32),
        ),
        scratch_shapes=[
            pltpu.VMEM((n_rows, D_SSM), f32),
            pltpu.VMEM((n_rows, D_SSM), f32),
            pltpu.VMEM((n_rows, 2 * N_GROUPS * D_STATE), f32),
        ],
        compiler_params=_cparams(("arbitrary",)),
        name="ssd_sample",
    )(pz8, pxbc8, pdt8, h0, cw, cb, dtb, alog, dx, nw, e_mat, sel)


N_LT = D_CONF // LANES
HIST = 32
CONF_RC = 64


def _conf_norm_act(acc, g, beta):
    mu = jnp.mean(acc, axis=-1, keepdims=True)
    xc = acc - mu
    var = jnp.mean(xc * xc, axis=-1, keepdims=True)
    return _silu(xc * lax.rsqrt(var + EPS) * g + beta)


def _conf_conv_tile(s_tile, w_ref, p_ref, lt, y_tile, tt):
    off = HIST - (CONF_KERNEL - 1)
    for r0 in range(0, tt, CONF_RC):
        acc = jnp.broadcast_to(p_ref[lt, 0:1, :], (CONF_RC, LANES))
        for k in range(CONF_KERNEL):
            acc = acc + w_ref[lt, k:k + 1, :] * s_tile[r0 + off + k:r0 + off + k + CONF_RC, :]
        y_tile[r0:r0 + CONF_RC, :] = _conf_norm_act(acc, p_ref[lt, 1:2, :], p_ref[lt, 2:3, :])


def _conf_tile_kernel(ca_ref, cg_ref, hist_ref, w_ref, p_ref, y_ref, tail_ref, s_scr, y_scr, *, tt, n_valid):
    t = pl.program_id(1)

    @pl.when(t == 0)
    def _():
        for lt in range(N_LT):
            s_scr[lt, 0:HIST, :] = hist_ref[:, lt * LANES:(lt + 1) * LANES]

    @pl.when(t > 0)
    def _():
        s_scr[:, 0:HIST, :] = s_scr[:, tt:tt + HIST, :]

    glu = ca_ref[...] * _sigmoid(cg_ref[...])
    for lt in range(N_LT):
        s_scr[lt, HIST:HIST + tt, :] = glu[:, lt * LANES:(lt + 1) * LANES]

    def lane_tile(lt, carry):
        _conf_conv_tile(s_scr.at[lt], w_ref, p_ref, lt, y_scr.at[lt], tt)
        return carry

    lax.fori_loop(0, N_LT, lane_tile, 0)
    for lt in range(N_LT):
        y_ref[:, lt * LANES:(lt + 1) * LANES] = y_scr[lt].astype(bf16)
        tail_ref[:, lt * LANES:(lt + 1) * LANES] = s_scr[lt, n_valid:n_valid + HIST, :]


def _conf_tiles(pc, hist, conf_w, *, n_seq, n_tiles, tt, row_block0, n_valid):
    w3, p3 = conf_w
    in_specs = [
        pl.BlockSpec((tt, D_CONF), lambda b, t: (row_block0 + b * n_tiles + t, 0)),
        pl.BlockSpec((tt, D_CONF), lambda b, t: (row_block0 + b * n_tiles + t, 1)),
        pl.BlockSpec((HIST, D_CONF), lambda b, t: (0, 0)),
        pl.BlockSpec((N_LT, HIST, LANES), lambda b, t: (0, 0, 0)),
        pl.BlockSpec((N_LT, SUBLANES, LANES), lambda b, t: (0, 0, 0)),
    ]
    return pl.pallas_call(
        functools.partial(_conf_tile_kernel, tt=tt, n_valid=n_valid),
        grid=(n_seq, n_tiles),
        in_specs=in_specs,
        out_specs=(
            pl.BlockSpec((tt, D_CONF), lambda b, t: (b * n_tiles + t, 0)),
            pl.BlockSpec((None, HIST, D_CONF), lambda b, t: (b, 0, 0)),
        ),
        out_shape=(
            jax.ShapeDtypeStruct((n_seq * n_tiles * tt, D_CONF), bf16),
            jax.ShapeDtypeStruct((n_seq, HIST, D_CONF), f32),
        ),
        scratch_shapes=[
            pltpu.VMEM((N_LT, HIST + tt, LANES), f32),
            pltpu.VMEM((N_LT, tt, LANES), f32),
        ],
        compiler_params=_cparams(("arbitrary", "arbitrary")),
        name="conf_tiles",
    )(pc, pc, hist, w3, p3)


CP_ROWS = 1024
CP_COLS = 512


def _conf_proj_kernel(u_ref, wca_ref, wcg_ref, hist_ref, w_ref, p_ref, *rest, tiles_per_seq, n_side):
    side_in, (y_ref, tail_ref) = rest[:n_side], rest[n_side:n_side + 2]
    side_out, (s_scr, g_scr, y_scr) = rest[n_side + 2:2 * n_side + 2], rest[2 * n_side + 2:]
    i = pl.program_id(0)
    j = pl.program_id(1)
    n_lt = CP_COLS // LANES
    _run_side_cast(zip(side_in, side_out))

    @pl.when((i == 0) & (j == 0))
    def _():
        g_scr[...] = jnp.zeros(g_scr.shape, f32)
        s_scr[...] = jnp.zeros(s_scr.shape, f32)

    @pl.when(j == 0)
    def _():
        @pl.when((i - 1) % tiles_per_seq == 0)
        def _():
            for lt in range(N_LT):
                s_scr[lt, 0:HIST, :] = hist_ref[:, lt * LANES:(lt + 1) * LANES]

        @pl.when((i - 1) % tiles_per_seq != 0)
        def _():
            s_scr[:, 0:HIST, :] = s_scr[:, CP_ROWS:CP_ROWS + HIST, :]

        s_scr[:, HIST:HIST + CP_ROWS, :] = g_scr[...]

    u = u_ref[...]
    nt = (((1,), (1,)), ((), ()))
    glu = (lax.dot_general(u, wca_ref[...], nt, preferred_element_type=f32)
           * _sigmoid(lax.dot_general(u, wcg_ref[...], nt, preferred_element_type=f32)))
    for q in range(n_lt):
        lt = j * n_lt + q
        _conf_conv_tile(s_scr.at[lt], w_ref, p_ref, lt, y_scr.at[q], CP_ROWS)
        g_scr[lt] = glu[:, q * LANES:(q + 1) * LANES]
        y_ref[:, q * LANES:(q + 1) * LANES] = y_scr[q].astype(bf16)
        tail_ref[:, q * LANES:(q + 1) * LANES] = s_scr[lt, CP_ROWS:CP_ROWS + HIST, :]


def _conf_proj(u, w_t, ca_row0, hist, conf_w, sides, *, n_seq, seq_rows):
    w3, p3 = conf_w
    n_tiles = n_seq * seq_rows // CP_ROWS
    tiles_per_seq = seq_rows // CP_ROWS
    grid = (n_tiles + 1, D_CONF // CP_COLS)
    side_specs = [_side_cast(*s, grid) for s in sides]
    align = 2 * 16
    w_block = lambda row0: pl.BlockSpec(
        (pl.Element(CP_COLS), pl.Element(D_MODEL)), lambda i, j: (pl.multiple_of(row0 + CP_COLS * j, align), 0))
    slot = lambda i: (i + n_tiles) % (n_tiles + 1)
    return pl.pallas_call(
        functools.partial(_conf_proj_kernel, tiles_per_seq=tiles_per_seq, n_side=len(side_specs)),
        grid=grid,
        in_specs=[
            pl.BlockSpec((CP_ROWS, D_MODEL), lambda i, j: (jnp.minimum(i, n_tiles - 1), 0)),
            w_block(ca_row0),
            w_block(ca_row0 + D_CONF),
            pl.BlockSpec((HIST, D_CONF), lambda i, j: (0, 0)),
            pl.BlockSpec((N_LT, HIST, LANES), lambda i, j: (0, 0, 0)),
            pl.BlockSpec((N_LT, SUBLANES, LANES), lambda i, j: (0, 0, 0)),
        ] + [s[0] for s in side_specs],
        out_specs=(
            pl.BlockSpec((CP_ROWS, CP_COLS), lambda i, j: (slot(i), j)),
            pl.BlockSpec((None, HIST, CP_COLS), lambda i, j: (slot(i), 0, j)),
        ) + tuple(s[1] for s in side_specs),
        out_shape=(
            jax.ShapeDtypeStruct(((n_tiles + 1) * CP_ROWS, D_CONF), bf16),
            jax.ShapeDtypeStruct((n_tiles + 1, HIST, D_CONF), f32),
        ) + tuple(s[2] for s in side_specs),
        scratch_shapes=[
            pltpu.VMEM((N_LT, HIST + CP_ROWS, LANES), f32),
            pltpu.VMEM((N_LT, CP_ROWS, LANES), f32),
            pltpu.VMEM((CP_COLS // LANES, CP_ROWS, LANES), f32),
        ],
        compiler_params=_cparams(("arbitrary", "arbitrary")),
        name="conf_proj",
    )(u, w_t, w_t, hist, w3, p3, *[s[0] for s in sides])


CONF_BS = 16


def _conf_sample_kernel(ca_ref, cg_ref, buf_ref, w_ref, p_ref, y_ref, newbuf_ref):
    n_hist = CONF_KERNEL - 1
    n_tok = ca_ref.shape[0]
    glu = [ca_ref[t] * _sigmoid(cg_ref[t]) for t in range(n_tok)]
    win = lambda j: buf_ref[j] if j < n_hist else glu[j - n_hist]
    for j in range(n_hist):
        newbuf_ref[j] = win(j + n_tok)
    for t in range(n_tok):
        acc = jnp.broadcast_to(p_ref[0:1, :], (CONF_BS, D_CONF))
        for k in range(CONF_KERNEL):
            acc = acc + w_ref[k:k + 1, :] * win(t + k)
        for lt in range(N_LT):
            sl = slice(lt * LANES, (lt + 1) * LANES)
            y_ref[t, :, sl] = _conf_norm_act(acc[:, sl], p_ref[1:2, sl], p_ref[2:3, sl]).astype(bf16)


def _conf_sample(pc_t, buf_t, w2, p2, *, n_tok, tok_block0):
    n_seq = pc_t.shape[1]
    n_hist = CONF_KERNEL - 1
    seq_blocks = lambda col: (lambda i: (0, i, col))
    return pl.pallas_call(
        _conf_sample_kernel,
        grid=(n_seq // CONF_BS,),
        in_specs=[
            pl.BlockSpec((n_tok, CONF_BS, D_CONF), lambda i: (tok_block0, i, 0)),
            pl.BlockSpec((n_tok, CONF_BS, D_CONF), lambda i: (tok_block0, i, 1)),
            pl.BlockSpec((n_hist, CONF_BS, D_CONF), seq_blocks(0)),
            pl.BlockSpec((CONF_KERNEL, D_CONF), lambda i: (0, 0)),
            pl.BlockSpec((3, D_CONF), lambda i: (0, 0)),
        ],
        out_specs=(
            pl.BlockSpec((n_tok, CONF_BS, D_CONF), seq_blocks(0)),
            pl.BlockSpec((n_hist, CONF_BS, D_CONF), seq_blocks(0)),
        ),
        out_shape=(
            jax.ShapeDtypeStruct((n_tok, n_seq, D_CONF), bf16),
            jax.ShapeDtypeStruct((n_hist, n_seq, D_CONF), f32),
        ),
        compiler_params=_cparams(("arbitrary",)),
        name="conf_sample",
    )(pc_t, pc_t, buf_t, w2, p2)


def _seq_tiles(rows, n_seq, n_tok):
    w = rows.shape[-1]
    return jnp.pad(rows.reshape(n_seq, n_tok, w), ((0, 0), (SEQ_TILE - n_tok, 0), (0, 0))).reshape(n_seq * SEQ_TILE, w)


def kernel(x_prompt, x_sample, state_ssm, state_ssm_conv, state_conf_conv, meta_tokens, ffn1_norm, ffn1_w_gate, ffn1_w_up, ffn1_w_down, mix_norm, w_in, ssm_conv_w, ssm_conv_b, ssm_dt_bias, ssm_A_log, ssm_D, ssm_norm, conf_conv_w, conf_conv_b, conf_ln_g, conf_ln_b, w_out, ffn2_norm, ffn2_w_gate, ffn2_w_up, ffn2_w_down, final_norm):
    n_pb, n_pt = x_prompt.shape[0], x_prompt.shape[1]
    n_sb, n_st = x_sample.shape[0], x_sample.shape[1]
    assert (n_pb * n_pt, n_sb * n_st, n_st) == (N_PROMPT_ROWS, N_SAMPLE_ROWS, 4) and n_pt % CHUNK == 0

    row = lambda v: v.reshape(1, -1)
    swap01 = lambda v: jnp.transpose(v, (1, 0, 2))
    x_rows = (x_prompt.reshape(N_PROMPT_ROWS, D_MODEL), swap01(x_sample).reshape(N_SAMPLE_ROWS, D_MODEL), meta_tokens)
    h, u, w_t, wg2 = _ffn(x_rows, row(ffn1_norm[0]), _to_bf16(ffn1_w_gate, 256), _to_bf16(ffn1_w_up, 256),
                          _to_bf16(ffn1_w_down, 704), row(mix_norm[0]),
                          sides=((jnp.transpose(w_in[0]), 96), (ffn2_w_gate, 16)))

    c_in0 = D_SSM + CONV_DIM
    n_small = N_ROWS - ROW_SAMPLE
    small = dict(tm=n_small, a_rows=(ROW_SAMPLE, n_small))
    pzx = _matmul_nt(u, w_t, row0=0, n=c_in0, tn=1536, **small)
    pdt = _matmul_nt(u, w_t, row0=c_in0, n=LANES, tn=LANES, **small)
    pc = _matmul_nt(u, w_t, row0=c_in0 + N_HEADS, n=2 * D_CONF, tn=1024, **small)

    pad_heads = lambda v: jnp.pad(v.reshape(1, N_HEADS), ((0, 0), (0, LANES - N_HEADS)))
    ssm_w = (ssm_conv_w[0], row(ssm_conv_b[0]), pad_heads(ssm_dt_bias[0]), pad_heads(ssm_A_log[0]),
             row(jnp.repeat(ssm_D[0], HEAD_DIM)), row(ssm_norm[0]))
    tri = jnp.tril(jnp.ones((CHUNK, CHUNK), f32))
    e_mat = (jnp.arange(LANES)[:, None] == (jnp.arange(D_SSM)[None, :] // HEAD_DIM)).astype(bf16)
    lane_major = lambda v: v.reshape(v.shape[0], N_LT, LANES).transpose(1, 0, 2)
    conf_w = (lane_major(jnp.pad(conf_conv_w[0], ((0, HIST - CONF_KERNEL), (0, 0)))),
              lane_major(jnp.pad(jnp.stack([conf_conv_b[0], conf_ln_g[0], conf_ln_b[0]]), ((0, SUBLANES - 3), (0, 0)))))
    out_rows = jnp.arange(SSD_BS * 4)
    sel = ((out_rows // 4) * SEQ_TILE + TOK0 + out_rows % 4)[:, None] == jnp.arange(SSD_BS * SEQ_TILE)[None, :]
    sel = sel.astype(bf16)

    meta_block = (ROW_META - ROW_SAMPLE) // CHUNK
    ys_m, h_meta, tail_meta = _ssd_chunks(
        pzx, pdt, jnp.zeros((SUBLANES, CONV_DIM), f32), jnp.zeros((D_STATE, D_SSM), f32), ssm_w, (tri, e_mat),
        n_seq=1, n_chunks=1, row_block0=meta_block, n_valid=N_META, transpose_out=False)
    yc_m, hist_meta = _conf_tiles(pc, jnp.zeros((HIST, D_CONF), f32), conf_w, n_seq=1, n_tiles=1, tt=CHUNK,
                                  row_block0=meta_block, n_valid=N_META)

    ys_p, p_ssm, p_tail, wo = _ssd_proj(u, w_t, tail_meta[0], h_meta[0], ssm_w, (tri, e_mat), (w_out, 64),
                                        n_seq=n_pb, chunks_per_seq=n_pt // CHUNK)
    p_ssm, p_tail = p_ssm[:n_pb], p_tail[:n_pb]
    yc_p, tile_hist, wu2, wd2 = _conf_proj(u, w_t, c_in0 + N_HEADS, hist_meta[0], conf_w,
                                           ((ffn2_w_up, 64), (ffn2_w_down, 256)), n_seq=n_pb, seq_rows=n_pt)
    tiles_per_seq = n_pt // CP_ROWS
    p_hist = tile_hist[tiles_per_seq - 1:n_pb * tiles_per_seq:tiles_per_seq]

    smp = slice(0, N_SAMPLE_ROWS)
    seq_major = lambda v: swap01(v.reshape(n_st, n_sb, v.shape[-1]))
    pzx_s = seq_major(pzx[smp])
    pxbc_s = pzx_s[:, :, D_SSM:]
    pxbc8 = jnp.concatenate([jnp.zeros((n_sb, 1, CONV_DIM), f32), state_ssm_conv[0], pxbc_s], axis=1)
    ys_s, s_ssm = _ssd_sample(_seq_tiles(pzx_s[:, :, :D_SSM], n_sb, n_st), pxbc8.reshape(n_sb * SEQ_TILE, CONV_DIM),
                              _seq_tiles(seq_major(pdt[smp]), n_sb, n_st), state_ssm[0].reshape(n_sb, D_SSM, D_STATE),
                              ssm_w, e_mat, sel)
    ys_s = swap01(ys_s.reshape(n_sb, n_st, D_SSM)).reshape(N_SAMPLE_ROWS, D_SSM)
    yc_t, cconv_t = _conf_sample(pc.reshape(n_small // n_sb, n_sb, 2 * D_CONF), swap01(state_conf_conv[0]),
                                 conf_conv_w[0], jnp.stack([conf_conv_b[0], conf_ln_g[0], conf_ln_b[0]]),
                                 n_tok=n_st, tok_block0=0)
    yc_s = yc_t.reshape(N_SAMPLE_ROWS, D_CONF)
    s_cconv = swap01(cconv_t)

    h2 = _outproj(h, (ys_p, ys_s, ys_m), (yc_p, yc_s, yc_m), wo, main_rows=N_PROMPT_ROWS, tm=TM_PROJ, tn=512)
    y_prompt, y_sample = _ffn((h2,), row(ffn2_norm[0]), wg2, wu2, wd2, row(final_norm),
                              dst_rows=(N_PROMPT_ROWS, N_SAMPLE_ROWS))
    n_hist = CONF_KERNEL - 1
    return (
        y_prompt.reshape(n_pb, n_pt, D_MODEL), swap01(y_sample.reshape(n_st, n_sb, D_MODEL)),
        p_ssm.reshape(1, n_pb, N_HEADS, HEAD_DIM, D_STATE),
        p_tail[:, SUBLANES - (SSM_CONV - 1):, :][None],
        p_hist[:, HIST - n_hist:, :][None],
        s_ssm.reshape(1, n_sb, N_HEADS, HEAD_DIM, D_STATE),
        pxbc_s[:, n_st - (SSM_CONV - 1):, :][None],
        s_cconv[None],
    )
```

```python
import functools

import jax
import jax.numpy as jnp
from jax import lax
from jax.experimental import pallas as pl
from jax.experimental.pallas import tpu as pltpu

f32 = jnp.float32
bf16 = jnp.bfloat16

D_MODEL = 2048
N_META = 16
D_SSM = 2048
D_CONF = 2048
HEAD_DIM = 64
N_HEADS = 32
N_GROUPS = 2
D_STATE = 128
SSM_CONV = 4
CONV_DIM = D_SSM + 2 * N_GROUPS * D_STATE
CONF_KERNEL = 31
D_FF = 5632
EPS = 1e-5
LOG2E = 1.4426950408889634

LANES = 128
SUBLANES = 8
BF16_ROWS = 16
CHUNK = 128
VMEM_LIMIT = 60 * 1024 * 1024

N_PROMPT_ROWS = 4 * 2048
N_SAMPLE_ROWS = 128 * 4
ROW_SAMPLE = N_PROMPT_ROWS
ROW_META = N_PROMPT_ROWS + N_SAMPLE_ROWS
N_ROWS = ROW_META + CHUNK
TM = 736
TM_PROJ = 1104
TF = 512


def _cparams(sem):
    return pltpu.CompilerParams(dimension_semantics=sem, vmem_limit_bytes=VMEM_LIMIT)


def _sigmoid(x):
    return 1.0 / (1.0 + jnp.exp(-x))


def _silu(x):
    return x * _sigmoid(x)


def _rms(x, g):
    return x * lax.rsqrt(jnp.mean(x * x, axis=-1, keepdims=True) + EPS) * g


def _softplus(x):
    return jnp.maximum(x, 0.0) + jnp.log1p(jnp.exp(-jnp.abs(x)))


def _split3(v):
    hi = v.astype(bf16).astype(f32)
    r1 = v - hi
    mid = r1.astype(bf16).astype(f32)
    return hi, mid, r1 - mid


def _row_segments(src_rows, n_total, tm):
    bounds, start = [], 0
    for k, n in enumerate(src_rows):
        bounds.append((k, start, n))
        start += n
    if start < n_total:
        bounds.append((None, start, n_total - start))
    table = []
    for t in range(n_total // tm):
        lo, hi = t * tm, (t + 1) * tm
        segs = []
        for k, s, n in bounds:
            a, b = max(lo, s), min(hi, s + n)
            if a < b:
                segs.append((k, a - s, a - lo, b - a))
        table.append(segs)
    return table


def _load_tile(i, seg_table, groups, sem):
    tm = groups[0][2].shape[0]
    special = [t for t, segs in enumerate(seg_table) if segs != [(0, t * tm, 0, tm)]]
    first = special[0] if special else len(seg_table)
    assert special == list(range(first, len(seg_table)))

    @pl.when(i < first)
    def _():
        for main_ref, _, dst_ref in groups:
            dst_ref[...] = main_ref[...]

    for t in special:
        @pl.when(i == t)
        def _(segs=seg_table[t]):
            copies = []
            for main_ref, tail_srcs, dst_ref in groups:
                for k, r0, d0, n in segs:
                    if k is None:
                        dst_ref[d0:d0 + n, :] = jnp.zeros((n, dst_ref.shape[1]), dst_ref.dtype)
                    elif k == 0:
                        dst_ref[d0:d0 + n, :] = main_ref[d0:d0 + n, :]
                    else:
                        cp = pltpu.make_async_copy(tail_srcs[k - 1].at[pl.ds(r0, n)], dst_ref.at[pl.ds(d0, n)], sem)
                        cp.start()
                        copies.append(cp)
            for cp in copies:
                cp.wait()


def _side_cast(w, block_rows, grid):
    r, c = w.shape[-2:]
    n_blocks = r // block_rows
    assert r % block_rows == 0 and n_blocks <= grid[0] * grid[1]
    blk = lambda i, j: jnp.minimum(i * grid[1] + j, n_blocks - 1)
    if w.ndim == 3:
        in_spec = pl.BlockSpec((None, block_rows, c), lambda i, j: (0, blk(i, j), 0))
    else:
        in_spec = pl.BlockSpec((block_rows, c), lambda i, j: (blk(i, j), 0))
    return in_spec, pl.BlockSpec((block_rows, c), lambda i, j: (blk(i, j), 0)), jax.ShapeDtypeStruct((r, c), bf16)


def _run_side_cast(side_refs):
    for src_ref, dst_ref in side_refs:
        dst_ref[...] = src_ref[...].astype(bf16)


def _drain_rows(i, out_table, dsts, acc_scr, sems):
    def copies(t):
        return [pltpu.make_async_copy(acc_scr.at[t % 2, pl.ds(s0, n)], dsts[k].at[pl.ds(r0, n)], sems.at[t % 2])
                for k, r0, s0, n in out_table[t] if k is not None]

    n_tiles = len(out_table)
    for t in range(n_tiles):
        @pl.when(i == t)
        def _(t=t):
            if t > 0:
                for cp in copies(t - 1):
                    cp.wait()
            for cp in copies(t):
                cp.start()
            if t == n_tiles - 1:
                for cp in copies(t):
                    cp.wait()


def _ffn_kernel(*refs, seg_table, out_table, n_src, n_side):
    x_ref, tail_srcs = refs[0], refs[1:n_src]
    nw_ref, wg_ref, wu_ref, wd_ref, pw_ref = refs[n_src:n_src + 5]
    pos = n_src + 5
    side_in, pos = refs[pos:pos + n_side], pos + n_side
    i = pl.program_id(0)
    f = pl.program_id(1)
    if out_table is None:
        (acc, yn_ref), pos = refs[pos:pos + 2], pos + 2
        side_out, pos = refs[pos:pos + n_side], pos + n_side
        xn_ref, sem = refs[pos:]
    else:
        n_dst = 1 + max(k for segs in out_table for k, _, _, _ in segs if k is not None)
        dsts, pos = refs[pos:pos + n_dst], pos + n_dst
        side_out, pos = refs[pos:pos + n_side], pos + n_side
        acc_scr, xn_ref, sem, out_sems = refs[pos:]
        acc = acc_scr.at[i % 2]
    _run_side_cast(zip(side_in, side_out))

    @pl.when(f == 0)
    def _():
        _load_tile(i, seg_table, [(x_ref, tail_srcs, acc)], sem)
        xn_ref[...] = _rms(acc[...], nw_ref[...]).astype(bf16)

    xn = xn_ref[...]
    g = jnp.dot(xn, wg_ref[...], preferred_element_type=f32)
    u = jnp.dot(xn, wu_ref[...], preferred_element_type=f32)
    a = (_silu(g) * u).astype(bf16)
    acc[...] += 0.5 * jnp.dot(a, wd_ref[...], preferred_element_type=f32)

    @pl.when(f == pl.num_programs(1) - 1)
    def _():
        if out_table is None:
            yn_ref[...] = _rms(acc[...], pw_ref[...]).astype(yn_ref.dtype)
        else:
            acc[...] = _rms(acc[...], pw_ref[...])
            _drain_rows(i, out_table, dsts, acc_scr, out_sems)


def _ffn(srcs, norm_w, wg, wu, wd, post_w, *, dst_rows=None, sides=(), tf=TF):
    seg_table = _row_segments([s.shape[0] for s in srcs], N_ROWS, TM)
    grid = (N_ROWS // TM, D_FF // tf)
    side_specs = [_side_cast(*s, grid) for s in sides]
    row_spec = pl.BlockSpec((TM, D_MODEL), lambda i, f: (i, 0))
    vec_spec = pl.BlockSpec((1, D_MODEL), lambda i, f: (0, 0))
    scratch = [pltpu.VMEM((TM, D_MODEL), bf16), pltpu.SemaphoreType.DMA(())]
    if dst_rows is None:
        out_table = None
        out_shape = (jax.ShapeDtypeStruct((N_ROWS, D_MODEL), f32), jax.ShapeDtypeStruct((N_ROWS, D_MODEL), bf16))
        out_specs = (row_spec, row_spec)
    else:
        out_table = _row_segments(dst_rows, N_ROWS, TM)
        out_shape = tuple(jax.ShapeDtypeStruct((n, D_MODEL), f32) for n in dst_rows)
        out_specs = tuple(pl.BlockSpec(memory_space=pl.ANY) for _ in dst_rows)
        scratch = [pltpu.VMEM((2, TM, D_MODEL), f32)] + scratch + [pltpu.SemaphoreType.DMA((2,))]
    return pl.pallas_call(
        functools.partial(_ffn_kernel, seg_table=seg_table, out_table=out_table, n_src=len(srcs),
                          n_side=len(side_specs)),
        grid=grid,
        in_specs=[row_spec] + [pl.BlockSpec(memory_space=pl.ANY)] * (len(srcs) - 1) + [
            vec_spec,
            pl.BlockSpec((D_MODEL, tf), lambda i, f: (0, f)),
            pl.BlockSpec((D_MODEL, tf), lambda i, f: (0, f)),
            pl.BlockSpec((tf, D_MODEL), lambda i, f: (f, 0)),
            vec_spec,
        ] + [s[0] for s in side_specs],
        out_specs=tuple(out_specs) + tuple(s[1] for s in side_specs),
        out_shape=tuple(out_shape) + tuple(s[2] for s in side_specs),
        scratch_shapes=scratch,
        compiler_params=_cparams(("arbitrary", "arbitrary")),
        name="ffn",
    )(*srcs, norm_w, wg, wu, wd, post_w, *[s[0] for s in sides])


def _cast_kernel(x_ref, o_ref):
    o_ref[...] = x_ref[...].astype(bf16)


def _to_bf16(w, block_rows):
    r, c = w.shape[-2:]
    if w.ndim == 3:
        in_spec = pl.BlockSpec((None, block_rows, c), lambda i: (0, i, 0))
    else:
        in_spec = pl.BlockSpec((block_rows, c), lambda i: (i, 0))
    return pl.pallas_call(
        _cast_kernel,
        grid=(r // block_rows,),
        in_specs=[in_spec],
        out_specs=pl.BlockSpec((block_rows, c), lambda i: (i, 0)),
        out_shape=jax.ShapeDtypeStruct((r, c), bf16),
        compiler_params=_cparams(("arbitrary",)),
        name="cast_bf16",
    )(w)


def _matmul_nt_kernel(a_ref, w_ref, o_ref):
    o_ref[...] = lax.dot_general(a_ref[...], w_ref[...], (((1,), (1,)), ((), ())), preferred_element_type=f32)


def _matmul_nt(a, w_t, *, row0, n, tn, a_rows):
    k = a.shape[1]
    a_row0, tm = a_rows
    assert row0 % BF16_ROWS == 0 and tn % BF16_ROWS == 0 and n % tn == 0 and a_row0 % BF16_ROWS == 0
    return pl.pallas_call(
        _matmul_nt_kernel,
        grid=(1, n // tn),
        in_specs=[
            pl.BlockSpec((pl.Element(tm), pl.Element(k)), lambda i, j: (a_row0, 0)),
            pl.BlockSpec((pl.Element(tn), pl.Element(k)), lambda i, j: (pl.multiple_of(row0 + tn * j, BF16_ROWS), 0)),
        ],
        out_specs=pl.BlockSpec((tm, tn), lambda i, j: (0, j)),
        out_shape=jax.ShapeDtypeStruct((tm, n), f32),
        compiler_params=_cparams(("arbitrary", "arbitrary")),
        name="in_proj",
    )(a, w_t)


def _outproj_kernel(*refs, seg_table, n_src):
    ys_ref, ys_tails = refs[0], refs[1:n_src]
    yc_ref, yc_tails = refs[n_src], refs[n_src + 1:2 * n_src]
    h_ref, w0_ref, w1_ref, o_ref, ys_scr, yc_scr, sem = refs[2 * n_src:]

    @pl.when(pl.program_id(1) == 0)
    def _():
        _load_tile(pl.program_id(0), seg_table, [(ys_ref, ys_tails, ys_scr), (yc_ref, yc_tails, yc_scr)], sem)

    acc = jnp.dot(ys_scr[...], w0_ref[...], preferred_element_type=f32)
    acc = acc + jnp.dot(yc_scr[...], w1_ref[...], preferred_element_type=f32)
    o_ref[...] = h_ref[...] + acc


def _outproj(h, ys_srcs, yc_srcs, w_out, *, main_rows, tm, tn):
    n_rows = h.shape[0]
    assert n_rows % tm == 0
    seg_table = _row_segments([main_rows] + [s.shape[0] for s in ys_srcs[1:]], n_rows, tm)
    n_src = len(ys_srcs)
    y_specs = [pl.BlockSpec((tm, D_SSM), lambda i, j: (i, 0))] + [pl.BlockSpec(memory_space=pl.ANY)] * (n_src - 1)
    return pl.pallas_call(
        functools.partial(_outproj_kernel, seg_table=seg_table, n_src=n_src),
        grid=(n_rows // tm, D_MODEL // tn),
        in_specs=y_specs + y_specs + [
            pl.BlockSpec((tm, tn), lambda i, j: (i, j)),
            pl.BlockSpec((D_SSM, tn), lambda i, j: (0, j)),
            pl.BlockSpec((D_CONF, tn), lambda i, j: (1, j)),
        ],
        out_specs=pl.BlockSpec((tm, tn), lambda i, j: (i, j)),
        out_shape=jax.ShapeDtypeStruct((n_rows, D_MODEL), f32),
        scratch_shapes=[pltpu.VMEM((tm, D_SSM), bf16), pltpu.VMEM((tm, D_CONF), bf16), pltpu.SemaphoreType.DMA(())],
        compiler_params=_cparams(("arbitrary", "arbitrary")),
        name="out_proj",
    )(*ys_srcs, *yc_srcs, h, w_out, w_out)


def _gate_norm(y, z, nw):
    y = y * _silu(z)
    half = D_SSM // N_GROUPS
    parts = []
    for g in range(N_GROUPS):
        yg = y[:, g * half:(g + 1) * half]
        ms = jnp.mean(yg * yg, axis=-1, keepdims=True)
        parts.append(yg * lax.rsqrt(ms + EPS) * nw[:, g * half:(g + 1) * half])
    return jnp.concatenate(parts, axis=1)


def _ssd_chunk_math(z, xbc, dt_raw, cinit_ref, hinit_ref, cw_ref, cb_ref, dtb_ref, alog_ref,
                    dx_ref, nw_ref, tri_ref, e_ref, y_ref, hout_ref, tail_ref, win_scr, ht_scr, y_scr,
                    *, n_valid, transpose_out, first=None, last=None, side_work=None):
    if first is None:
        first = pl.program_id(1) == 0
        last = pl.program_id(1) == pl.num_programs(1) - 1

    @pl.when(first)
    def _():
        ht_scr[...] = hinit_ref[...]
        win_scr[0:SUBLANES, :] = cinit_ref[...]

    win_scr[SUBLANES:SUBLANES + CHUNK, :] = xbc
    acc = cb_ref[...] + cw_ref[SSM_CONV - 1:SSM_CONV, :] * xbc
    for j in range(1, SSM_CONV):
        acc = acc + cw_ref[SSM_CONV - 1 - j:SSM_CONV - j, :] * win_scr[SUBLANES - j:SUBLANES - j + CHUNK, :]
    tail_ref[...] = win_scr[n_valid:n_valid + SUBLANES, :]
    win_scr[0:SUBLANES, :] = win_scr[CHUNK:CHUNK + SUBLANES, :]
    xc = _silu(acc)
    b_mats = [xc[:, D_SSM + g * D_STATE:D_SSM + (g + 1) * D_STATE] for g in range(N_GROUPS)]
    c_mats = [xc[:, D_SSM + (N_GROUPS + g) * D_STATE:D_SSM + (N_GROUPS + g + 1) * D_STATE] for g in range(N_GROUPS)]

    rows = lax.broadcasted_iota(jnp.int32, (CHUNK, LANES), 0)
    lanes = lax.broadcasted_iota(jnp.int32, (CHUNK, LANES), 1)
    dt = _softplus(dt_raw + dtb_ref[...])
    if n_valid < CHUNK:
        dt = jnp.where(rows < n_valid, dt, 0.0)
    a = dt * (-jnp.exp(alog_ref[...]))
    acs = jnp.dot(tri_ref[...], a, precision=lax.Precision.HIGHEST, preferred_element_type=f32) * LOG2E
    total = acs[CHUNK - 1:CHUNK, :]
    dend = jnp.exp2(total - acs)
    acs_t = acs.T
    dt_t = dt.T
    w_t = (dt * dend).T
    hi, mid, lo = _split3(jnp.exp2(total))
    rid = lax.broadcasted_iota(jnp.int32, (SUBLANES, LANES), 0)
    cd3 = jnp.where(rid == 0, hi, jnp.where(rid == 1, mid, jnp.where(rid == 2, lo, 0.0)))
    cdx = jnp.sum(jnp.dot(cd3.astype(bf16), e_ref[...], preferred_element_type=f32), axis=0, keepdims=True)

    cb_mats, bt_mats = [], []
    for g in range(N_GROUPS):
        cb_mats.append(lax.dot_general(c_mats[g].astype(bf16), b_mats[g].astype(bf16), (((1,), (1,)), ((), ())),
                                       preferred_element_type=f32))
        bt_mats.append(b_mats[g].T)

    lane_lo = lanes < HEAD_DIM
    tril = rows >= lanes
    heads_per_group = N_HEADS // N_GROUPS
    for j in range(N_HEADS // 2):
        if side_work is not None:
            side_work(j)
        sl = slice(j * LANES, (j + 1) * LANES)
        xs_pair = xc[:, sl]
        ht_pair = ht_scr[:, sl]
        y_pair = jnp.zeros((CHUNK, LANES), f32)
        upd = jnp.zeros((D_STATE, LANES), f32)
        for half in range(2):
            h = 2 * j + half
            g = h // heads_per_group
            keep = lane_lo if half == 0 else jnp.logical_not(lane_lo)
            xs_m = jnp.where(keep, xs_pair, 0.0).astype(bf16)
            ht_m = jnp.where(keep, ht_pair, 0.0).astype(bf16)
            acs_col = jnp.broadcast_to(acs[:, h:h + 1], (CHUNK, CHUNK))
            acs_row = jnp.broadcast_to(acs_t[h:h + 1, :], (CHUNK, CHUNK))
            lmat = jnp.exp2(jnp.where(tril, acs_col - acs_row, -jnp.inf))
            m_h = cb_mats[g] * lmat * dt_t[h:h + 1, :]
            ce_h = c_mats[g] * jnp.exp2(acs_col)
            lhs = jnp.concatenate([m_h, ce_h], axis=1).astype(bf16)
            rhs = jnp.concatenate([xs_m, ht_m], axis=0)
            y_pair = y_pair + jnp.dot(lhs, rhs, preferred_element_type=f32)
            bw = (bt_mats[g] * w_t[h:h + 1, :]).astype(bf16)
            upd = upd + jnp.dot(bw, xs_m, preferred_element_type=f32)
        y_scr[:, sl] = y_pair
        ht_scr[:, sl] = ht_pair * cdx[:, sl] + upd

    xs = xc[:, :D_SSM]
    y = y_scr[...] + xs * dx_ref[...]
    y_ref[...] = _gate_norm(y, z(), nw_ref[...]).astype(bf16)

    @pl.when(last)
    def _():
        if transpose_out:
            hout_ref[...] = ht_scr[...].T
        else:
            hout_ref[...] = ht_scr[...]


N_SSD_PARAMS = 10


def _ssd_chunk_kernel(z_ref, xs_ref, bc_ref, dt_ref, *rest, n_valid, transpose_out):
    xbc = jnp.concatenate([xs_ref[...], bc_ref[...]], axis=1)
    _ssd_chunk_math(lambda: z_ref[...], xbc, dt_ref[...], *rest, n_valid=n_valid, transpose_out=transpose_out)


def _ssd_chunks(pzx, pdt, cinit, hinit, ssm_w, consts, *, n_seq, n_chunks, row_block0, n_valid, transpose_out):
    cw, cb, dtb, alog, dx, nw = ssm_w
    tri, e_mat = consts
    bc_dim = CONV_DIM - D_SSM
    rows_map = lambda b, c: (row_block0 + b * n_chunks + c, 0)
    fixed = lambda b, c: (0, 0)
    in_specs = [
        pl.BlockSpec((CHUNK, D_SSM), rows_map),
        pl.BlockSpec((CHUNK, D_SSM), lambda b, c: (row_block0 + b * n_chunks + c, 1)),
        pl.BlockSpec((CHUNK, bc_dim), lambda b, c: (row_block0 + b * n_chunks + c, 2 * D_SSM // bc_dim)),
        pl.BlockSpec((CHUNK, LANES), rows_map),
        pl.BlockSpec((SUBLANES, CONV_DIM), fixed),
        pl.BlockSpec((D_STATE, D_SSM), fixed),
        pl.BlockSpec((SSM_CONV, CONV_DIM), fixed),
        pl.BlockSpec((1, CONV_DIM), fixed),
        pl.BlockSpec((1, LANES), fixed),
        pl.BlockSpec((1, LANES), fixed),
        pl.BlockSpec((1, D_SSM), fixed),
        pl.BlockSpec((1, D_SSM), fixed),
        pl.BlockSpec((CHUNK, CHUNK), fixed),
        pl.BlockSpec((LANES, D_SSM), fixed),
    ]
    args = [pzx, pzx, pzx, pdt, cinit, hinit, cw, cb, dtb, alog, dx, nw, tri, e_mat]
    hshape = (n_seq, D_SSM, D_STATE) if transpose_out else (n_seq, D_STATE, D_SSM)
    return pl.pallas_call(
        functools.partial(_ssd_chunk_kernel, n_valid=n_valid, transpose_out=transpose_out),
        grid=(n_seq, n_chunks),
        in_specs=in_specs,
        out_specs=(
            pl.BlockSpec((CHUNK, D_SSM), lambda b, c: (b * n_chunks + c, 0)),
            pl.BlockSpec((None,) + hshape[1:], lambda b, c: (b, 0, 0)),
            pl.BlockSpec((None, SUBLANES, CONV_DIM), lambda b, c: (b, 0, 0)),
        ),
        out_shape=(
            jax.ShapeDtypeStruct((n_seq * n_chunks * CHUNK, D_SSM), bf16),
            jax.ShapeDtypeStruct(hshape, f32),
            jax.ShapeDtypeStruct((n_seq, SUBLANES, CONV_DIM), f32),
        ),
        scratch_shapes=[
            pltpu.VMEM((SUBLANES + CHUNK, CONV_DIM), f32),
            pltpu.VMEM((D_STATE, D_SSM), f32),
            pltpu.VMEM((CHUNK, D_SSM), f32),
        ],
        compiler_params=_cparams(("arbitrary", "arbitrary")),
        name="ssd_chunks",
    )(*args)


SP_CHUNKS = 4
SP_ROWS = SP_CHUNKS * CHUNK
SP_COLS = 1280


def _ssd_proj_kernel(u_ref, w_ref, *rest, chunks_per_seq):
    params, rest = rest[:N_SSD_PARAMS], rest[N_SSD_PARAMS:]
    side_in, y_ref, hout_ref, tail_ref, side_out, win_scr, ht_scr, y_scr, p_cur, p_next = rest
    i = pl.program_id(0)
    j = pl.program_id(1)
    side_out[...] = side_in[...].astype(bf16)

    @pl.when((j == 0) & (i > 0))
    def _():
        p_cur[...] = p_next[...]

    piece = 2 * LANES
    n_pieces = SP_COLS // piece
    n_pairs = N_HEADS // 2

    def side_work(pair):
        if pair % (n_pairs // n_pieces) == 0 and pair // (n_pairs // n_pieces) < n_pieces:
            q = pair // (n_pairs // n_pieces)
            p_next[j, :, q * piece:(q + 1) * piece] = lax.dot_general(
                u_ref[...], w_ref[q * piece:(q + 1) * piece, :], (((1,), (1,)), ((), ())), preferred_element_type=f32)

    rows = pl.ds(pl.multiple_of(j * CHUNK, CHUNK), CHUNK)

    def cols(lo, hi):
        parts = []
        for k in range(SP_CHUNKS):
            a, b = max(lo, k * SP_COLS), min(hi, (k + 1) * SP_COLS)
            if a < b:
                parts.append(p_cur[k, rows, a - k * SP_COLS:b - k * SP_COLS])
        return parts[0] if len(parts) == 1 else jnp.concatenate(parts, axis=1)

    c = ((i - 1) % (chunks_per_seq // SP_CHUNKS)) * SP_CHUNKS + j

    def math(side):
        _ssd_chunk_math(lambda: cols(0, D_SSM), cols(D_SSM, D_SSM + CONV_DIM),
                        cols(D_SSM + CONV_DIM, D_SSM + CONV_DIM + LANES), *params, y_ref, hout_ref, tail_ref,
                        win_scr, ht_scr, y_scr, n_valid=CHUNK, transpose_out=True,
                        first=c == 0, last=c == chunks_per_seq - 1, side_work=side)

    last_i = pl.num_programs(0) - 1

    @pl.when(i == 0)
    def _():
        for pair in range(n_pairs):
            side_work(pair)
        y_ref[...] = jnp.zeros(y_ref.shape, y_ref.dtype)
        hout_ref[...] = jnp.zeros(hout_ref.shape, f32)
        tail_ref[...] = jnp.zeros(tail_ref.shape, f32)

    @pl.when((i > 0) & (i < last_i))
    def _():
        math(side_work)

    @pl.when(i == last_i)
    def _():
        math(None)


def _ssd_proj(u, w_t, cinit, hinit, ssm_w, consts, side, *, n_seq, chunks_per_seq):
    cw, cb, dtb, alog, dx, nw = ssm_w
    tri, e_mat = consts
    tiles_per_seq = chunks_per_seq // SP_CHUNKS
    n_tiles = n_seq * tiles_per_seq
    grid = (n_tiles + 1, SP_CHUNKS)
    side_in, side_out, side_shape = _side_cast(*side, grid)
    fixed = lambda i, j: (0, 0)
    slot = lambda i: (i + n_tiles) % (n_tiles + 1)
    seq_slot = lambda i: jnp.where(i == 0, n_seq, (i - 1) // tiles_per_seq)
    return pl.pallas_call(
        functools.partial(_ssd_proj_kernel, chunks_per_seq=chunks_per_seq),
        grid=grid,
        in_specs=[
            pl.BlockSpec((SP_ROWS, D_MODEL), lambda i, j: (jnp.minimum(i, n_tiles - 1), 0)),
            pl.BlockSpec((SP_COLS, D_MODEL), lambda i, j: (j, 0)),
            pl.BlockSpec((SUBLANES, CONV_DIM), fixed),
            pl.BlockSpec((D_STATE, D_SSM), fixed),
            pl.BlockSpec((SSM_CONV, CONV_DIM), fixed),
            pl.BlockSpec((1, CONV_DIM), fixed),
            pl.BlockSpec((1, LANES), fixed),
            pl.BlockSpec((1, LANES), fixed),
            pl.BlockSpec((1, D_SSM), fixed),
            pl.BlockSpec((1, D_SSM), fixed),
            pl.BlockSpec((CHUNK, CHUNK), fixed),
            pl.BlockSpec((LANES, D_SSM), fixed),
            side_in,
        ],
        out_specs=(
            pl.BlockSpec((CHUNK, D_SSM), lambda i, j: (slot(i) * SP_CHUNKS + j, 0)),
            pl.BlockSpec((None, D_SSM, D_STATE), lambda i, j: (seq_slot(i), 0, 0)),
            pl.BlockSpec((None, SUBLANES, CONV_DIM), lambda i, j: (seq_slot(i), 0, 0)),
            side_out,
        ),
        out_shape=(
            jax.ShapeDtypeStruct(((n_tiles + 1) * SP_ROWS, D_SSM), bf16),
            jax.ShapeDtypeStruct((n_seq + 1, D_SSM, D_STATE), f32),
            jax.ShapeDtypeStruct((n_seq + 1, SUBLANES, CONV_DIM), f32),
            side_shape,
        ),
        scratch_shapes=[
            pltpu.VMEM((SUBLANES + CHUNK, CONV_DIM), f32),
            pltpu.VMEM((D_STATE, D_SSM), f32),
            pltpu.VMEM((CHUNK, D_SSM), f32),
            pltpu.VMEM((SP_CHUNKS, SP_ROWS, SP_COLS), f32),
            pltpu.VMEM((SP_CHUNKS, SP_ROWS, SP_COLS), f32),
        ],
        compiler_params=_cparams(("arbitrary", "arbitrary")),
        name="ssd_proj",
    )(u, w_t, cinit, hinit, cw, cb, dtb, alog, dx, nw, tri, e_mat, side[0])


SEQ_TILE = SUBLANES
TOK0 = SEQ_TILE - 4
SSD_BS = 8


def _ssd_sample_kernel(z_ref, xbc_ref, dt_ref, h0_ref, cw_ref, cb_ref, dtb_ref, alog_ref, dx_ref, nw_ref,
                       e_ref, sel_ref, y_ref, hnew_ref, yoff_scr, lhs_scr, rhs_scr):
    n_rows = SSD_BS * SEQ_TILE
    half = D_SSM // N_GROUPS
    x8 = xbc_ref[...]
    acc = cb_ref[...] + cw_ref[SSM_CONV - 1:SSM_CONV, :] * x8
    for j in range(1, SSM_CONV):
        acc = acc + cw_ref[SSM_CONV - 1 - j:SSM_CONV - j, :] * pltpu.roll(x8, j, axis=0)
    xc = _silu(acc)
    xs = xc[:, :D_SSM]
    bm = xc[:, D_SSM:D_SSM + N_GROUPS * D_STATE]
    cm = xc[:, D_SSM + N_GROUPS * D_STATE:]

    pos = lax.broadcasted_iota(jnp.int32, (n_rows, LANES), 0) % SEQ_TILE
    lanes = lax.broadcasted_iota(jnp.int32, (n_rows, LANES), 1)
    tok = pos >= TOK0
    dt = jnp.where(tok, _softplus(dt_ref[...] + dtb_ref[...]), 0.0)
    a = dt * (-jnp.exp(alog_ref[...]))
    acs = a
    for d in range(1, 4):
        acs = acs + pltpu.roll(a, d, axis=0)
    tot = jnp.sum(a.reshape(SSD_BS, SEQ_TILE, LANES), axis=1, keepdims=True)
    tot = jnp.broadcast_to(tot, (SSD_BS, SEQ_TILE, LANES)).reshape(n_rows, LANES)
    dend = jnp.where(tok, jnp.exp(tot - acs), 0.0)
    eacs = jnp.where(tok, jnp.exp(acs), 0.0)
    cdec = jnp.exp(tot)

    def expand(v):
        hi, mid, lo = _split3(v)
        stack = jnp.concatenate([hi, mid, lo], axis=0).astype(bf16)
        r = jnp.dot(stack, e_ref[...], preferred_element_type=f32)
        return r[0:n_rows] + r[n_rows:2 * n_rows] + r[2 * n_rows:3 * n_rows]

    heads_per_group = N_HEADS // N_GROUPS
    y = expand(eacs)
    for s in range(SSD_BS):
        for g in range(N_GROUPS):
            c8 = cm[s * SEQ_TILE:(s + 1) * SEQ_TILE, g * D_STATE:(g + 1) * D_STATE].astype(bf16)
            hg = h0_ref[s, g * half:(g + 1) * half, :].astype(bf16)
            yoff_scr[s * SEQ_TILE:(s + 1) * SEQ_TILE, g * half:(g + 1) * half] = lax.dot_general(
                c8, hg, (((1,), (1,)), ((), ())), preferred_element_type=f32)
    y = y * yoff_scr[...] + xs * dx_ref[...]
    for d in range(4):
        b_sh = pltpu.roll(bm, d, axis=0) if d else bm
        prod = cm * b_sh
        cbs = [jnp.sum(prod[:, g * D_STATE:(g + 1) * D_STATE], axis=-1, keepdims=True) for g in range(N_GROUPS)]
        cbh = jnp.where(lanes < heads_per_group, cbs[0], cbs[1])
        if d:
            coef = cbh * jnp.exp(acs - pltpu.roll(acs, d, axis=0)) * pltpu.roll(dt, d, axis=0)
        else:
            coef = cbh * dt
        coef = jnp.where(pos >= TOK0 + d, coef, 0.0)
        y = y + expand(coef) * (pltpu.roll(xs, d, axis=0) if d else xs)

    yn = _gate_norm(y, z_ref[...], nw_ref[...]).astype(bf16)
    y_ref[...] = jnp.dot(sel_ref[...], yn, preferred_element_type=f32).astype(bf16)

    xw = xs * expand(dt * dend)
    hi, mid, lo = _split3(expand(cdec))
    pos_w = lax.broadcasted_iota(jnp.int32, (n_rows, D_SSM), 0) % SEQ_TILE
    lhs_scr[...] = jnp.where(pos_w == 0, hi, jnp.where(pos_w == 1, mid, jnp.where(pos_w == 2, lo, xw)))
    ones = jnp.where(pos < 3, 1.0, 0.0)
    for g in range(N_GROUPS):
        rhs_scr[:, 2 * g * D_STATE:(2 * g + 1) * D_STATE] = jnp.where(tok, bm[:, g * D_STATE:(g + 1) * D_STATE], 0.0)
        rhs_scr[:, (2 * g + 1) * D_STATE:(2 * g + 2) * D_STATE] = ones
    for s in range(SSD_BS):
        for g in range(N_GROUPS):
            l8 = lhs_scr[s * SEQ_TILE:(s + 1) * SEQ_TILE, g * half:(g + 1) * half].astype(bf16)
            r8 = rhs_scr[s * SEQ_TILE:(s + 1) * SEQ_TILE, 2 * g * D_STATE:(2 * g + 2) * D_STATE].astype(bf16)
            res = lax.dot_general(l8, r8, (((0,), (0,)), ((), ())), preferred_element_type=f32)
            hnew_ref[s, g * half:(g + 1) * half, :] = (
                h0_ref[s, g * half:(g + 1) * half, :] * res[:, D_STATE:] + res[:, :D_STATE])


def _ssd_sample(pz8, pxbc8, pdt8, h0, ssm_w, e_mat, sel):
    cw, cb, dtb, alog, dx, nw = ssm_w
    n_seq = h0.shape[0]
    n_rows = SSD_BS * SEQ_TILE
    n_out = SSD_BS * 4
    rows_map = lambda i: (i, 0)
    fixed = lambda i: (0, 0)
    return pl.pallas_call(
        _ssd_sample_kernel,
        grid=(n_seq // SSD_BS,),
        in_specs=[
            pl.BlockSpec((n_rows, D_SSM), rows_map),
            pl.BlockSpec((n_rows, CONV_DIM), rows_map),
            pl.BlockSpec((n_rows, LANES), rows_map),
            pl.BlockSpec((SSD_BS, D_SSM, D_STATE), lambda i: (i, 0, 0)),
            pl.BlockSpec((SSM_CONV, CONV_DIM), fixed),
            pl.BlockSpec((1, CONV_DIM), fixed),
            pl.BlockSpec((1, LANES), fixed),
            pl.BlockSpec((1, LANES), fixed),
            pl.BlockSpec((1, D_SSM), fixed),
            pl.BlockSpec((1, D_SSM), fixed),
            pl.BlockSpec((LANES, D_SSM), fixed),
            pl.BlockSpec((n_out, n_rows), fixed),
        ],
        out_specs=(
            pl.BlockSpec((n_out, D_SSM), rows_map),
            pl.BlockSpec((SSD_BS, D_SSM, D_STATE), lambda i: (i, 0, 0)),
        ),
        out_shape=(
            jax.ShapeDtypeStruct((n_seq * 4, D_SSM), bf16),
            jax.ShapeDtypeStruct((n_seq, D_SSM, D_STATE), f32),
        ),
        scratch_shapes=[
            pltpu.VMEM((n_rows, D_SSM), f32),
            pltpu.VMEM((n_rows, D_SSM), f32),
            pltpu.VMEM((n_rows, 2 * N_GROUPS * D_STATE), f32),
        ],
        compiler_params=_cparams(("arbitrary",)),
        name="ssd_sample",
    )(pz8, pxbc8, pdt8, h0, cw, cb, dtb, alog, dx, nw, e_mat, sel)


N_LT = D_CONF // LANES
HIST = 32
CONF_RC = 64


def _conf_norm_act(acc, g, beta):
    mu = jnp.mean(acc, axis=-1, keepdims=True)
    xc = acc - mu
    var = jnp.mean(xc * xc, axis=-1, keepdims=True)
    return _silu(xc * lax.rsqrt(var + EPS) * g + beta)


def _conf_conv_tile(s_tile, w_ref, p_ref, lt, y_tile, tt):
    off = HIST - (CONF_KERNEL - 1)
    for r0 in range(0, tt, CONF_RC):
        acc = jnp.broadcast_to(p_ref[lt, 0:1, :], (CONF_RC, LANES))
        for k in range(CONF_KERNEL):
            acc = acc + w_ref[lt, k:k + 1, :] * s_tile[r0 + off + k:r0 + off + k + CONF_RC, :]
        y_tile[r0:r0 + CONF_RC, :] = _conf_norm_act(acc, p_ref[lt, 1:2, :], p_ref[lt, 2:3, :])


def _conf_tile_kernel(ca_ref, cg_ref, hist_ref, w_ref, p_ref, y_ref, tail_ref, s_scr, y_scr, *, tt, n_valid):
    t = pl.program_id(1)

    @pl.when(t == 0)
    def _():
        for lt in range(N_LT):
            s_scr[lt, 0:HIST, :] = hist_ref[:, lt * LANES:(lt + 1) * LANES]

    @pl.when(t > 0)
    def _():
        s_scr[:, 0:HIST, :] = s_scr[:, tt:tt + HIST, :]

    glu = ca_ref[...] * _sigmoid(cg_ref[...])
    for lt in range(N_LT):
        s_scr[lt, HIST:HIST + tt, :] = glu[:, lt * LANES:(lt + 1) * LANES]

    def lane_tile(lt, carry):
        _conf_conv_tile(s_scr.at[lt], w_ref, p_ref, lt, y_scr.at[lt], tt)
        return carry

    lax.fori_loop(0, N_LT, lane_tile, 0)
    for lt in range(N_LT):
        y_ref[:, lt * LANES:(lt + 1) * LANES] = y_scr[lt].astype(bf16)
        tail_ref[:, lt * LANES:(lt + 1) * LANES] = s_scr[lt, n_valid:n_valid + HIST, :]


def _conf_tiles(pc, hist, conf_w, *, n_seq, n_tiles, tt, row_block0, n_valid):
    w3, p3 = conf_w
    in_specs = [
        pl.BlockSpec((tt, D_CONF), lambda b, t: (row_block0 + b * n_tiles + t, 0)),
        pl.BlockSpec((tt, D_CONF), lambda b, t: (row_block0 + b * n_tiles + t, 1)),
        pl.BlockSpec((HIST, D_CONF), lambda b, t: (0, 0)),
        pl.BlockSpec((N_LT, HIST, LANES), lambda b, t: (0, 0, 0)),
        pl.BlockSpec((N_LT, SUBLANES, LANES), lambda b, t: (0, 0, 0)),
    ]
    return pl.pallas_call(
        functools.partial(_conf_tile_kernel, tt=tt, n_valid=n_valid),
        grid=(n_seq, n_tiles),
        in_specs=in_specs,
        out_specs=(
            pl.BlockSpec((tt, D_CONF), lambda b, t: (b * n_tiles + t, 0)),
            pl.BlockSpec((None, HIST, D_CONF), lambda b, t: (b, 0, 0)),
        ),
        out_shape=(
            jax.ShapeDtypeStruct((n_seq * n_tiles * tt, D_CONF), bf16),
            jax.ShapeDtypeStruct((n_seq, HIST, D_CONF), f32),
        ),
        scratch_shapes=[
            pltpu.VMEM((N_LT, HIST + tt, LANES), f32),
            pltpu.VMEM((N_LT, tt, LANES), f32),
        ],
        compiler_params=_cparams(("arbitrary", "arbitrary")),
        name="conf_tiles",
    )(pc, pc, hist, w3, p3)


CP_ROWS = 1024
CP_COLS = 512


def _conf_proj_kernel(u_ref, wca_ref, wcg_ref, hist_ref, w_ref, p_ref, *rest, tiles_per_seq, n_side):
    side_in, (y_ref, tail_ref) = rest[:n_side], rest[n_side:n_side + 2]
    side_out, (s_scr, g_scr, y_scr) = rest[n_side + 2:2 * n_side + 2], rest[2 * n_side + 2:]
    i = pl.program_id(0)
    j = pl.program_id(1)
    n_lt = CP_COLS // LANES
    _run_side_cast(zip(side_in, side_out))

    @pl.when((j == 0) & (i > 0))
    def _():
        @pl.when((i - 1) % tiles_per_seq == 0)
        def _():
            for lt in range(N_LT):
                s_scr[lt, 0:HIST, :] = hist_ref[:, lt * LANES:(lt + 1) * LANES]

        @pl.when((i - 1) % tiles_per_seq != 0)
        def _():
            s_scr[:, 0:HIST, :] = s_scr[:, CP_ROWS:CP_ROWS + HIST, :]

        s_scr[:, HIST:HIST + CP_ROWS, :] = g_scr[...]

    def step(project, conv):
        if project:
            u = u_ref[...]
            nt = (((1,), (1,)), ((), ()))
            glu = (lax.dot_general(u, wca_ref[...], nt, preferred_element_type=f32)
                   * _sigmoid(lax.dot_general(u, wcg_ref[...], nt, preferred_element_type=f32)))
        for q in range(n_lt):
            lt = j * n_lt + q
            sl = slice(q * LANES, (q + 1) * LANES)
            if conv:
                _conf_conv_tile(s_scr.at[lt], w_ref, p_ref, lt, y_scr.at[q], CP_ROWS)
                y_ref[:, sl] = y_scr[q].astype(bf16)
                tail_ref[:, sl] = s_scr[lt, CP_ROWS:CP_ROWS + HIST, :]
            if project:
                g_scr[lt] = glu[:, sl]

    last_i = pl.num_programs(0) - 1

    @pl.when(i == 0)
    def _():
        step(True, False)
        y_ref[...] = jnp.zeros(y_ref.shape, y_ref.dtype)
        tail_ref[...] = jnp.zeros(tail_ref.shape, f32)

    @pl.when((i > 0) & (i < last_i))
    def _():
        step(True, True)

    @pl.when(i == last_i)
    def _():
        step(False, True)


def _conf_proj(u, w_t, ca_row0, hist, conf_w, sides, *, n_seq, seq_rows):
    w3, p3 = conf_w
    n_tiles = n_seq * seq_rows // CP_ROWS
    tiles_per_seq = seq_rows // CP_ROWS
    grid = (n_tiles + 1, D_CONF // CP_COLS)
    side_specs = [_side_cast(*s, grid) for s in sides]
    w_block = lambda row0: pl.BlockSpec(
        (pl.Element(CP_COLS), pl.Element(D_MODEL)), lambda i, j: (pl.multiple_of(row0 + CP_COLS * j, BF16_ROWS), 0))
    slot = lambda i: (i + n_tiles) % (n_tiles + 1)
    return pl.pallas_call(
        functools.partial(_conf_proj_kernel, tiles_per_seq=tiles_per_seq, n_side=len(side_specs)),
        grid=grid,
        in_specs=[
            pl.BlockSpec((CP_ROWS, D_MODEL), lambda i, j: (jnp.minimum(i, n_tiles - 1), 0)),
            w_block(ca_row0),
            w_block(ca_row0 + D_CONF),
            pl.BlockSpec((HIST, D_CONF), lambda i, j: (0, 0)),
            pl.BlockSpec((N_LT, HIST, LANES), lambda i, j: (0, 0, 0)),
            pl.BlockSpec((N_LT, SUBLANES, LANES), lambda i, j: (0, 0, 0)),
        ] + [s[0] for s in side_specs],
        out_specs=(
            pl.BlockSpec((CP_ROWS, CP_COLS), lambda i, j: (slot(i), j)),
            pl.BlockSpec((None, HIST, CP_COLS), lambda i, j: (slot(i), 0, j)),
        ) + tuple(s[1] for s in side_specs),
        out_shape=(
            jax.ShapeDtypeStruct(((n_tiles + 1) * CP_ROWS, D_CONF), bf16),
            jax.ShapeDtypeStruct((n_tiles + 1, HIST, D_CONF), f32),
        ) + tuple(s[2] for s in side_specs),
        scratch_shapes=[
            pltpu.VMEM((N_LT, HIST + CP_ROWS, LANES), f32),
            pltpu.VMEM((N_LT, CP_ROWS, LANES), f32),
            pltpu.VMEM((CP_COLS // LANES, CP_ROWS, LANES), f32),
        ],
        compiler_params=_cparams(("arbitrary", "arbitrary")),
        name="conf_proj",
    )(u, w_t, w_t, hist, w3, p3, *[s[0] for s in sides])


CONF_BS = 16


def _conf_sample_kernel(ca_ref, cg_ref, buf_ref, w_ref, p_ref, y_ref, newbuf_ref):
    n_hist = CONF_KERNEL - 1
    n_tok = ca_ref.shape[0]
    glu = [ca_ref[t] * _sigmoid(cg_ref[t]) for t in range(n_tok)]
    win = lambda j: buf_ref[j] if j < n_hist else glu[j - n_hist]
    for j in range(n_hist):
        newbuf_ref[j] = win(j + n_tok)
    for t in range(n_tok):
        acc = jnp.broadcast_to(p_ref[0:1, :], (CONF_BS, D_CONF))
        for k in range(CONF_KERNEL):
            acc = acc + w_ref[k:k + 1, :] * win(t + k)
        for lt in range(N_LT):
            sl = slice(lt * LANES, (lt + 1) * LANES)
            y_ref[t, :, sl] = _conf_norm_act(acc[:, sl], p_ref[1:2, sl], p_ref[2:3, sl]).astype(bf16)


def _conf_sample(pc_t, buf_t, w2, p2, *, n_tok, tok_block0):
    n_seq = pc_t.shape[1]
    n_hist = CONF_KERNEL - 1
    seq_blocks = lambda col: (lambda i: (0, i, col))
    return pl.pallas_call(
        _conf_sample_kernel,
        grid=(n_seq // CONF_BS,),
        in_specs=[
            pl.BlockSpec((n_tok, CONF_BS, D_CONF), lambda i: (tok_block0, i, 0)),
            pl.BlockSpec((n_tok, CONF_BS, D_CONF), lambda i: (tok_block0, i, 1)),
            pl.BlockSpec((n_hist, CONF_BS, D_CONF), seq_blocks(0)),
            pl.BlockSpec((CONF_KERNEL, D_CONF), lambda i: (0, 0)),
            pl.BlockSpec((3, D_CONF), lambda i: (0, 0)),
        ],
        out_specs=(
            pl.BlockSpec((n_tok, CONF_BS, D_CONF), seq_blocks(0)),
            pl.BlockSpec((n_hist, CONF_BS, D_CONF), seq_blocks(0)),
        ),
        out_shape=(
            jax.ShapeDtypeStruct((n_tok, n_seq, D_CONF), bf16),
            jax.ShapeDtypeStruct((n_hist, n_seq, D_CONF), f32),
        ),
        compiler_params=_cparams(("arbitrary",)),
        name="conf_sample",
    )(pc_t, pc_t, buf_t, w2, p2)


def _seq_tiles(rows, n_seq, n_tok):
    w = rows.shape[-1]
    return jnp.pad(rows.reshape(n_seq, n_tok, w), ((0, 0), (SEQ_TILE - n_tok, 0), (0, 0))).reshape(n_seq * SEQ_TILE, w)


def kernel(x_prompt, x_sample, state_ssm, state_ssm_conv, state_conf_conv, meta_tokens, ffn1_norm, ffn1_w_gate, ffn1_w_up, ffn1_w_down, mix_norm, w_in, ssm_conv_w, ssm_conv_b, ssm_dt_bias, ssm_A_log, ssm_D, ssm_norm, conf_conv_w, conf_conv_b, conf_ln_g, conf_ln_b, w_out, ffn2_norm, ffn2_w_gate, ffn2_w_up, ffn2_w_down, final_norm):
    n_pb, n_pt = x_prompt.shape[0], x_prompt.shape[1]
    n_sb, n_st = x_sample.shape[0], x_sample.shape[1]
    assert (n_pb * n_pt, n_sb * n_st, n_st) == (N_PROMPT_ROWS, N_SAMPLE_ROWS, 4) and n_pt % CHUNK == 0

    row = lambda v: v.reshape(1, -1)
    swap01 = lambda v: jnp.transpose(v, (1, 0, 2))
    x_rows = (x_prompt.reshape(N_PROMPT_ROWS, D_MODEL), swap01(x_sample).reshape(N_SAMPLE_ROWS, D_MODEL), meta_tokens)
    h, u, w_t, wg2 = _ffn(x_rows, row(ffn1_norm[0]), _to_bf16(ffn1_w_gate, 256), _to_bf16(ffn1_w_up, 256),
                          _to_bf16(ffn1_w_down, 704), row(mix_norm[0]),
                          sides=((jnp.transpose(w_in[0]), 96), (ffn2_w_gate, 16)))

    c_in0 = D_SSM + CONV_DIM
    n_small = N_ROWS - ROW_SAMPLE
    small = (ROW_SAMPLE, n_small)
    pzx = _matmul_nt(u, w_t, row0=0, n=c_in0, tn=1536, a_rows=small)
    pdt = _matmul_nt(u, w_t, row0=c_in0, n=LANES, tn=LANES, a_rows=small)
    pc = _matmul_nt(u, w_t, row0=c_in0 + N_HEADS, n=2 * D_CONF, tn=1024, a_rows=small)

    pad_heads = lambda v: jnp.pad(v.reshape(1, N_HEADS), ((0, 0), (0, LANES - N_HEADS)))
    ssm_w = (ssm_conv_w[0], row(ssm_conv_b[0]), pad_heads(ssm_dt_bias[0]), pad_heads(ssm_A_log[0]),
             row(jnp.repeat(ssm_D[0], HEAD_DIM)), row(ssm_norm[0]))
    tri = jnp.tril(jnp.ones((CHUNK, CHUNK), f32))
    e_mat = (jnp.arange(LANES)[:, None] == (jnp.arange(D_SSM)[None, :] // HEAD_DIM)).astype(bf16)
    lane_major = lambda v: v.reshape(v.shape[0], N_LT, LANES).transpose(1, 0, 2)
    conf_w = (lane_major(jnp.pad(conf_conv_w[0], ((0, HIST - CONF_KERNEL), (0, 0)))),
              lane_major(jnp.pad(jnp.stack([conf_conv_b[0], conf_ln_g[0], conf_ln_b[0]]), ((0, SUBLANES - 3), (0, 0)))))
    out_rows = jnp.arange(SSD_BS * 4)
    sel = ((out_rows // 4) * SEQ_TILE + TOK0 + out_rows % 4)[:, None] == jnp.arange(SSD_BS * SEQ_TILE)[None, :]
    sel = sel.astype(bf16)

    meta_block = (ROW_META - ROW_SAMPLE) // CHUNK
    ys_m, h_meta, tail_meta = _ssd_chunks(
        pzx, pdt, jnp.zeros((SUBLANES, CONV_DIM), f32), jnp.zeros((D_STATE, D_SSM), f32), ssm_w, (tri, e_mat),
        n_seq=1, n_chunks=1, row_block0=meta_block, n_valid=N_META, transpose_out=False)
    yc_m, hist_meta = _conf_tiles(pc, jnp.zeros((HIST, D_CONF), f32), conf_w, n_seq=1, n_tiles=1, tt=CHUNK,
                                  row_block0=meta_block, n_valid=N_META)

    ys_p, p_ssm, p_tail, wo = _ssd_proj(u, w_t, tail_meta[0], h_meta[0], ssm_w, (tri, e_mat), (w_out, 64),
                                        n_seq=n_pb, chunks_per_seq=n_pt // CHUNK)
    p_ssm, p_tail = p_ssm[:n_pb], p_tail[:n_pb]
    yc_p, tile_hist, wu2, wd2 = _conf_proj(u, w_t, c_in0 + N_HEADS, hist_meta[0], conf_w,
                                           ((ffn2_w_up, 64), (ffn2_w_down, 256)), n_seq=n_pb, seq_rows=n_pt)
    tiles_per_seq = n_pt // CP_ROWS
    p_hist = tile_hist[tiles_per_seq - 1:n_pb * tiles_per_seq:tiles_per_seq]

    smp = slice(0, N_SAMPLE_ROWS)
    seq_major = lambda v: swap01(v.reshape(n_st, n_sb, v.shape[-1]))
    pzx_s = seq_major(pzx[smp])
    pxbc_s = pzx_s[:, :, D_SSM:]
    pxbc8 = jnp.concatenate([jnp.zeros((n_sb, 1, CONV_DIM), f32), state_ssm_conv[0], pxbc_s], axis=1)
    ys_s, s_ssm = _ssd_sample(_seq_tiles(pzx_s[:, :, :D_SSM], n_sb, n_st), pxbc8.reshape(n_sb * SEQ_TILE, CONV_DIM),
                              _seq_tiles(seq_major(pdt[smp]), n_sb, n_st), state_ssm[0].reshape(n_sb, D_SSM, D_STATE),
                              ssm_w, e_mat, sel)
    ys_s = swap01(ys_s.reshape(n_sb, n_st, D_SSM)).reshape(N_SAMPLE_ROWS, D_SSM)
    yc_t, cconv_t = _conf_sample(pc.reshape(n_small // n_sb, n_sb, 2 * D_CONF), swap01(state_conf_conv[0]),
                                 conf_conv_w[0], jnp.stack([conf_conv_b[0], conf_ln_g[0], conf_ln_b[0]]),
                                 n_tok=n_st, tok_block0=0)
    yc_s = yc_t.reshape(N_SAMPLE_ROWS, D_CONF)
    s_cconv = swap01(cconv_t)

    h2 = _outproj(h, (ys_p, ys_s, ys_m), (yc_p, yc_s, yc_m), wo, main_rows=N_PROMPT_ROWS, tm=TM, tn=1024)
    y_prompt, y_sample = _ffn((h2,), row(ffn2_norm[0]), wg2, wu2, wd2, row(final_norm),
                              dst_rows=(N_PROMPT_ROWS, N_SAMPLE_ROWS))
    n_hist = CONF_KERNEL - 1
    return (
        y_prompt.reshape(n_pb, n_pt, D_MODEL), swap01(y_sample.reshape(n_st, n_sb, D_MODEL)),
        p_ssm.reshape(1, n_pb, N_HEADS, HEAD_DIM, D_STATE),
        p_tail[:, SUBLANES - (SSM_CONV - 1):, :][None],
        p_hist[:, HIST - n_hist:, :][None],
        s_ssm.reshape(1, n_sb, N_HEADS, HEAD_DIM, D_STATE),
        pxbc_s[:, n_st - (SSM_CONV - 1):, :][None],
        s_cconv[None],
    )
```
